```python
import jax, jax.numpy as jnp
from jax import lax
import numpy as np

D_MODEL = 4096
BATCH = 2
SEQ = 8192
DEPTH = 4

CHUNK = 64
Q_BLOCK = 128
H_A = 16
DH_A = 128
W_A = H_A * DH_A
R_KV = 256
H_IDX = 32
D_IDX = 64
TOPK_MAX = 256
H_B = 16
N_B = 64
W_B = H_B * N_B
W_LORA = 64
A_LORA = 64
G_LORA = 160
V_LORA = 32
RWKV_SIZES = (W_B, W_B, W_B, W_LORA, A_LORA, G_LORA)
RWKV_COLS = sum(RWKV_SIZES)
LNX_EPS = 64e-5
N_MEM = 256
H_M = 4
DH_M = 256
W_M = H_M * DH_M
N_BRANCH = 3
IN_SIZES = (W_A, R_KV, H_IDX * D_IDX, D_IDX, H_IDX, RWKV_COLS, W_M, N_BRANCH * D_MODEL)
N_IN = sum(IN_SIZES)
N_EXPERTS = 64
N_GROUPS = 8
TOPK_GROUPS = 4
TOP_K = 8
F_EXPERT = 96
F_SHARED = 96
ROUTED_SCALE = 2.5
ALPHA = (2 * DEPTH) ** 0.25
BETA = (8 * DEPTH) ** -0.25
LN_EPS = 1e-5

kernel_name = "hybrid_dsa_rwkv7_mem_moe_deepnorm"


def layer_norm(x, g, b):
    xf = x.astype(jnp.float32)
    mu = jnp.mean(xf, axis=-1, keepdims=True)
    var = jnp.mean(jnp.square(xf - mu), axis=-1, keepdims=True)
    return ((xf - mu) * lax.rsqrt(var + LN_EPS) * g + b).astype(x.dtype)


def rms_norm(x, g):
    xf = x.astype(jnp.float32)
    return (xf * lax.rsqrt(jnp.mean(jnp.square(xf), axis=-1, keepdims=True) + LN_EPS) * g).astype(x.dtype)


def split_cols(z, sizes):
    return jnp.split(z, np.cumsum(sizes)[:-1].tolist(), axis=-1)


def token_shift(z, mu):
    prev = jnp.pad(z, ((0, 0), (1, 0), (0, 0)))[:, :-1]
    return z + (prev - z) * mu


def dsa_branch(q, c_kv, q_idx, k_idx, w_idx, ckv_g, w_uk, w_uv):
    B, S, _ = q.shape
    k_sel = min(TOPK_MAX, S // 4)
    ckv = rms_norm(c_kv, ckv_g)
    q = q.reshape(B, S, H_A, DH_A)
    k = (ckv @ w_uk).reshape(B, S, H_A, DH_A)
    v = (ckv @ w_uv).reshape(B, S, H_A, DH_A)
    qi = q_idx.reshape(B, S, H_IDX, D_IDX).astype(jnp.float32) * (D_IDX ** -0.5)
    ki = k_idx.astype(jnp.float32)
    wi = w_idx.astype(jnp.float32) * (H_IDX ** -0.5)
    key_chunk = jnp.arange(S) // CHUNK

    def block(blk):
        start = blk * Q_BLOCK
        qi_b = lax.dynamic_slice_in_dim(qi, start, Q_BLOCK, axis=1)
        wi_b = lax.dynamic_slice_in_dim(wi, start, Q_BLOCK, axis=1)
        q_b = lax.dynamic_slice_in_dim(q, start, Q_BLOCK, axis=1)
        q_chunk = (start + jnp.arange(Q_BLOCK)) // CHUNK
        admissible = key_chunk[None, :] <= q_chunk[:, None]
        rel = jax.nn.relu(jnp.einsum('bqhd,bsd->bqhs', qi_b, ki))
        score = jnp.einsum('bqhs,bqh->bqs', rel, wi_b)
        score = jnp.where(admissible[None], score, -jnp.inf)
        _, sel = lax.top_k(score, k_sel)
        valid = (sel // CHUNK) <= q_chunk[None, :, None]
        k_g = jax.vmap(lambda kb, ib: jnp.take(kb, ib, axis=0))(k, sel)
        v_g = jax.vmap(lambda vb, ib: jnp.take(vb, ib, axis=0))(v, sel)
        logits = jnp.einsum('bqhd,bqkhd->bhqk', q_b, k_g).astype(jnp.float32) * (DH_A ** -0.5)
        logits = jnp.where(valid[:, None], logits, -jnp.inf)
        p = jax.nn.softmax(logits, axis=-1).astype(v_g.dtype)
        return jnp.einsum('bhqk,bqkhd->bqhd', p, v_g)

    out = lax.map(block, jnp.arange(S // Q_BLOCK))
    return out.transpose(1, 0, 2, 3, 4).reshape(B, S, W_A)


def wkv7_scan(r, decay, k, v, kk, a):
    B, S, H, N = r.shape

    def step(state, inp):
        r_t, w_t, k_t, v_t, kk_t, a_t = inp
        sa = jnp.einsum('bhvi,bhi->bhv', state, -kk_t)
        state = (state * w_t[:, :, None, :]
                 + sa[..., None] * (kk_t * a_t)[:, :, None, :]
                 + v_t[..., None] * k_t[:, :, None, :])
        return state, jnp.einsum('bhvi,bhi->bhv', state, r_t)

    xs = tuple(jnp.moveaxis(z, 1, 0) for z in (r, decay, k, v, kk, a))
    _, out = lax.scan(step, jnp.zeros((B, H, N, N), jnp.float32), xs)
    return jnp.moveaxis(out, 0, 1)


def rwkv7_branch(r, k, v, xw, xa, xg, w0, w_up, a0, a_up, g_up, k_k, k_a, r_k, lnx_g, lnx_b):
    B, S, _ = r.shape
    heads = lambda z: z.reshape(B, S, H_B, N_B)
    w_log = -jax.nn.softplus(-(w0 + jnp.tanh(xw) @ w_up)) - 0.5
    decay = jnp.exp(-jnp.exp(w_log))
    a = jax.nn.sigmoid(a0 + xa @ a_up)
    g = jax.nn.sigmoid(xg) @ g_up
    kk = heads(k * k_k)
    kk = kk / jnp.maximum(jnp.sqrt(jnp.sum(jnp.square(kk), axis=-1, keepdims=True)), 1e-12)
    k = k * (1.0 + (a - 1.0) * k_a)
    rh, kh, vh = heads(r), heads(k), heads(v)
    o = wkv7_scan(rh, heads(decay), kh, vh, kk, heads(a))
    mu = jnp.mean(o, axis=-1, keepdims=True)
    var = jnp.mean(jnp.square(o - mu), axis=-1, keepdims=True)
    o = ((o - mu) * lax.rsqrt(var + LNX_EPS)).reshape(B, S, W_B) * lnx_g + lnx_b
    bonus = jnp.sum(rh * kh * r_k, axis=-1, keepdims=True) * vh
    return (o + bonus.reshape(B, S, W_B)) * g


def memory_branch(q_m, mem, w_mk, w_mv):
    B, S, _ = q_m.shape
    q = q_m.reshape(B, S, H_M, DH_M)
    km = (mem @ w_mk).reshape(B, N_MEM, H_M, DH_M)
    vm = (mem @ w_mv).reshape(B, N_MEM, H_M, DH_M)
    logits = jnp.einsum('bshd,bmhd->bhsm', q, km).astype(jnp.float32) * (DH_M ** -0.5)
    p = jax.nn.softmax(logits, axis=-1).astype(vm.dtype)
    return jnp.einsum('bhsm,bmhd->bshd', p, vm).reshape(B, S, W_M)


def moe_ffn(h, w_router, router_bias, w_e1, w_e3, w_e2, w_s1, w_s3, w_s2):
    B, S, D = h.shape
    t = h.reshape(B * S, D)
    T = t.shape[0]
    scores = jax.nn.sigmoid((t @ w_router).astype(jnp.float32))
    biased = scores + router_bias.astype(jnp.float32)
    grp = biased.reshape(T, N_GROUPS, N_EXPERTS // N_GROUPS)
    group_score = jnp.sum(lax.top_k(grp, 2)[0], axis=-1)
    _, gidx = lax.top_k(group_score, TOPK_GROUPS)
    gmask = jnp.sum(jax.nn.one_hot(gidx, N_GROUPS, dtype=jnp.float32), axis=1)
    emask = jnp.repeat(gmask, N_EXPERTS // N_GROUPS, axis=1)
    _, eidx = lax.top_k(jnp.where(emask > 0, biased, -jnp.inf), TOP_K)
    sel = jnp.take_along_axis(scores, eidx, axis=1)
    wts = sel / jnp.sum(sel, axis=-1, keepdims=True) * ROUTED_SCALE
    gate = jnp.sum(jax.nn.one_hot(eidx, N_EXPERTS, dtype=jnp.float32) * wts[..., None], axis=1)
    h1 = jnp.einsum('td,edf->tef', t, w_e1)
    h3 = jnp.einsum('td,edf->tef', t, w_e3)
    act = jax.nn.silu(h1) * h3 * gate[..., None].astype(h1.dtype)
    routed = jnp.einsum('tef,efd->td', act, w_e2)
    shared = (jax.nn.silu(t @ w_s1) * (t @ w_s3)) @ w_s2
    return (routed + shared).reshape(B, S, D)


def setup_inputs(seed: int = 0) -> dict:
    key = jax.random.key(seed)
    ks = jax.random.split(key, 48)

    def nrm(i, shape, scale):
        return jax.random.normal(ks[i], shape, jnp.float32) * scale

    def unif(i, shape, lo, hi):
        return jax.random.uniform(ks[i], shape, jnp.float32, lo, hi)

    L = DEPTH
    Lr = DEPTH - 1
    D = D_MODEL
    return {
        "x": nrm(0, (BATCH, SEQ, D), 1.0),
        "mem": nrm(1, (BATCH, N_MEM, D), 1.0),
        "ln0_g": 1.0 + nrm(2, (D,), 0.02),
        "ln0_b": nrm(3, (D,), 0.02),
        "w_in_first": nrm(4, (D, N_IN), D ** -0.5),
        "w_in_rest": nrm(5, (Lr, D, N_IN + V_LORA), D ** -0.5),
        "ckv_g": 1.0 + nrm(6, (L, R_KV), 0.02),
        "w_uk": nrm(7, (L, R_KV, W_A), R_KV ** -0.5),
        "w_uv": nrm(8, (L, R_KV, W_A), R_KV ** -0.5),
        "mu_rwkv": unif(9, (L, RWKV_COLS), 0.0, 1.0),
        "mu_vres": unif(10, (Lr, V_LORA), 0.0, 1.0),
        "rw_w0": unif(11, (L, W_B), -6.0, 1.0),
        "rw_w_up": nrm(12, (L, W_LORA, W_B), 0.5 * W_LORA ** -0.5),
        "rw_a0": nrm(13, (L, W_B), 0.5),
        "rw_a_up": nrm(14, (L, A_LORA, W_B), 0.5 * A_LORA ** -0.5),
        "rw_g_up": nrm(15, (L, G_LORA, W_B), G_LORA ** -0.5),
        "rw_v0": nrm(16, (Lr, W_B), 0.5),
        "rw_v_up": nrm(17, (Lr, V_LORA, W_B), 0.5 * V_LORA ** -0.5),
        "rw_k_k": 0.85 + nrm(18, (L, W_B), 0.05),
        "rw_k_a": 1.0 + nrm(19, (L, W_B), 0.05),
        "rw_r_k": nrm(20, (L, H_B, N_B), 0.1),
        "rw_lnx_g": 1.0 + nrm(21, (L, W_B), 0.02),
        "rw_lnx_b": nrm(22, (L, W_B), 0.02),
        "w_mk": nrm(23, (L, D, W_M), D ** -0.5),
        "w_mv": nrm(24, (L, D, W_M), D ** -0.5),
        "w_pa": nrm(25, (L, W_A, D), W_A ** -0.5),
        "w_pb": nrm(26, (L, W_B, D), W_B ** -0.5),
        "w_pc": nrm(27, (L, W_M, D), W_M ** -0.5),
        "w_o": nrm(28, (L, D, D), BETA * D ** -0.5),
        "ln1_g": 1.0 + nrm(29, (L, D), 0.02),
        "ln1_b": nrm(30, (L, D), 0.02),
        "w_router": nrm(31, (L, D, N_EXPERTS), D ** -0.5),
        "router_bias": nrm(32, (L, N_EXPERTS), 0.01),
        "w_e1": nrm(33, (L, N_EXPERTS, D, F_EXPERT), D ** -0.5),
        "w_e3": nrm(34, (L, N_EXPERTS, D, F_EXPERT), D ** -0.5),
        "w_e2": nrm(35, (L, N_EXPERTS, F_EXPERT, D), BETA * F_EXPERT ** -0.5),
        "w_s1": nrm(36, (L, D, F_SHARED), D ** -0.5),
        "w_s3": nrm(37, (L, D, F_SHARED), D ** -0.5),
        "w_s2": nrm(38, (L, F_SHARED, D), BETA * F_SHARED ** -0.5),
        "ln2_g": 1.0 + nrm(39, (L, D), 0.02),
        "ln2_b": nrm(40, (L, D), 0.02),
    }


def reference(x, mem, ln0_g, ln0_b, w_in_first, w_in_rest, ckv_g, w_uk, w_uv, mu_rwkv, mu_vres,
              rw_w0, rw_w_up, rw_a0, rw_a_up, rw_g_up, rw_v0, rw_v_up, rw_k_k, rw_k_a, rw_r_k,
              rw_lnx_g, rw_lnx_b, w_mk, w_mv, w_pa, w_pb, w_pc, w_o, ln1_g, ln1_b,
              w_router, router_bias, w_e1, w_e3, w_e2, w_s1, w_s3, w_s2, ln2_g, ln2_b):
    B, S, D = x.shape
    h = layer_norm(x, ln0_g, ln0_b)
    v_first = None
    for i in range(DEPTH):
        if i == 0:
            proj = h @ w_in_first
            parts = split_cols(proj, IN_SIZES)
        else:
            proj = h @ w_in_rest[i - 1]
            parts = split_cols(proj, IN_SIZES + (V_LORA,))
        q_a, c_kv, q_idx, k_idx, w_idx, rwkv_cols, q_m, gate_logits = parts[:8]

        o_a = dsa_branch(q_a, c_kv, q_idx, k_idx, w_idx, ckv_g[i], w_uk[i], w_uv[i])

        rw = token_shift(rwkv_cols.astype(jnp.float32), mu_rwkv[i])
        r_b, k_b, v_b, xw, xa, xg = split_cols(rw, RWKV_SIZES)
        if i == 0:
            v_first = v_b
        else:
            xv_lo = token_shift(parts[8].astype(jnp.float32), mu_vres[i - 1])
            v_b = v_b + (v_first - v_b) * jax.nn.sigmoid(rw_v0[i - 1] + xv_lo @ rw_v_up[i - 1])
        o_b = rwkv7_branch(r_b, k_b, v_b, xw, xa, xg, rw_w0[i], rw_w_up[i], rw_a0[i], rw_a_up[i],
                           rw_g_up[i], rw_k_k[i], rw_k_a[i], rw_r_k[i], rw_lnx_g[i], rw_lnx_b[i])

        o_c = memory_branch(q_m, mem, w_mk[i], w_mv[i])

        gates = jax.nn.sigmoid(gate_logits.astype(jnp.float32)).reshape(B, S, N_BRANCH, D).astype(h.dtype)
        merged = (gates[:, :, 0] * (o_a @ w_pa[i])
                  + gates[:, :, 1] * (o_b.astype(h.dtype) @ w_pb[i])
                  + gates[:, :, 2] * (o_c @ w_pc[i]))
        h = layer_norm(ALPHA * h + merged @ w_o[i], ln1_g[i], ln1_b[i])

        ffn = moe_ffn(h, w_router[i], router_bias[i], w_e1[i], w_e3[i], w_e2[i], w_s1[i], w_s3[i], w_s2[i])
        h = layer_norm(ALPHA * h + ffn, ln2_g[i], ln2_b[i])
    return h
```

```python
import functools

import jax
import jax.numpy as jnp
import numpy as np
from jax import lax
from jax.experimental import pallas as pl
from jax.experimental.pallas import tpu as pltpu

F32 = jnp.float32
BF16 = jnp.bfloat16

D_MODEL = 4096
DEPTH = 4
CHUNK = 64
H_A, DH_A = 16, 128
W_A = H_A * DH_A
R_KV = 256
H_IDX, D_IDX = 32, 64
TOPK_MAX = 256
H_B, N_B = 16, 64
W_B = H_B * N_B
W_LORA, A_LORA, G_LORA, V_LORA = 64, 64, 160, 32
LNX_EPS = 64e-5
N_MEM = 256
H_M, DH_M = 4, 256
W_M = H_M * DH_M
N_EXPERTS, N_GROUPS, TOPK_GROUPS, TOP_K = 64, 8, 4, 8
F_EXPERT = 96
ROUTED_SCALE = 2.5
ALPHA = (2 * DEPTH) ** 0.25
LN_EPS = 1e-5

OFF_GATE = 0
OFF_QA = 3 * D_MODEL
OFF_QIDX = OFF_QA + W_A
OFF_RKV = OFF_QIDX + H_IDX * D_IDX
OFF_QM = OFF_RKV + 3 * W_B
OFF_CKV = OFF_QM + W_M
OFF_LORA = OFF_CKV + R_KV
LORA_W = 384
OFF_IDX = OFF_LORA + LORA_W
IDX_W = 128
N_PROJ = 21504

NEG_BIAS = -1e30
VMEM_LIMIT = 56 * 1024 * 1024


def _cparams(sem):
    return pltpu.CompilerParams(dimension_semantics=sem, vmem_limit_bytes=VMEM_LIMIT)


def _split2(x):
    hi = x.astype(BF16)
    lo = (x - hi.astype(F32)).astype(BF16)
    return hi, lo


def _split3(x):
    hi = x.astype(BF16)
    r1 = x - hi.astype(F32)
    mid = r1.astype(BF16)
    lo = (r1 - mid.astype(F32)).astype(BF16)
    return hi, mid, lo


def _mm_kernel(x_ref, w_ref, o_ref, acc_ref, *, nk):
    k = pl.program_id(2)

    @pl.when(k == 0)
    def _():
        acc_ref[...] = jnp.zeros_like(acc_ref)

    acc_ref[...] += jnp.dot(x_ref[...], w_ref[...], preferred_element_type=F32)

    @pl.when(k == nk - 1)
    def _():
        o_ref[...] = acc_ref[...].astype(o_ref.dtype)


def matmul(x, w, out_dtype, tm, tn, tk):
    M, K = x.shape
    N = w.shape[1]
    tm, tn, tk = min(tm, M), min(tn, N), min(tk, K)
    assert M % tm == 0 and N % tn == 0 and K % tk == 0
    nk = K // tk
    return pl.pallas_call(
        functools.partial(_mm_kernel, nk=nk),
        out_shape=jax.ShapeDtypeStruct((M, N), out_dtype),
        grid=(M // tm, N // tn, nk),
        in_specs=[pl.BlockSpec((tm, tk), lambda i, j, k: (i, k)),
                  pl.BlockSpec((tk, tn), lambda i, j, k: (k, j))],
        out_specs=pl.BlockSpec((tm, tn), lambda i, j, k: (i, j)),
        scratch_shapes=[pltpu.VMEM((tm, tn), F32)],
        compiler_params=_cparams(("parallel", "parallel", "arbitrary")),
        name="matmul",
    )(x, w)


def _ln_rows(y, g, b):
    mu = jnp.mean(y, axis=-1, keepdims=True)
    yc = y - mu
    var = jnp.mean(yc * yc, axis=-1, keepdims=True)
    return yc * lax.rsqrt(var + LN_EPS) * g + b


def _ln_kernel(x_ref, g_ref, b_ref, o_ref, ob_ref):
    y = _ln_rows(x_ref[...], g_ref[...], b_ref[...])
    o_ref[...] = y
    ob_ref[...] = y.astype(BF16)


def _res_ln_kernel(h_ref, f_ref, g_ref, b_ref, o_ref, ob_ref):
    y = _ln_rows(ALPHA * h_ref[...] + f_ref[...], g_ref[...], b_ref[...])
    o_ref[...] = y
    ob_ref[...] = y.astype(BF16)


def layer_norm(x, g, b, tm=256):
    M, D = x.shape
    row = pl.BlockSpec((tm, D), lambda i: (i, 0))
    vec = pl.BlockSpec((1, D), lambda i: (0, 0))
    return pl.pallas_call(
        _ln_kernel,
        out_shape=(jax.ShapeDtypeStruct((M, D), F32), jax.ShapeDtypeStruct((M, D), BF16)),
        grid=(M // tm,),
        in_specs=[row, vec, vec],
        out_specs=(row, row),
        compiler_params=_cparams(("parallel",)),
        name="layer_norm",
    )(x, g.reshape(1, D), b.reshape(1, D))


def residual_layer_norm(h, f, g, b, tm=256):
    M, D = h.shape
    row = pl.BlockSpec((tm, D), lambda i: (i, 0))
    vec = pl.BlockSpec((1, D), lambda i: (0, 0))
    return pl.pallas_call(
        _res_ln_kernel,
        out_shape=(jax.ShapeDtypeStruct((M, D), F32), jax.ShapeDtypeStruct((M, D), BF16)),
        grid=(M // tm,),
        in_specs=[row, row, vec, vec],
        out_specs=(row, row),
        compiler_params=_cparams(("parallel",)),
        name="residual_layer_norm",
    )(h, f, g.reshape(1, D), b.reshape(1, D))


def _mm_res_ln_kernel(x_ref, w_ref, h_ref, g_ref, b_ref, o_ref, ob_ref, acc_ref, *, nj, tn):
    j = pl.program_id(1)
    acc_ref[j] = jnp.dot(x_ref[...], w_ref[...], preferred_element_type=F32)

    @pl.when(j == nj - 1)
    def _():
        D = nj * tn
        s1 = None
        for jj in range(nj):
            y = ALPHA * h_ref[:, jj * tn:(jj + 1) * tn] + acc_ref[jj]
            acc_ref[jj] = y
            p = jnp.sum(y, axis=-1, keepdims=True)
            s1 = p if s1 is None else s1 + p
        mu = s1 * (1.0 / D)
        s2 = None
        for jj in range(nj):
            yc = acc_ref[jj] - mu
            p = jnp.sum(yc * yc, axis=-1, keepdims=True)
            s2 = p if s2 is None else s2 + p
        rstd = lax.rsqrt(s2 * (1.0 / D) + LN_EPS)
        for jj in range(nj):
            sl = slice(jj * tn, (jj + 1) * tn)
            y = (acc_ref[jj] - mu) * rstd * g_ref[:, sl] + b_ref[:, sl]
            o_ref[:, sl] = y
            ob_ref[:, sl] = y.astype(BF16)


def matmul_residual_ln(x, w, h, g, b, tm=256, tn=512):
    M, K = x.shape
    D = w.shape[1]
    nj = D // tn
    row = pl.BlockSpec((tm, D), lambda i, j: (i, 0))
    vec = pl.BlockSpec((1, D), lambda i, j: (0, 0))
    return pl.pallas_call(
        functools.partial(_mm_res_ln_kernel, nj=nj, tn=tn),
        out_shape=(jax.ShapeDtypeStruct((M, D), F32), jax.ShapeDtypeStruct((M, D), BF16)),
        grid=(M // tm, nj),
        in_specs=[pl.BlockSpec((tm, K), lambda i, j: (i, 0)),
                  pl.BlockSpec((K, tn), lambda i, j: (0, j)),
                  row, vec, vec],
        out_specs=(row, row),
        scratch_shapes=[pltpu.VMEM((nj, tm, tn), F32)],
        compiler_params=_cparams(("parallel", "arbitrary")),
        name="matmul_residual_ln",
    )(x, w, h, g.reshape(1, D), b.reshape(1, D))


def _kv_kernel(c_ref, g_ref, wk_ref, wv_ref, k_ref, v_ref):
    c = c_ref[...]
    ms = jnp.mean(c * c, axis=-1, keepdims=True)
    cn = (c * lax.rsqrt(ms + LN_EPS) * g_ref[...]).astype(BF16)
    k_ref[...] = jnp.dot(cn, wk_ref[...], preferred_element_type=F32).astype(BF16)
    v_ref[...] = jnp.dot(cn, wv_ref[...], preferred_element_type=F32).astype(BF16)


def kv_project(proj2, ckv_g, w_uk, w_uv, tm=512):
    M = proj2.shape[0]
    out = jax.ShapeDtypeStruct((M, W_A), BF16)
    wspec = pl.BlockSpec((R_KV, W_A), lambda i: (0, 0))
    ospec = pl.BlockSpec((tm, W_A), lambda i: (i, 0))
    return pl.pallas_call(
        _kv_kernel,
        out_shape=(out, out),
        grid=(M // tm,),
        in_specs=[pl.BlockSpec((tm, R_KV), lambda i: (i, OFF_CKV // R_KV)),
                  pl.BlockSpec((1, R_KV), lambda i: (0, 0)), wspec, wspec],
        out_specs=(ospec, ospec),
        compiler_params=_cparams(("parallel",)),
        name="kv_project",
    )(proj2, ckv_g.reshape(1, R_KV), w_uk, w_uv)


IDX_TQ = 128
IDX_TK = 256
KEY_NEG_INF = -2139095041
INT_MIN = -2147483648


def _indexer_kernel(q_ref, iq_ref, ik_ref, bias_ref, q2_ref, wib_ref, key_ref, *, n_kt, ksel):
    tq, tk = IDX_TQ, IDX_TK
    i = pl.program_id(1)
    q = q_ref[...]
    wi = iq_ref[...][:, D_IDX:D_IDX + H_IDX] * (H_IDX ** -0.5)
    for h in range(H_IDX):
        q2_ref[h] = (q[:, h * D_IDX:(h + 1) * D_IDX] * (D_IDX ** -0.5)).astype(BF16)
        wib_ref[h] = jnp.broadcast_to(wi[:, h:h + 1], (tq, 128))

    n_t = ((i + 1) * tq + tk - 1) // tk
    q_chunk = (i * tq + lax.broadcasted_iota(jnp.int32, (tq, tk), 0)) // CHUNK
    col_iota = lax.broadcasted_iota(jnp.int32, (tq, tk), 1)

    def score_tile(j, carry):
        kt = ik_ref[pl.ds(pl.multiple_of(j * tk, tk), tk), :][:, :D_IDX].astype(BF16)
        acc = jnp.zeros((tq, tk), F32)
        for h in range(H_IDX):
            z = lax.dot_general(q2_ref[h], kt, (((1,), (1,)), ((), ())),
                                preferred_element_type=F32)
            w = wib_ref[h]
            acc = acc + jnp.concatenate([w] * (tk // 128), axis=1) * jnp.maximum(z, 0.0)
        admissible = ((j * tk + col_iota) // CHUNK) <= q_chunk
        score = jnp.where(admissible, acc, -jnp.inf)
        bits = lax.bitcast_convert_type(score, jnp.int32)
        key_ref[j] = bits ^ ((bits >> 31) & 0x7FFFFFFF)
        return carry

    lax.fori_loop(0, n_t, score_tile, 0)

    def bit_step(it, thr):
        cand = thr + lax.shift_left(jnp.int32(1), 31 - it)
        cand_b = jnp.concatenate([cand] * (tk // 128), axis=1)

        def count_tile(j, cnt):
            c = (key_ref[j] >= cand_b).astype(jnp.int32)
            for s in range(tk // 128):
                cnt = cnt + c[:, s * 128:(s + 1) * 128]
            return cnt

        cnt = lax.fori_loop(0, n_t, count_tile, jnp.zeros((tq, 128), jnp.int32))
        total = jnp.sum(cnt.astype(F32), axis=1, keepdims=True)
        return jnp.where(total >= float(ksel), cand, thr)

    thr = lax.fori_loop(0, 32, bit_step, jnp.full((tq, 128), INT_MIN, jnp.int32))
    thr_b = jnp.concatenate([thr] * (tk // 128), axis=1)

    def bias_tile(j, carry):
        key = key_ref[j]
        sel = (key >= thr_b) & (key > KEY_NEG_INF)
        bias_ref[j] = jnp.where(sel, 0.0, NEG_BIAS).astype(BF16)
        return carry

    lax.fori_loop(0, n_t, bias_tile, 0)

    def fill_tile(j, carry):
        bias_ref[j] = jnp.full((tq, tk), NEG_BIAS, BF16)
        return carry

    lax.fori_loop(n_t, n_kt, fill_tile, 0)


def indexer_bias(proj3):
    B, S, _ = proj3.shape
    tq, tk = IDX_TQ, IDX_TK
    n_qt, n_kt = S // tq, S // tk
    ksel = min(TOPK_MAX, S // 4)
    return pl.pallas_call(
        functools.partial(_indexer_kernel, n_kt=n_kt, ksel=ksel),
        out_shape=jax.ShapeDtypeStruct((B, n_qt, n_kt, tq, tk), BF16),
        grid=(B, n_qt),
        in_specs=[pl.BlockSpec((None, tq, H_IDX * D_IDX), lambda b, i: (b, i, OFF_QIDX // (H_IDX * D_IDX))),
                  pl.BlockSpec((None, tq, IDX_W), lambda b, i: (b, i, OFF_IDX // IDX_W)),
                  pl.BlockSpec((None, S, IDX_W), lambda b, i: (b, 0, OFF_IDX // IDX_W))],
        out_specs=pl.BlockSpec((None, None, n_kt, tq, tk), lambda b, i: (b, i, 0, 0, 0)),
        scratch_shapes=[pltpu.VMEM((H_IDX, tq, D_IDX), BF16),
                        pltpu.VMEM((H_IDX, tq, 128), F32),
                        pltpu.VMEM((n_kt, tq, tk), jnp.int32)],
        compiler_params=_cparams(("parallel", "parallel")),
        name="indexer_bias",
    )(proj3, proj3, proj3)


ATT_TQ = 256
ATT_HP = 2


def _attention_kernel(q_ref, k_ref, v_ref, bias_ref, o_ref):
    tq, tk, hp = ATT_TQ, IDX_TK, ATT_HP
    nq = tq // IDX_TQ
    i = pl.program_id(2)
    n_t = ((i + 1) * tq + tk - 1) // tk
    scale = DH_A ** -0.5
    qs = [[(q_ref[qq * IDX_TQ:(qq + 1) * IDX_TQ, hh * DH_A:(hh + 1) * DH_A] * scale).astype(BF16)
           for qq in range(nq)] for hh in range(hp)]

    def body(j, carry):
        row0 = pl.multiple_of(j * tk, tk)
        kt = k_ref[pl.ds(row0, tk), :]
        vt = v_ref[pl.ds(row0, tk), :]
        new = []
        idx = 0
        for hh in range(hp):
            kh = kt[:, hh * DH_A:(hh + 1) * DH_A]
            vh = vt[:, hh * DH_A:(hh + 1) * DH_A]
            for qq in range(nq):
                m, l, acc = carry[idx]
                s = lax.dot_general(qs[hh][qq], kh, (((1,), (1,)), ((), ())),
                                    preferred_element_type=F32)
                s = s + bias_ref[qq, j].astype(F32)
                m_new = jnp.maximum(m, jnp.max(s, axis=-1, keepdims=True))
                a = jnp.exp(m - m_new)
                p = jnp.exp(s - m_new)
                l = a * l + jnp.sum(p, axis=-1, keepdims=True)
                acc = a * acc + jnp.dot(p.astype(BF16), vh, preferred_element_type=F32)
                new.append((m_new, l, acc))
                idx += 1
        return tuple(new)

    init = tuple((jnp.full((IDX_TQ, 1), NEG_BIAS, F32), jnp.zeros((IDX_TQ, 1), F32),
                  jnp.zeros((IDX_TQ, DH_A), F32)) for _ in range(hp * nq))
    res = lax.fori_loop(0, n_t, body, init)
    idx = 0
    for hh in range(hp):
        for qq in range(nq):
            m, l, acc = res[idx]
            o_ref[qq * IDX_TQ:(qq + 1) * IDX_TQ, hh * DH_A:(hh + 1) * DH_A] = (acc / l).astype(BF16)
            idx += 1


def sparse_attention(proj3, k3, v3, bias5):
    B, S, _ = proj3.shape
    tq, hp = ATT_TQ, ATT_HP
    wb = hp * DH_A
    n_kt = S // IDX_TK
    return pl.pallas_call(
        _attention_kernel,
        out_shape=jax.ShapeDtypeStruct((B, S, W_A), BF16),
        grid=(B, H_A // hp, S // tq),
        in_specs=[pl.BlockSpec((None, tq, wb), lambda b, h, i: (b, i, OFF_QA // wb + h)),
                  pl.BlockSpec((None, S, wb), lambda b, h, i: (b, 0, h)),
                  pl.BlockSpec((None, S, wb), lambda b, h, i: (b, 0, h)),
                  pl.BlockSpec((None, tq // IDX_TQ, n_kt, IDX_TQ, IDX_TK),
                               lambda b, h, i: (b, i, 0, 0, 0))],
        out_specs=pl.BlockSpec((None, tq, wb), lambda b, h, i: (b, i, h)),
        compiler_params=_cparams(("parallel", "parallel", "parallel")),
        name="sparse_attention",
    )(proj3, k3, v3, bias5)


def _mem_attention_kernel(q_ref, km_ref, vm_ref, o_ref):
    scale = DH_M ** -0.5
    for h in range(H_M):
        sl = slice(h * DH_M, (h + 1) * DH_M)
        q = (q_ref[:, sl] * scale).astype(BF16)
        s = lax.dot_general(q, km_ref[:, sl], (((1,), (1,)), ((), ())), preferred_element_type=F32)
        m = jnp.max(s, axis=-1, keepdims=True)
        p = jnp.exp(s - m)
        p = p / jnp.sum(p, axis=-1, keepdims=True)
        o_ref[:, sl] = jnp.dot(p.astype(BF16), vm_ref[:, sl], preferred_element_type=F32).astype(BF16)


def memory_attention(proj3, km3, vm3, tm=512):
    B, S, _ = proj3.shape
    mspec = pl.BlockSpec((None, N_MEM, W_M), lambda b, i: (b, 0, 0))
    return pl.pallas_call(
        _mem_attention_kernel,
        out_shape=jax.ShapeDtypeStruct((B, S, W_M), BF16),
        grid=(B, S // tm),
        in_specs=[pl.BlockSpec((None, tm, W_M), lambda b, i: (b, i, OFF_QM // W_M)), mspec, mspec],
        out_specs=pl.BlockSpec((None, tm, W_M), lambda b, i: (b, i, 0)),
        compiler_params=_cparams(("parallel", "parallel")),
        name="memory_attention",
    )(proj3, km3, vm3)


def _group_sum(x, g_ref):
    hi, lo = _split2(x)
    return (jnp.dot(hi, g_ref[...], preferred_element_type=F32)
            + jnp.dot(lo, g_ref[...], preferred_element_type=F32))


def _rwkv_prep_kernel(*refs, has_vres):
    if has_vres:
        (r_ref, k_ref, v_ref, l_ref, vf_ref, mu_ref, mul_ref, par_ref, wl_ref, g_ref,
         ro_ref, lw_ref, ko_ref, vo_ref, kk_ref, a_ref, go_ref, prev_ref, prevl_ref) = refs
    else:
        (r_ref, k_ref, v_ref, l_ref, mu_ref, mul_ref, par_ref, wl_ref, g_ref,
         ro_ref, lw_ref, ko_ref, vo_ref, kk_ref, a_ref, go_ref, prev_ref, prevl_ref) = refs
        vf_ref = None
    s_idx = pl.program_id(1)

    @pl.when(s_idx == 0)
    def _():
        prev_ref[...] = jnp.zeros_like(prev_ref)
        prevl_ref[...] = jnp.zeros_like(prevl_ref)

    def shift(x, carry_row, mu):
        tm = x.shape[0]
        rolled = pltpu.roll(x, shift=1, axis=0)
        first = lax.broadcasted_iota(jnp.int32, x.shape, 0) == 0
        prev = jnp.where(first, jnp.broadcast_to(carry_row, x.shape), rolled)
        return x + (prev - x) * mu, x[tm - 1:tm, :]

    r, r_last = shift(r_ref[...], prev_ref[0:1, :], mu_ref[0:1, :])
    k, k_last = shift(k_ref[...], prev_ref[1:2, :], mu_ref[1:2, :])
    v, v_last = shift(v_ref[...], prev_ref[2:3, :], mu_ref[2:3, :])
    lo, l_last = shift(l_ref[...], prevl_ref[0:1, :], mul_ref[...])
    prev_ref[0:1, :] = r_last
    prev_ref[1:2, :] = k_last
    prev_ref[2:3, :] = v_last
    prevl_ref[0:1, :] = l_last

    lane = lax.broadcasted_iota(jnp.int32, lo.shape, 1)
    lt = jnp.where(lane < W_LORA, jnp.tanh(lo),
                   jnp.where((lane >= W_LORA + A_LORA) & (lane < W_LORA + A_LORA + G_LORA),
                             jax.nn.sigmoid(lo), lo))
    up = jnp.dot(lt.astype(BF16), wl_ref[...], preferred_element_type=F32)
    w0, a0, v0 = par_ref[0:1, :], par_ref[1:2, :], par_ref[2:3, :]
    k_k, k_a = par_ref[3:4, :], par_ref[4:5, :]

    y = -(w0 + up[:, 0:W_B])
    softplus = jnp.maximum(y, 0.0) + jnp.log(1.0 + jnp.exp(-jnp.abs(y)))
    w_log = -softplus - 0.5
    lw_ref[...] = -jnp.exp(w_log)
    a = jax.nn.sigmoid(a0 + up[:, W_B:2 * W_B])
    go_ref[...] = up[:, 2 * W_B:3 * W_B]
    if has_vres:
        v = v + (vf_ref[...] - v) * jax.nn.sigmoid(v0 + up[:, 3 * W_B:4 * W_B])
    kk = k * k_k
    norm = jnp.sqrt(_group_sum(kk * kk, g_ref))
    kk_ref[...] = kk / jnp.maximum(norm, 1e-12)
    ko_ref[...] = k * (1.0 + (a - 1.0) * k_a)
    ro_ref[...] = r
    vo_ref[...] = v
    a_ref[...] = a


def rwkv_prep(proj3, v_first, mu_rkv, mu_lora, params, w_lora, gmat, tm=256):
    B, S, _ = proj3.shape
    has_vres = v_first is not None
    tok = lambda c: pl.BlockSpec((None, tm, W_B), lambda b, s: (b, s, c))
    full = lambda shp: pl.BlockSpec(shp, lambda b, s: (0,) * len(shp))
    in_specs = [tok(OFF_RKV // W_B), tok(OFF_RKV // W_B + 1), tok(OFF_RKV // W_B + 2),
                pl.BlockSpec((None, tm, LORA_W), lambda b, s: (b, s, OFF_LORA // LORA_W))]
    args = [proj3, proj3, proj3, proj3]
    if has_vres:
        in_specs.append(tok(0))
        args.append(v_first)
    in_specs += [full((3, W_B)), full((1, LORA_W)), full((8, W_B)), full((LORA_W, 4 * W_B)),
                 full((W_B, W_B))]
    args += [mu_rkv, mu_lora, params, w_lora, gmat]
    out = jax.ShapeDtypeStruct((B, S, W_B), F32)
    return pl.pallas_call(
        functools.partial(_rwkv_prep_kernel, has_vres=has_vres),
        out_shape=(out,) * 7,
        grid=(B, S // tm),
        in_specs=in_specs,
        out_specs=(tok(0),) * 7,
        scratch_shapes=[pltpu.VMEM((8, W_B), F32), pltpu.VMEM((8, LORA_W), F32)],
        compiler_params=_cparams(("parallel", "arbitrary")),
        name="rwkv_prep",
    )(*args)


SCAN_HPG = 8


def _rwkv_scan_kernel(r_ref, lw_ref, k_ref, v_ref, kk_ref, a_ref, o_ref, state_ref):
    C, N = CHUNK, N_B

    @pl.when(pl.program_id(2) == 0)
    def _():
        state_ref[...] = jnp.zeros_like(state_ref)

    row = lax.broadcasted_iota(jnp.int32, (C, C), 0)
    col = lax.broadcasted_iota(jnp.int32, (C, C), 1)
    incl = (col <= row).astype(F32)
    strict = (col < row).astype(F32)
    eye = (col == row).astype(F32)
    tri_b = incl.astype(BF16)
    nt = (((1,), (1,)), ((), ()))

    for h in range(SCAN_HPG):
        sl = slice(h * N, (h + 1) * N)
        r, lw, k, v, kk, a = (ref[:, sl] for ref in (r_ref, lw_ref, k_ref, v_ref, kk_ref, a_ref))
        p0, p1, p2 = _split3(lw)
        cs = (jnp.dot(tri_b, p0, preferred_element_type=F32)
              + jnp.dot(tri_b, p1, preferred_element_type=F32)
              + jnp.dot(tri_b, p2, preferred_element_type=F32))
        e_neg = jnp.exp(-cs)
        at = (-kk) * jnp.exp(cs - lw)
        bt = kk * a * e_neg
        kt = k * e_neg
        rt = r * jnp.exp(cs)
        x = jnp.concatenate([at, rt], axis=0).astype(BF16)
        mb = lax.dot_general(x, bt.astype(BF16), nt, preferred_element_type=F32)
        mk = lax.dot_general(x, kt.astype(BF16), nt, preferred_element_type=F32)
        lab, mrb = mb[:C] * strict, mb[C:] * incl
        lak, mrk = mk[:C] * strict, mk[C:] * incl
        t = eye + lab
        p = lab
        for _ in range(5):
            pb = p.astype(BF16)
            p = jnp.dot(pb, pb, preferred_element_type=F32)
            t = t + jnp.dot(t.astype(BF16), p.astype(BF16), preferred_element_type=F32)
        state = state_ref[h]
        xh = lax.dot_general(x, state.astype(BF16), nt, preferred_element_type=F32)
        vb = v.astype(BF16)
        lv = jnp.dot(jnp.concatenate([lak, mrk], axis=0).astype(BF16), vb,
                     preferred_element_type=F32)
        u = jnp.dot(t.astype(BF16), (xh[:C] + lv[:C]).astype(BF16), preferred_element_type=F32)
        o_ref[:, sl] = xh[C:] + jnp.dot(mrb.astype(BF16), u.astype(BF16),
                                        preferred_element_type=F32) + lv[C:]
        g_last = jnp.exp(cs[C - 1:C, :])
        y2 = (jnp.concatenate([bt, kt], axis=0) * g_last).astype(BF16)
        uv = jnp.concatenate([u, v], axis=0)
        uv_t = jnp.concatenate([uv, jnp.zeros_like(uv)], axis=1).T[:N].astype(BF16)
        state_ref[h] = state * g_last + jnp.dot(uv_t, y2, preferred_element_type=F32)


def rwkv_scan(r, lw, k, v, kk, a):
    B, S, _ = r.shape
    wb = SCAN_HPG * N_B
    spec = pl.BlockSpec((None, CHUNK, wb), lambda b, g, c: (b, c, g))
    return pl.pallas_call(
        _rwkv_scan_kernel,
        out_shape=jax.ShapeDtypeStruct((B, S, W_B), F32),
        grid=(B, H_B // SCAN_HPG, S // CHUNK),
        in_specs=[spec] * 6,
        out_specs=spec,
        scratch_shapes=[pltpu.VMEM((SCAN_HPG, N_B, N_B), F32)],
        compiler_params=_cparams(("parallel", "parallel", "arbitrary")),
        name="rwkv_scan",
    )(r, lw, k, v, kk, a)


def _rwkv_post_kernel(o_ref, r_ref, k_ref, v_ref, g_ref, par_ref, gm_ref, out_ref):
    o = o_ref[...]
    mu = _group_sum(o, gm_ref) * (1.0 / N_B)
    oc = o - mu
    var = _group_sum(oc * oc, gm_ref) * (1.0 / N_B)
    on = oc * lax.rsqrt(var + LNX_EPS) * par_ref[0:1, :] + par_ref[1:2, :]
    bonus = _group_sum(r_ref[...] * k_ref[...] * par_ref[2:3, :], gm_ref) * v_ref[...]
    out_ref[...] = ((on + bonus) * g_ref[...]).astype(BF16)


def rwkv_post(o, r, k, v, g, params, gmat, tm=256):
    M = o.shape[0]
    row = pl.BlockSpec((tm, W_B), lambda i: (i, 0))
    return pl.pallas_call(
        _rwkv_post_kernel,
        out_shape=jax.ShapeDtypeStruct((M, W_B), BF16),
        grid=(M // tm,),
        in_specs=[row] * 5 + [pl.BlockSpec((8, W_B), lambda i: (0, 0)),
                              pl.BlockSpec((W_B, W_B), lambda i: (0, 0))],
        out_specs=row,
        compiler_params=_cparams(("parallel",)),
        name="rwkv_post",
    )(o, r, k, v, g, params, gmat)


def _merge_kernel(oa_ref, ob_ref, oc_ref, wa_ref, wb_ref, wc_ref, g0_ref, g1_ref, g2_ref, out_ref):
    ya = jnp.dot(oa_ref[...], wa_ref[...], preferred_element_type=F32)
    yb = jnp.dot(ob_ref[...], wb_ref[...], preferred_element_type=F32)
    yc = jnp.dot(oc_ref[...], wc_ref[...], preferred_element_type=F32)
    out = (jax.nn.sigmoid(g0_ref[...]) * ya + jax.nn.sigmoid(g1_ref[...]) * yb
           + jax.nn.sigmoid(g2_ref[...]) * yc)
    out_ref[...] = out.astype(BF16)


def gated_merge(o_a, o_b, o_c, w_pa, w_pb, w_pc, proj2, tm=512, tn=1024):
    M = o_a.shape[0]
    D = D_MODEL
    nj = D // tn
    act = lambda w: pl.BlockSpec((tm, w), lambda i, j: (i, 0))
    wgt = lambda w: pl.BlockSpec((w, tn), lambda i, j: (0, j))
    gate = lambda br: pl.BlockSpec((tm, tn), lambda i, j: (i, OFF_GATE // tn + br * nj + j))
    return pl.pallas_call(
        _merge_kernel,
        out_shape=jax.ShapeDtypeStruct((M, D), BF16),
        grid=(M // tm, nj),
        in_specs=[act(W_A), act(W_B), act(W_M), wgt(W_A), wgt(W_B), wgt(W_M),
                  gate(0), gate(1), gate(2)],
        out_specs=pl.BlockSpec((tm, tn), lambda i, j: (i, j)),
        compiler_params=_cparams(("parallel", "arbitrary")),
        name="gated_merge",
    )(o_a, o_b, o_c, w_pa, w_pb, w_pc, proj2, proj2, proj2)


def _first_index_of_max(x, idx, n):
    m = jnp.max(x, axis=0, keepdims=True)
    first = jnp.min(jnp.where(x == m, idx, n), axis=0, keepdims=True)
    return m, idx == first


def _router_kernel(h_ref, wr_ref, bias_ref, gate_ref):
    E, G = N_EXPERTS, N_GROUPS
    per = E // G
    logits = lax.dot_general(wr_ref[...], h_ref[...], (((1,), (1,)), ((), ())),
                             preferred_element_type=F32)
    tm = logits.shape[1]
    scores = jax.nn.sigmoid(logits)
    biased = scores + jnp.concatenate([bias_ref[...]] * (tm // 128), axis=1)
    idx8 = lax.broadcasted_iota(jnp.int32, (per, tm), 0)
    gscores = []
    for g in range(G):
        xg = biased[g * per:(g + 1) * per, :]
        m1, hit = _first_index_of_max(xg, idx8, per)
        m2 = jnp.max(jnp.where(hit, -jnp.inf, xg), axis=0, keepdims=True)
        gscores.append(m1 + m2)
    gs = jnp.concatenate(gscores, axis=0)
    gidx = lax.broadcasted_iota(jnp.int32, (G, tm), 0)
    gsel = jnp.zeros((G, tm), F32)
    for _ in range(TOPK_GROUPS):
        _, hit = _first_index_of_max(gs, gidx, G)
        gsel = jnp.where(hit, 1.0, gsel)
        gs = jnp.where(hit, -jnp.inf, gs)
    emask = jnp.concatenate(
        [jnp.broadcast_to(gsel[g:g + 1, :], (per, tm)) for g in range(G)], axis=0)
    cand = jnp.where(emask > 0.0, biased, -jnp.inf)
    eidx = lax.broadcasted_iota(jnp.int32, (E, tm), 0)
    picked = jnp.zeros((E, tm), F32)
    for _ in range(TOP_K):
        _, hit = _first_index_of_max(cand, eidx, E)
        picked = jnp.where(hit, scores, picked)
        cand = jnp.where(hit, -jnp.inf, cand)
    gate = picked / jnp.sum(picked, axis=0, keepdims=True) * ROUTED_SCALE
    gate_ref[...] = jnp.concatenate([gate, jnp.zeros_like(gate)], axis=0).T


def moe_router(hb, w_router_t, bias_b, tm=256):
    M, D = hb.shape
    return pl.pallas_call(
        _router_kernel,
        out_shape=jax.ShapeDtypeStruct((M, 2 * N_EXPERTS), F32),
        grid=(M // tm,),
        in_specs=[pl.BlockSpec((tm, D), lambda i: (i, 0)),
                  pl.BlockSpec((N_EXPERTS, D), lambda i: (0, 0)),
                  pl.BlockSpec((N_EXPERTS, 128), lambda i: (0, 0))],
        out_specs=pl.BlockSpec((tm, 2 * N_EXPERTS), lambda i: (i, 0)),
        compiler_params=_cparams(("parallel",)),
        name="moe_router",
    )(hb, w_router_t, bias_b)


MOE_TF = 384
MOE_NC = (N_EXPERTS * F_EXPERT) // MOE_TF + 1
MOE_F = MOE_NC * MOE_TF


def _moe_kernel(x_ref, gate_ref, e_ref, w1_ref, w3_ref, w2_ref, o_ref, acc_ref):
    c = pl.program_id(1)

    @pl.when(c == 0)
    def _():
        acc_ref[...] = jnp.zeros_like(acc_ref)

    x = x_ref[...]
    h1 = jnp.dot(x, w1_ref[...], preferred_element_type=F32)
    h3 = jnp.dot(x, w3_ref[...], preferred_element_type=F32)
    g0, g1, g2 = _split3(gate_ref[...])
    e = e_ref[...]
    gexp = (jnp.dot(g0, e, preferred_element_type=F32) + jnp.dot(g1, e, preferred_element_type=F32)
            + jnp.dot(g2, e, preferred_element_type=F32))
    gexp = jnp.where(c == MOE_NC - 1, 1.0, gexp)
    act = h1 * jax.nn.sigmoid(h1) * h3 * gexp
    acc_ref[...] += jnp.dot(act.astype(BF16), w2_ref[...], preferred_element_type=F32)

    @pl.when(c == MOE_NC - 1)
    def _():
        o_ref[...] = acc_ref[...]


def moe_ffn(hb, gate, expand, w1c, w3c, w2c, tm=512):
    M, D = hb.shape
    return pl.pallas_call(
        _moe_kernel,
        out_shape=jax.ShapeDtypeStruct((M, D), F32),
        grid=(M // tm, MOE_NC),
        in_specs=[pl.BlockSpec((tm, D), lambda i, c: (i, 0)),
                  pl.BlockSpec((tm, 2 * N_EXPERTS), lambda i, c: (i, 0)),
                  pl.BlockSpec((2 * N_EXPERTS, MOE_TF), lambda i, c: (0, c)),
                  pl.BlockSpec((D, MOE_TF), lambda i, c: (0, c)),
                  pl.BlockSpec((D, MOE_TF), lambda i, c: (0, c)),
                  pl.BlockSpec((MOE_TF, D), lambda i, c: (c, 0))],
        out_specs=pl.BlockSpec((tm, D), lambda i, c: (i, 0)),
        scratch_shapes=[pltpu.VMEM((tm, D), F32)],
        compiler_params=_cparams(("parallel", "arbitrary")),
        name="moe_ffn",
    )(hb, gate, expand, w1c, w3c, w2c)


def _regroup_w_in(w, has_vres):
    D = w.shape[0]
    o = np.cumsum((0, W_A, R_KV, H_IDX * D_IDX, D_IDX, H_IDX, 3 * W_B, W_LORA, A_LORA, G_LORA, W_M,
                   3 * D_MODEL)).tolist()
    q_a, c_kv, q_idx = w[:, o[0]:o[1]], w[:, o[1]:o[2]], w[:, o[2]:o[3]]
    idx = w[:, o[3]:o[5]]
    rkv, lora = w[:, o[5]:o[6]], w[:, o[6]:o[9]]
    q_m, gates = w[:, o[9]:o[10]], w[:, o[10]:o[11]]
    zeros = lambda n: jnp.zeros((D, n), w.dtype)
    vres = w[:, o[11]:o[11] + V_LORA] if has_vres else zeros(V_LORA)
    n_lora = W_LORA + A_LORA + G_LORA + V_LORA
    parts = [gates, q_a, q_idx, rkv, q_m, c_kv, lora, vres, zeros(LORA_W - n_lora),
             idx, zeros(IDX_W - D_IDX - H_IDX)]
    used = sum(p.shape[1] for p in parts)
    parts.append(zeros(N_PROJ - used))
    return jnp.concatenate(parts, axis=1).astype(BF16)


def _lora_up_matrix(w_up, a_up, g_up, v_up):
    m = jnp.zeros((LORA_W, 4 * W_B), F32)
    m = m.at[0:W_LORA, 0:W_B].set(w_up)
    m = m.at[W_LORA:W_LORA + A_LORA, W_B:2 * W_B].set(a_up)
    m = m.at[W_LORA + A_LORA:W_LORA + A_LORA + G_LORA, 2 * W_B:3 * W_B].set(g_up)
    if v_up is not None:
        r0 = W_LORA + A_LORA + G_LORA
        m = m.at[r0:r0 + V_LORA, 3 * W_B:4 * W_B].set(v_up)
    return m.astype(BF16)


def _rows(vectors, n_rows, width):
    m = jnp.zeros((n_rows, width), F32)
    for i, vec in enumerate(vectors):
        m = m.at[i, :vec.shape[0]].set(vec)
    return m


def _expert_in_weights(w_e, w_s):
    D = w_e.shape[1]
    routed = jnp.transpose(w_e, (1, 0, 2)).reshape(D, N_EXPERTS * F_EXPERT)
    pad = jnp.zeros((D, MOE_F - N_EXPERTS * F_EXPERT - w_s.shape[1]), w_e.dtype)
    return jnp.concatenate([routed, w_s, pad], axis=1).astype(BF16)


def _expert_out_weights(w_e2, w_s2):
    D = w_e2.shape[2]
    routed = w_e2.reshape(N_EXPERTS * F_EXPERT, D)
    pad = jnp.zeros((MOE_F - N_EXPERTS * F_EXPERT - w_s2.shape[0], D), w_e2.dtype)
    return jnp.concatenate([routed, w_s2, pad], axis=0).astype(BF16)


def kernel(x, mem, ln0_g, ln0_b, w_in_first, w_in_rest, ckv_g, w_uk, w_uv, mu_rwkv, mu_vres, rw_w0, rw_w_up, rw_a0, rw_a_up, rw_g_up, rw_v0, rw_v_up, rw_k_k, rw_k_a, rw_r_k, rw_lnx_g, rw_lnx_b, w_mk, w_mv, w_pa, w_pb, w_pc, w_o, ln1_g, ln1_b, w_router, router_bias, w_e1, w_e3, w_e2, w_s1, w_s3, w_s2, ln2_g, ln2_b):
    B, S, D = x.shape
    T = B * S
    gmat = jnp.kron(jnp.eye(H_B, dtype=F32), jnp.ones((N_B, N_B), F32)).astype(BF16)
    expand = jnp.kron(jnp.eye(N_EXPERTS, dtype=F32), jnp.ones((1, F_EXPERT), F32))
    expand = jnp.pad(expand, ((0, N_EXPERTS), (0, MOE_F - N_EXPERTS * F_EXPERT))).astype(BF16)
    mem_b = mem.reshape(B * N_MEM, D).astype(BF16)

    h, hb = layer_norm(x.reshape(T, D), ln0_g, ln0_b)
    v_first = None
    for i in range(DEPTH):
        has_vres = i > 0
        w_in = _regroup_w_in(w_in_first if i == 0 else w_in_rest[i - 1], has_vres)
        proj2 = matmul(hb, w_in, F32, 1024, 1024, 512)
        proj3 = proj2.reshape(B, S, N_PROJ)

        k2, v2 = kv_project(proj2, ckv_g[i], w_uk[i].astype(BF16), w_uv[i].astype(BF16))
        bias5 = indexer_bias(proj3)
        o_a = sparse_attention(proj3, k2.reshape(B, S, W_A), v2.reshape(B, S, W_A), bias5)

        mu = mu_rwkv[i]
        mu_rkv = mu[:3 * W_B].reshape(3, W_B)
        mu_l = [mu[3 * W_B:]] + ([mu_vres[i - 1]] if has_vres else [])
        mu_lora = _rows([jnp.concatenate(mu_l)], 1, LORA_W)
        prep_par = _rows([rw_w0[i], rw_a0[i], rw_v0[i - 1] if has_vres else jnp.zeros((W_B,), F32),
                          rw_k_k[i], rw_k_a[i]], 8, W_B)
        w_lora = _lora_up_matrix(rw_w_up[i], rw_a_up[i], rw_g_up[i],
                                 rw_v_up[i - 1] if has_vres else None)
        r_b, lw_b, k_b, v_b, kk_b, a_b, g_b = rwkv_prep(proj3, v_first, mu_rkv, mu_lora, prep_par,
                                                        w_lora, gmat)
        if i == 0:
            v_first = v_b
        o_scan = rwkv_scan(r_b, lw_b, k_b, v_b, kk_b, a_b)
        post_par = _rows([rw_lnx_g[i], rw_lnx_b[i], rw_r_k[i].reshape(W_B)], 8, W_B)
        flat = lambda z: z.reshape(T, W_B)
        o_b = rwkv_post(flat(o_scan), flat(r_b), flat(k_b), flat(v_b), flat(g_b), post_par, gmat)

        km = matmul(mem_b, w_mk[i].astype(BF16), BF16, 512, 1024, 512).reshape(B, N_MEM, W_M)
        vm = matmul(mem_b, w_mv[i].astype(BF16), BF16, 512, 1024, 512).reshape(B, N_MEM, W_M)
        o_c = memory_attention(proj3, km, vm)

        merged = gated_merge(o_a.reshape(T, W_A), o_b, o_c.reshape(T, W_M), w_pa[i].astype(BF16),
                             w_pb[i].astype(BF16), w_pc[i].astype(BF16), proj2)
        h, hb = matmul_residual_ln(merged, w_o[i].astype(BF16), h, ln1_g[i], ln1_b[i])

        bias_b = jnp.broadcast_to(router_bias[i].reshape(N_EXPERTS, 1), (N_EXPERTS, 128))
        gate = moe_router(hb, w_router[i].T.astype(BF16), bias_b)
        ffn = moe_ffn(hb, gate, expand, _expert_in_weights(w_e1[i], w_s1[i]),
                      _expert_in_weights(w_e3[i], w_s3[i]), _expert_out_weights(w_e2[i], w_s2[i]))
        h, hb = residual_layer_norm(h, ffn, ln2_g[i], ln2_b[i])
    return h.reshape(B, S, D)
```

```python
import functools

import jax
import jax.numpy as jnp
import numpy as np
from jax import lax
from jax.experimental import pallas as pl
from jax.experimental.pallas import tpu as pltpu

F32 = jnp.float32
BF16 = jnp.bfloat16

D_MODEL = 4096
DEPTH = 4
CHUNK = 64
H_A, DH_A = 16, 128
W_A = H_A * DH_A
R_KV = 256
H_IDX, D_IDX = 32, 64
TOPK_MAX = 256
H_B, N_B = 16, 64
W_B = H_B * N_B
W_LORA, A_LORA, G_LORA, V_LORA = 64, 64, 160, 32
LNX_EPS = 64e-5
N_MEM = 256
H_M, DH_M = 4, 256
W_M = H_M * DH_M
N_EXPERTS, N_GROUPS, TOPK_GROUPS, TOP_K = 64, 8, 4, 8
F_EXPERT = 96
ROUTED_SCALE = 2.5
ALPHA = (2 * DEPTH) ** 0.25
LN_EPS = 1e-5

OFF_GATE = 0
OFF_QA = 3 * D_MODEL
OFF_QIDX = OFF_QA + W_A
OFF_RKV = OFF_QIDX + H_IDX * D_IDX
OFF_QM = OFF_RKV + 3 * W_B
OFF_CKV = OFF_QM + W_M
OFF_LORA = OFF_CKV + R_KV
LORA_W = 384
OFF_IDX = OFF_LORA + LORA_W
IDX_W = 128
N_PROJ = 21504

NEG_BIAS = -1e30
VMEM_LIMIT = 56 * 1024 * 1024


def _cparams(sem):
    return pltpu.CompilerParams(dimension_semantics=sem, vmem_limit_bytes=VMEM_LIMIT)


def _split2(x):
    hi = x.astype(BF16)
    lo = (x - hi.astype(F32)).astype(BF16)
    return hi, lo


def _split3(x):
    hi = x.astype(BF16)
    r1 = x - hi.astype(F32)
    mid = r1.astype(BF16)
    lo = (r1 - mid.astype(F32)).astype(BF16)
    return hi, mid, lo


def _mm_kernel(x_ref, w_ref, o_ref):
    o_ref[...] = jnp.dot(x_ref[...], w_ref[...], preferred_element_type=F32).astype(o_ref.dtype)


def matmul(x, w, out_dtype, tm, tn):
    M, K = x.shape
    N = w.shape[1]
    tm, tn = min(tm, M), min(tn, N)
    assert M % tm == 0 and N % tn == 0
    return pl.pallas_call(
        _mm_kernel,
        out_shape=jax.ShapeDtypeStruct((M, N), out_dtype),
        grid=(M // tm, N // tn),
        in_specs=[pl.BlockSpec((tm, K), lambda i, j: (i, 0)),
                  pl.BlockSpec((K, tn), lambda i, j: (0, j))],
        out_specs=pl.BlockSpec((tm, tn), lambda i, j: (i, j)),
        compiler_params=_cparams(("parallel", "arbitrary")),
        name="matmul",
    )(x, w)


def _ln_rows(y, g, b):
    mu = jnp.mean(y, axis=-1, keepdims=True)
    yc = y - mu
    var = jnp.mean(yc * yc, axis=-1, keepdims=True)
    return yc * lax.rsqrt(var + LN_EPS) * g + b


def _ln_kernel(x_ref, g_ref, b_ref, o_ref, ob_ref):
    y = _ln_rows(x_ref[...], g_ref[...], b_ref[...])
    o_ref[...] = y
    ob_ref[...] = y.astype(BF16)


def _res_ln_kernel(h_ref, f_ref, g_ref, b_ref, o_ref, ob_ref):
    y = _ln_rows(ALPHA * h_ref[...] + f_ref[...], g_ref[...], b_ref[...])
    o_ref[...] = y
    ob_ref[...] = y.astype(BF16)


def layer_norm(x, g, b, tm=256):
    M, D = x.shape
    row = pl.BlockSpec((tm, D), lambda i: (i, 0))
    vec = pl.BlockSpec((1, D), lambda i: (0, 0))
    return pl.pallas_call(
        _ln_kernel,
        out_shape=(jax.ShapeDtypeStruct((M, D), F32), jax.ShapeDtypeStruct((M, D), BF16)),
        grid=(M // tm,),
        in_specs=[row, vec, vec],
        out_specs=(row, row),
        compiler_params=_cparams(("parallel",)),
        name="layer_norm",
    )(x, g.reshape(1, D), b.reshape(1, D))


def residual_layer_norm(h, f, g, b, tm=256):
    M, D = h.shape
    row = pl.BlockSpec((tm, D), lambda i: (i, 0))
    vec = pl.BlockSpec((1, D), lambda i: (0, 0))
    return pl.pallas_call(
        _res_ln_kernel,
        out_shape=(jax.ShapeDtypeStruct((M, D), F32), jax.ShapeDtypeStruct((M, D), BF16)),
        grid=(M // tm,),
        in_specs=[row, row, vec, vec],
        out_specs=(row, row),
        compiler_params=_cparams(("parallel",)),
        name="residual_layer_norm",
    )(h, f, g.reshape(1, D), b.reshape(1, D))


def _mm_res_ln_kernel(x_ref, w_ref, h_ref, g_ref, b_ref, o_ref, ob_ref, acc_ref, *, nj, tn):
    j = pl.program_id(1)
    acc_ref[j] = jnp.dot(x_ref[...], w_ref[...], preferred_element_type=F32)

    @pl.when(j == nj - 1)
    def _():
        D = nj * tn
        s1 = None
        for jj in range(nj):
            y = ALPHA * h_ref[:, jj * tn:(jj + 1) * tn] + acc_ref[jj]
            acc_ref[jj] = y
            p = jnp.sum(y, axis=-1, keepdims=True)
            s1 = p if s1 is None else s1 + p
        mu = s1 * (1.0 / D)
        s2 = None
        for jj in range(nj):
            yc = acc_ref[jj] - mu
            p = jnp.sum(yc * yc, axis=-1, keepdims=True)
            s2 = p if s2 is None else s2 + p
        rstd = lax.rsqrt(s2 * (1.0 / D) + LN_EPS)
        for jj in range(nj):
            sl = slice(jj * tn, (jj + 1) * tn)
            y = (acc_ref[jj] - mu) * rstd * g_ref[:, sl] + b_ref[:, sl]
            o_ref[:, sl] = y
            ob_ref[:, sl] = y.astype(BF16)


def matmul_residual_ln(x, w, h, g, b, tm=256, tn=512):
    M, K = x.shape
    D = w.shape[1]
    nj = D // tn
    row = pl.BlockSpec((tm, D), lambda i, j: (i, 0))
    vec = pl.BlockSpec((1, D), lambda i, j: (0, 0))
    return pl.pallas_call(
        functools.partial(_mm_res_ln_kernel, nj=nj, tn=tn),
        out_shape=(jax.ShapeDtypeStruct((M, D), F32), jax.ShapeDtypeStruct((M, D), BF16)),
        grid=(M // tm, nj),
        in_specs=[pl.BlockSpec((tm, K), lambda i, j: (i, 0)),
                  pl.BlockSpec((K, tn), lambda i, j: (0, j)),
                  row, vec, vec],
        out_specs=(row, row),
        scratch_shapes=[pltpu.VMEM((nj, tm, tn), F32)],
        compiler_params=_cparams(("parallel", "arbitrary")),
        name="matmul_residual_ln",
    )(x, w, h, g.reshape(1, D), b.reshape(1, D))


def _kv_kernel(c_ref, g_ref, wk_ref, wv_ref, k_ref, v_ref):
    c = c_ref[...]
    ms = jnp.mean(c * c, axis=-1, keepdims=True)
    cn = (c * lax.rsqrt(ms + LN_EPS) * g_ref[...]).astype(BF16)
    k_ref[...] = jnp.dot(cn, wk_ref[...], preferred_element_type=F32).astype(BF16)
    v_ref[...] = jnp.dot(cn, wv_ref[...], preferred_element_type=F32).astype(BF16)


def kv_project(proj2, ckv_g, w_uk, w_uv, tm=512):
    M = proj2.shape[0]
    out = jax.ShapeDtypeStruct((M, W_A), BF16)
    wspec = pl.BlockSpec((R_KV, W_A), lambda i: (0, 0))
    ospec = pl.BlockSpec((tm, W_A), lambda i: (i, 0))
    return pl.pallas_call(
        _kv_kernel,
        out_shape=(out, out),
        grid=(M // tm,),
        in_specs=[pl.BlockSpec((tm, R_KV), lambda i: (i, OFF_CKV // R_KV)),
                  pl.BlockSpec((1, R_KV), lambda i: (0, 0)), wspec, wspec],
        out_specs=(ospec, ospec),
        compiler_params=_cparams(("parallel",)),
        name="kv_project",
    )(proj2, ckv_g.reshape(1, R_KV), w_uk, w_uv)


IDX_TQ = 128
IDX_TK = 256
KEY_NEG_INF = -2139095041
INT_MIN = -2147483648


def _indexer_kernel(q_ref, iq_ref, ik_ref, bias_ref, q2_ref, wib_ref, key_ref, *, n_kt, ksel):
    tq, tk = IDX_TQ, IDX_TK
    i = pl.program_id(1)
    q = q_ref[...]
    wi = iq_ref[...][:, D_IDX:D_IDX + H_IDX] * (H_IDX ** -0.5)
    for h in range(H_IDX):
        q2_ref[h] = (q[:, h * D_IDX:(h + 1) * D_IDX] * (D_IDX ** -0.5)).astype(BF16)
        wib_ref[h] = jnp.broadcast_to(wi[:, h:h + 1], (tq, 128))

    n_t = ((i + 1) * tq + tk - 1) // tk
    q_chunk = (i * tq + lax.broadcasted_iota(jnp.int32, (tq, tk), 0)) // CHUNK
    col_iota = lax.broadcasted_iota(jnp.int32, (tq, tk), 1)

    def score_tile(j, carry):
        kt = ik_ref[pl.ds(pl.multiple_of(j * tk, tk), tk), :][:, :D_IDX].astype(BF16)
        acc = jnp.zeros((tq, tk), F32)
        for h in range(H_IDX):
            z = lax.dot_general(q2_ref[h], kt, (((1,), (1,)), ((), ())),
                                preferred_element_type=F32)
            w = wib_ref[h]
            acc = acc + jnp.concatenate([w] * (tk // 128), axis=1) * jnp.maximum(z, 0.0)
        admissible = ((j * tk + col_iota) // CHUNK) <= q_chunk
        score = jnp.where(admissible, acc, -jnp.inf)
        bits = lax.bitcast_convert_type(score, jnp.int32)
        key_ref[j] = bits ^ ((bits >> 31) & 0x7FFFFFFF)
        return carry

    lax.fori_loop(0, n_t, score_tile, 0)

    def bit_step(it, thr):
        cand = thr + lax.shift_left(jnp.int32(1), 31 - it)
        cand_b = jnp.concatenate([cand] * (tk // 128), axis=1)

        def count_tile(j, cnt):
            c = (key_ref[j] >= cand_b).astype(jnp.int32)
            for s in range(tk // 128):
                cnt = cnt + c[:, s * 128:(s + 1) * 128]
            return cnt

        cnt = lax.fori_loop(0, n_t, count_tile, jnp.zeros((tq, 128), jnp.int32))
        total = jnp.sum(cnt.astype(F32), axis=1, keepdims=True)
        return jnp.where(total >= float(ksel), cand, thr)

    thr = lax.fori_loop(0, 32, bit_step, jnp.full((tq, 128), INT_MIN, jnp.int32))
    thr_b = jnp.concatenate([thr] * (tk // 128), axis=1)

    def bias_tile(j, carry):
        key = key_ref[j]
        sel = (key >= thr_b) & (key > KEY_NEG_INF)
        bias_ref[j] = jnp.where(sel, 0.0, NEG_BIAS).astype(BF16)
        return carry

    lax.fori_loop(0, n_t, bias_tile, 0)

    def fill_tile(j, carry):
        bias_ref[j] = jnp.full((tq, tk), NEG_BIAS, BF16)
        return carry

    lax.fori_loop(n_t, n_kt, fill_tile, 0)


def indexer_bias(proj3):
    B, S, _ = proj3.shape
    tq, tk = IDX_TQ, IDX_TK
    n_qt, n_kt = S // tq, S // tk
    ksel = min(TOPK_MAX, S // 4)
    return pl.pallas_call(
        functools.partial(_indexer_kernel, n_kt=n_kt, ksel=ksel),
        out_shape=jax.ShapeDtypeStruct((B, n_qt, n_kt, tq, tk), BF16),
        grid=(B, n_qt),
        in_specs=[pl.BlockSpec((None, tq, H_IDX * D_IDX), lambda b, i: (b, i, OFF_QIDX // (H_IDX * D_IDX))),
                  pl.BlockSpec((None, tq, IDX_W), lambda b, i: (b, i, OFF_IDX // IDX_W)),
                  pl.BlockSpec((None, S, IDX_W), lambda b, i: (b, 0, OFF_IDX // IDX_W))],
        out_specs=pl.BlockSpec((None, None, n_kt, tq, tk), lambda b, i: (b, i, 0, 0, 0)),
        scratch_shapes=[pltpu.VMEM((H_IDX, tq, D_IDX), BF16),
                        pltpu.VMEM((H_IDX, tq, 128), F32),
                        pltpu.VMEM((n_kt, tq, tk), jnp.int32)],
        compiler_params=_cparams(("parallel", "parallel")),
        name="indexer_bias",
    )(proj3, proj3, proj3)


ATT_TQ = 256
ATT_HP = 2
ATT_TK = 512


ATT_RG = 16


def _attention_kernel(q_ref, k_ref, v_ref, bias_ref, o_ref, qs_ref, s_ref, p_ref, m_ref, a_ref,
                      acc_ref):
    tq, tk, hp = ATT_TQ, ATT_TK, ATT_HP
    nq = tq // IDX_TQ
    nb = tk // IDX_TK
    i = pl.program_id(2)
    scale = DH_A ** -0.5
    nt = (((1,), (1,)), ((), ()))
    streams = [(hh, qq) for hh in range(hp) for qq in range(nq)]
    cols = [slice(hh * DH_A, (hh + 1) * DH_A) for hh in range(hp)]
    rows = [slice(qq * IDX_TQ, (qq + 1) * IDX_TQ) for qq in range(nq)]
    for c, (hh, qq) in enumerate(streams):
        qs_ref[c] = (q_ref[rows[qq], cols[hh]] * scale).astype(BF16)
    m_ref[...] = jnp.full(m_ref.shape, NEG_BIAS, F32)
    acc_ref[...] = jnp.zeros_like(acc_ref)
    n_t = ((i + 1) * tq + tk - 1) // tk
    ones = jnp.ones((tk, DH_A), BF16)

    def body(j, carry):
        row0 = pl.multiple_of(j * tk, tk)
        for c, (hh, qq) in enumerate(streams):
            s_ref[c] = lax.dot_general(qs_ref[c], k_ref[pl.ds(row0, tk), cols[hh]], nt,
                                       preferred_element_type=F32)
        for c, (hh, qq) in enumerate(streams):
            for r in range(IDX_TQ // ATT_RG):
                rs = slice(r * ATT_RG, (r + 1) * ATT_RG)
                bias = jnp.concatenate([bias_ref[qq, nb * j + t, rs, :] for t in range(nb)], axis=1)
                sc = s_ref[c, rs, :] + bias.astype(F32)
                m_old = m_ref[c, rs, :]
                m_new = jnp.maximum(m_old, jnp.max(sc, axis=-1, keepdims=True))
                a_ref[c, rs, :] = jnp.exp(m_old - m_new)
                m_ref[c, rs, :] = m_new
                p_ref[c, rs, :] = jnp.exp(sc - jnp.concatenate([m_new] * (tk // 128), axis=1)
                                          ).astype(BF16)
        for c, (hh, qq) in enumerate(streams):
            v1 = jnp.concatenate([v_ref[pl.ds(row0, tk), cols[hh]], ones], axis=1)
            pv = jnp.dot(p_ref[c], v1, preferred_element_type=F32)
            acc_ref[c] = jnp.concatenate([a_ref[c]] * 2, axis=1) * acc_ref[c] + pv
        return carry

    lax.fori_loop(0, n_t, body, 0)
    for c, (hh, qq) in enumerate(streams):
        o_ref[rows[qq], cols[hh]] = (acc_ref[c, :, :DH_A] / acc_ref[c, :, DH_A:]).astype(BF16)


def sparse_attention(proj3, k3, v3, bias5):
    B, S, _ = proj3.shape
    tq, hp = ATT_TQ, ATT_HP
    wb = hp * DH_A
    n_kt = S // IDX_TK
    n_streams = hp * (tq // IDX_TQ)
    return pl.pallas_call(
        _attention_kernel,
        out_shape=jax.ShapeDtypeStruct((B, S, W_A), BF16),
        grid=(B, H_A // hp, S // tq),
        in_specs=[pl.BlockSpec((None, tq, wb), lambda b, h, i: (b, i, OFF_QA // wb + h)),
                  pl.BlockSpec((None, S, wb), lambda b, h, i: (b, 0, h)),
                  pl.BlockSpec((None, S, wb), lambda b, h, i: (b, 0, h)),
                  pl.BlockSpec((None, tq // IDX_TQ, n_kt, IDX_TQ, IDX_TK),
                               lambda b, h, i: (b, i, 0, 0, 0))],
        out_specs=pl.BlockSpec((None, tq, wb), lambda b, h, i: (b, i, h)),
        scratch_shapes=[pltpu.VMEM((n_streams, IDX_TQ, DH_A), BF16),
                        pltpu.VMEM((n_streams, IDX_TQ, ATT_TK), F32),
                        pltpu.VMEM((n_streams, IDX_TQ, ATT_TK), BF16),
                        pltpu.VMEM((n_streams, IDX_TQ, 128), F32),
                        pltpu.VMEM((n_streams, IDX_TQ, 128), F32),
                        pltpu.VMEM((n_streams, IDX_TQ, 2 * DH_A), F32)],
        compiler_params=_cparams(("parallel", "parallel", "parallel")),
        name="sparse_attention",
    )(proj3, k3, v3, bias5)


def _mem_attention_kernel(q_ref, km_ref, vm_ref, o_ref):
    scale = DH_M ** -0.5
    for h in range(H_M):
        sl = slice(h * DH_M, (h + 1) * DH_M)
        q = (q_ref[:, sl] * scale).astype(BF16)
        s = lax.dot_general(q, km_ref[:, sl], (((1,), (1,)), ((), ())), preferred_element_type=F32)
        m = jnp.max(s, axis=-1, keepdims=True)
        p = jnp.exp(s - m)
        p = p / jnp.sum(p, axis=-1, keepdims=True)
        o_ref[:, sl] = jnp.dot(p.astype(BF16), vm_ref[:, sl], preferred_element_type=F32).astype(BF16)


def memory_attention(proj3, km3, vm3, tm=512):
    B, S, _ = proj3.shape
    mspec = pl.BlockSpec((None, N_MEM, W_M), lambda b, i: (b, 0, 0))
    return pl.pallas_call(
        _mem_attention_kernel,
        out_shape=jax.ShapeDtypeStruct((B, S, W_M), BF16),
        grid=(B, S // tm),
        in_specs=[pl.BlockSpec((None, tm, W_M), lambda b, i: (b, i, OFF_QM // W_M)), mspec, mspec],
        out_specs=pl.BlockSpec((None, tm, W_M), lambda b, i: (b, i, 0)),
        compiler_params=_cparams(("parallel", "parallel")),
        name="memory_attention",
    )(proj3, km3, vm3)


def _group_sum(x, g_ref):
    hi, lo = _split2(x)
    return (jnp.dot(hi, g_ref[...], preferred_element_type=F32)
            + jnp.dot(lo, g_ref[...], preferred_element_type=F32))


def _rwkv_prep_kernel(*refs, has_vres):
    refs = list(refs)
    r_ref, k_ref, v_ref, l_ref = refs[:4]
    vf_ref = refs[4] if has_vres else None
    (mu_ref, mul_ref, par_ref, wl_ref, g_ref, tri_ref, same_ref,
     ro_ref, ko_ref, vo_ref, go_ref, gl_ref, at_ref, rt_ref, bt_ref, kt_ref, bt2_ref, kt2_ref,
     prev_ref, prevl_ref) = refs[4 + int(has_vres):]
    s_idx = pl.program_id(1)

    @pl.when(s_idx == 0)
    def _():
        prev_ref[...] = jnp.zeros_like(prev_ref)
        prevl_ref[...] = jnp.zeros_like(prevl_ref)

    def shift(x, carry_row, mu):
        tm = x.shape[0]
        rolled = pltpu.roll(x, shift=1, axis=0)
        first = lax.broadcasted_iota(jnp.int32, x.shape, 0) == 0
        prev = jnp.where(first, jnp.broadcast_to(carry_row, x.shape), rolled)
        return x + (prev - x) * mu, x[tm - 1:tm, :]

    r, r_last = shift(r_ref[...], prev_ref[0:1, :], mu_ref[0:1, :])
    k, k_last = shift(k_ref[...], prev_ref[1:2, :], mu_ref[1:2, :])
    v, v_last = shift(v_ref[...], prev_ref[2:3, :], mu_ref[2:3, :])
    lo, l_last = shift(l_ref[...], prevl_ref[0:1, :], mul_ref[...])
    prev_ref[0:1, :] = r_last
    prev_ref[1:2, :] = k_last
    prev_ref[2:3, :] = v_last
    prevl_ref[0:1, :] = l_last

    lane = lax.broadcasted_iota(jnp.int32, lo.shape, 1)
    lt = jnp.where(lane < W_LORA, jnp.tanh(lo),
                   jnp.where((lane >= W_LORA + A_LORA) & (lane < W_LORA + A_LORA + G_LORA),
                             jax.nn.sigmoid(lo), lo))
    up = jnp.dot(lt.astype(BF16), wl_ref[...], preferred_element_type=F32)
    w0, a0, v0 = par_ref[0:1, :], par_ref[1:2, :], par_ref[2:3, :]
    k_k, k_a = par_ref[3:4, :], par_ref[4:5, :]

    y = -(w0 + up[:, 0:W_B])
    softplus = jnp.maximum(y, 0.0) + jnp.log(1.0 + jnp.exp(-jnp.abs(y)))
    w_log = -softplus - 0.5
    lw = -jnp.exp(w_log)
    a = jax.nn.sigmoid(a0 + up[:, W_B:2 * W_B])
    go_ref[...] = up[:, 2 * W_B:3 * W_B]
    if has_vres:
        v = v + (vf_ref[...] - v) * jax.nn.sigmoid(v0 + up[:, 3 * W_B:4 * W_B])
    kk = k * k_k
    norm = jnp.sqrt(_group_sum(kk * kk, g_ref))
    kk = kk / jnp.maximum(norm, 1e-12)
    k = k * (1.0 + (a - 1.0) * k_a)
    ro_ref[...] = r
    ko_ref[...] = k
    vo_ref[...] = v

    p0, p1, p2 = _split3(lw)
    tri, same = tri_ref[...], same_ref[...]
    cs = (jnp.dot(tri, p0, preferred_element_type=F32) + jnp.dot(tri, p1, preferred_element_type=F32)
          + jnp.dot(tri, p2, preferred_element_type=F32))
    cl = (jnp.dot(same, p0, preferred_element_type=F32) + jnp.dot(same, p1, preferred_element_type=F32)
          + jnp.dot(same, p2, preferred_element_type=F32))
    e_neg = jnp.exp(-cs)
    to_end = jnp.exp(cl - cs)
    kka = kk * a
    at_ref[...] = (-kk * jnp.exp(cs - lw)).astype(BF16)
    rt_ref[...] = (r * jnp.exp(cs)).astype(BF16)
    bt_ref[...] = (kka * e_neg).astype(BF16)
    kt_ref[...] = (k * e_neg).astype(BF16)
    bt2_ref[...] = (kka * to_end).astype(BF16)
    kt2_ref[...] = (k * to_end).astype(BF16)
    gl_ref[...] = jnp.exp(cl)


def _chunk_matrices(tm):
    t = np.arange(tm)
    same = (t[:, None] // CHUNK) == (t[None, :] // CHUNK)
    tri = same & (t[None, :] <= t[:, None])
    return jnp.asarray(tri, BF16), jnp.asarray(same, BF16)


def rwkv_prep(proj3, v_first, mu_rkv, mu_lora, params, w_lora, gmat, tm=256):
    B, S, _ = proj3.shape
    has_vres = v_first is not None
    tok = lambda c: pl.BlockSpec((None, tm, W_B), lambda b, s: (b, s, c))
    full = lambda shp: pl.BlockSpec(shp, lambda b, s: (0,) * len(shp))
    in_specs = [tok(OFF_RKV // W_B), tok(OFF_RKV // W_B + 1), tok(OFF_RKV // W_B + 2),
                pl.BlockSpec((None, tm, LORA_W), lambda b, s: (b, s, OFF_LORA // LORA_W))]
    args = [proj3, proj3, proj3, proj3]
    if has_vres:
        in_specs.append(tok(0))
        args.append(v_first)
    in_specs += [full((3, W_B)), full((1, LORA_W)), full((8, W_B)), full((LORA_W, 4 * W_B)),
                 full((W_B, W_B)), full((tm, tm)), full((tm, tm))]
    args += [mu_rkv, mu_lora, params, w_lora, gmat, *_chunk_matrices(tm)]
    out32 = jax.ShapeDtypeStruct((B, S, W_B), F32)
    out16 = jax.ShapeDtypeStruct((B, S, W_B), BF16)
    return pl.pallas_call(
        functools.partial(_rwkv_prep_kernel, has_vres=has_vres),
        out_shape=(out32,) * 5 + (out16,) * 6,
        grid=(B, S // tm),
        in_specs=in_specs,
        out_specs=(tok(0),) * 11,
        scratch_shapes=[pltpu.VMEM((8, W_B), F32), pltpu.VMEM((8, LORA_W), F32)],
        compiler_params=_cparams(("parallel", "arbitrary")),
        name="rwkv_prep",
    )(*args)


SCAN_HPG = 8


def _rwkv_scan_kernel(at_ref, rt_ref, bt_ref, kt_ref, bt2_ref, kt2_ref, v_ref, gl_ref, o_ref,
                      state_ref):
    C, N, H = CHUNK, N_B, SCAN_HPG

    @pl.when(pl.program_id(2) == 0)
    def _():
        state_ref[...] = jnp.zeros_like(state_ref)

    row = lax.broadcasted_iota(jnp.int32, (C, C), 0)
    col = lax.broadcasted_iota(jnp.int32, (C, C), 1)
    incl = (col <= row).astype(F32)
    strict = (col < row).astype(F32)
    eye = (col == row).astype(F32)
    nt = (((1,), (1,)), ((), ()))
    dot = functools.partial(jnp.dot, preferred_element_type=F32)
    dot_nt = lambda x, y: lax.dot_general(x, y, nt, preferred_element_type=F32)
    hs = range(H)
    sl = [slice(h * N, (h + 1) * N) for h in hs]

    x = [jnp.concatenate([at_ref[:, sl[h]], rt_ref[:, sl[h]]], axis=0) for h in hs]
    v = [v_ref[:, sl[h]] for h in hs]
    vb = [v[h].astype(BF16) for h in hs]
    st = [state_ref[h] for h in hs]
    mb = [dot_nt(x[h], bt_ref[:, sl[h]]) for h in hs]
    mk = [dot_nt(x[h], kt_ref[:, sl[h]]) for h in hs]
    xh = [dot_nt(x[h], st[h].astype(BF16)) for h in hs]
    lab = [mb[h][:C] * strict for h in hs]
    mrb = [(mb[h][C:] * incl).astype(BF16) for h in hs]
    lm = [jnp.concatenate([mk[h][:C] * strict, mk[h][C:] * incl], axis=0).astype(BF16) for h in hs]
    lv = [dot(lm[h], vb[h]) for h in hs]
    t = [eye + lab[h] for h in hs]
    p = lab
    for _ in range(5):
        pb = [p[h].astype(BF16) for h in hs]
        p = [dot(pb[h], pb[h]) for h in hs]
        t = [t[h] + dot(t[h].astype(BF16), p[h].astype(BF16)) for h in hs]
    u = [dot(t[h].astype(BF16), (xh[h][:C] + lv[h][:C]).astype(BF16)) for h in hs]
    o = [xh[h][C:] + dot(mrb[h], u[h].astype(BF16)) + lv[h][C:] for h in hs]
    for h in hs:
        o_ref[:, sl[h]] = o[h]
    uv_t = []
    for h in hs:
        uv = jnp.concatenate([u[h], v[h]], axis=0)
        uv_t.append(jnp.concatenate([uv, jnp.zeros_like(uv)], axis=1).T[:N].astype(BF16))
    for h in hs:
        y2 = jnp.concatenate([bt2_ref[:, sl[h]], kt2_ref[:, sl[h]]], axis=0)
        state_ref[h] = st[h] * gl_ref[0:1, sl[h]] + dot(uv_t[h], y2)


def rwkv_scan(at, rt, bt, kt, bt2, kt2, v, gl):
    B, S, _ = v.shape
    wb = SCAN_HPG * N_B
    spec = pl.BlockSpec((None, CHUNK, wb), lambda b, g, c: (b, c, g))
    return pl.pallas_call(
        _rwkv_scan_kernel,
        out_shape=jax.ShapeDtypeStruct((B, S, W_B), F32),
        grid=(B, H_B // SCAN_HPG, S // CHUNK),
        in_specs=[spec] * 8,
        out_specs=spec,
        scratch_shapes=[pltpu.VMEM((SCAN_HPG, N_B, N_B), F32)],
        compiler_params=_cparams(("parallel", "parallel", "arbitrary")),
        name="rwkv_scan",
    )(at, rt, bt, kt, bt2, kt2, v, gl)


def _rwkv_post_kernel(o_ref, r_ref, k_ref, v_ref, g_ref, par_ref, gm_ref, out_ref):
    o = o_ref[...]
    mu = _group_sum(o, gm_ref) * (1.0 / N_B)
    oc = o - mu
    var = _group_sum(oc * oc, gm_ref) * (1.0 / N_B)
    on = oc * lax.rsqrt(var + LNX_EPS) * par_ref[0:1, :] + par_ref[1:2, :]
    bonus = _group_sum(r_ref[...] * k_ref[...] * par_ref[2:3, :], gm_ref) * v_ref[...]
    out_ref[...] = ((on + bonus) * g_ref[...]).astype(BF16)


def rwkv_post(o, r, k, v, g, params, gmat, tm=256):
    M = o.shape[0]
    row = pl.BlockSpec((tm, W_B), lambda i: (i, 0))
    return pl.pallas_call(
        _rwkv_post_kernel,
        out_shape=jax.ShapeDtypeStruct((M, W_B), BF16),
        grid=(M // tm,),
        in_specs=[row] * 5 + [pl.BlockSpec((8, W_B), lambda i: (0, 0)),
                              pl.BlockSpec((W_B, W_B), lambda i: (0, 0))],
        out_specs=row,
        compiler_params=_cparams(("parallel",)),
        name="rwkv_post",
    )(o, r, k, v, g, params, gmat)


def _merge_kernel(oa_ref, ob_ref, oc_ref, wa_ref, wb_ref, wc_ref, g0_ref, g1_ref, g2_ref, out_ref):
    ya = jnp.dot(oa_ref[...], wa_ref[...], preferred_element_type=F32)
    yb = jnp.dot(ob_ref[...], wb_ref[...], preferred_element_type=F32)
    yc = jnp.dot(oc_ref[...], wc_ref[...], preferred_element_type=F32)
    out = (jax.nn.sigmoid(g0_ref[...]) * ya + jax.nn.sigmoid(g1_ref[...]) * yb
           + jax.nn.sigmoid(g2_ref[...]) * yc)
    out_ref[...] = out.astype(BF16)


def gated_merge(o_a, o_b, o_c, w_pa, w_pb, w_pc, proj2, tm=512, tn=1024):
    M = o_a.shape[0]
    D = D_MODEL
    nj = D // tn
    act = lambda w: pl.BlockSpec((tm, w), lambda i, j: (i, 0))
    wgt = lambda w: pl.BlockSpec((w, tn), lambda i, j: (0, j))
    gate = lambda br: pl.BlockSpec((tm, tn), lambda i, j: (i, OFF_GATE // tn + br * nj + j))
    return pl.pallas_call(
        _merge_kernel,
        out_shape=jax.ShapeDtypeStruct((M, D), BF16),
        grid=(M // tm, nj),
        in_specs=[act(W_A), act(W_B), act(W_M), wgt(W_A), wgt(W_B), wgt(W_M),
                  gate(0), gate(1), gate(2)],
        out_specs=pl.BlockSpec((tm, tn), lambda i, j: (i, j)),
        compiler_params=_cparams(("parallel", "arbitrary")),
        name="gated_merge",
    )(o_a, o_b, o_c, w_pa, w_pb, w_pc, proj2, proj2, proj2)


def _first_index_of_max(x, idx, n):
    m = jnp.max(x, axis=0, keepdims=True)
    first = jnp.min(jnp.where(x == m, idx, n), axis=0, keepdims=True)
    return m, idx == first


def _router_kernel(h_ref, wr_ref, bias_ref, gate_ref):
    E, G = N_EXPERTS, N_GROUPS
    per = E // G
    logits = lax.dot_general(wr_ref[...], h_ref[...], (((1,), (1,)), ((), ())),
                             preferred_element_type=F32)
    tm = logits.shape[1]
    scores = jax.nn.sigmoid(logits)
    biased = scores + jnp.concatenate([bias_ref[...]] * (tm // 128), axis=1)
    idx8 = lax.broadcasted_iota(jnp.int32, (per, tm), 0)
    gscores = []
    for g in range(G):
        xg = biased[g * per:(g + 1) * per, :]
        m1, hit = _first_index_of_max(xg, idx8, per)
        m2 = jnp.max(jnp.where(hit, -jnp.inf, xg), axis=0, keepdims=True)
        gscores.append(m1 + m2)
    gs = jnp.concatenate(gscores, axis=0)
    gidx = lax.broadcasted_iota(jnp.int32, (G, tm), 0)
    gsel = jnp.zeros((G, tm), F32)
    for _ in range(TOPK_GROUPS):
        _, hit = _first_index_of_max(gs, gidx, G)
        gsel = jnp.where(hit, 1.0, gsel)
        gs = jnp.where(hit, -jnp.inf, gs)
    emask = jnp.concatenate(
        [jnp.broadcast_to(gsel[g:g + 1, :], (per, tm)) for g in range(G)], axis=0)
    cand = jnp.where(emask > 0.0, biased, -jnp.inf)
    eidx = lax.broadcasted_iota(jnp.int32, (E, tm), 0)
    picked = jnp.zeros((E, tm), F32)
    for _ in range(TOP_K):
        _, hit = _first_index_of_max(cand, eidx, E)
        picked = jnp.where(hit, scores, picked)
        cand = jnp.where(hit, -jnp.inf, cand)
    gate = picked / jnp.sum(picked, axis=0, keepdims=True) * ROUTED_SCALE
    shared = (eidx == 0).astype(F32)
    gate_ref[...] = jnp.concatenate([gate, shared], axis=0).T


def moe_router(hb, w_router_t, bias_b, tm=256):
    M, D = hb.shape
    return pl.pallas_call(
        _router_kernel,
        out_shape=jax.ShapeDtypeStruct((M, 2 * N_EXPERTS), F32),
        grid=(M // tm,),
        in_specs=[pl.BlockSpec((tm, D), lambda i: (i, 0)),
                  pl.BlockSpec((N_EXPERTS, D), lambda i: (0, 0)),
                  pl.BlockSpec((N_EXPERTS, 128), lambda i: (0, 0))],
        out_specs=pl.BlockSpec((tm, 2 * N_EXPERTS), lambda i: (i, 0)),
        compiler_params=_cparams(("parallel",)),
        name="moe_router",
    )(hb, w_router_t, bias_b)


MOE_F = 6656


def _moe_act_kernel(x_ref, gate_ref, e_ref, w1_ref, w3_ref, o_ref):
    x = x_ref[...]
    h1 = jnp.dot(x, w1_ref[...], preferred_element_type=F32)
    h3 = jnp.dot(x, w3_ref[...], preferred_element_type=F32)
    g0, g1, g2 = _split3(gate_ref[...])
    e = e_ref[...]
    gexp = (jnp.dot(g0, e, preferred_element_type=F32) + jnp.dot(g1, e, preferred_element_type=F32)
            + jnp.dot(g2, e, preferred_element_type=F32))
    o_ref[...] = (h1 * jax.nn.sigmoid(h1) * h3 * gexp).astype(BF16)


def moe_hidden(hb, gate, expand, w1c, w3c, tm=1024, tn=512):
    M, D = hb.shape
    wspec = pl.BlockSpec((D, tn), lambda i, j: (0, j))
    return pl.pallas_call(
        _moe_act_kernel,
        out_shape=jax.ShapeDtypeStruct((M, MOE_F), BF16),
        grid=(M // tm, MOE_F // tn),
        in_specs=[pl.BlockSpec((tm, D), lambda i, j: (i, 0)),
                  pl.BlockSpec((tm, 2 * N_EXPERTS), lambda i, j: (i, 0)),
                  pl.BlockSpec((2 * N_EXPERTS, tn), lambda i, j: (0, j)),
                  wspec, wspec],
        out_specs=pl.BlockSpec((tm, tn), lambda i, j: (i, j)),
        compiler_params=_cparams(("parallel", "arbitrary")),
        name="moe_hidden",
    )(hb, gate, expand, w1c, w3c)


def _regroup_w_in(w, has_vres):
    D = w.shape[0]
    o = np.cumsum((0, W_A, R_KV, H_IDX * D_IDX, D_IDX, H_IDX, 3 * W_B, W_LORA, A_LORA, G_LORA, W_M,
                   3 * D_MODEL)).tolist()
    q_a, c_kv, q_idx = w[:, o[0]:o[1]], w[:, o[1]:o[2]], w[:, o[2]:o[3]]
    idx = w[:, o[3]:o[5]]
    rkv, lora = w[:, o[5]:o[6]], w[:, o[6]:o[9]]
    q_m, gates = w[:, o[9]:o[10]], w[:, o[10]:o[11]]
    zeros = lambda n: jnp.zeros((D, n), w.dtype)
    vres = w[:, o[11]:o[11] + V_LORA] if has_vres else zeros(V_LORA)
    n_lora = W_LORA + A_LORA + G_LORA + V_LORA
    parts = [gates, q_a, q_idx, rkv, q_m, c_kv, lora, vres, zeros(LORA_W - n_lora),
             idx, zeros(IDX_W - D_IDX - H_IDX)]
    used = sum(p.shape[1] for p in parts)
    parts.append(zeros(N_PROJ - used))
    return jnp.concatenate(parts, axis=1).astype(BF16)


def _lora_up_matrix(w_up, a_up, g_up, v_up):
    m = jnp.zeros((LORA_W, 4 * W_B), F32)
    m = m.at[0:W_LORA, 0:W_B].set(w_up)
    m = m.at[W_LORA:W_LORA + A_LORA, W_B:2 * W_B].set(a_up)
    m = m.at[W_LORA + A_LORA:W_LORA + A_LORA + G_LORA, 2 * W_B:3 * W_B].set(g_up)
    if v_up is not None:
        r0 = W_LORA + A_LORA + G_LORA
        m = m.at[r0:r0 + V_LORA, 3 * W_B:4 * W_B].set(v_up)
    return m.astype(BF16)


def _rows(vectors, n_rows, width):
    m = jnp.zeros((n_rows, width), F32)
    for i, vec in enumerate(vectors):
        m = m.at[i, :vec.shape[0]].set(vec)
    return m


def _expert_in_weights(w_e, w_s):
    D = w_e.shape[1]
    routed = jnp.transpose(w_e, (1, 0, 2)).reshape(D, N_EXPERTS * F_EXPERT)
    pad = jnp.zeros((D, MOE_F - N_EXPERTS * F_EXPERT - w_s.shape[1]), w_e.dtype)
    return jnp.concatenate([routed, w_s, pad], axis=1).astype(BF16)


def _expert_out_weights(w_e2, w_s2):
    D = w_e2.shape[2]
    routed = w_e2.reshape(N_EXPERTS * F_EXPERT, D)
    pad = jnp.zeros((MOE_F - N_EXPERTS * F_EXPERT - w_s2.shape[0], D), w_e2.dtype)
    return jnp.concatenate([routed, w_s2, pad], axis=0).astype(BF16)


def kernel(x, mem, ln0_g, ln0_b, w_in_first, w_in_rest, ckv_g, w_uk, w_uv, mu_rwkv, mu_vres, rw_w0, rw_w_up, rw_a0, rw_a_up, rw_g_up, rw_v0, rw_v_up, rw_k_k, rw_k_a, rw_r_k, rw_lnx_g, rw_lnx_b, w_mk, w_mv, w_pa, w_pb, w_pc, w_o, ln1_g, ln1_b, w_router, router_bias, w_e1, w_e3, w_e2, w_s1, w_s3, w_s2, ln2_g, ln2_b):
    B, S, D = x.shape
    T = B * S
    gmat = jnp.kron(jnp.eye(H_B, dtype=F32), jnp.ones((N_B, N_B), F32)).astype(BF16)
    unit = np.arange(MOE_F)
    owner = np.where(unit < (N_EXPERTS + 1) * F_EXPERT, unit // F_EXPERT, -1)
    expand = jnp.asarray(owner[None, :] == np.arange(2 * N_EXPERTS)[:, None], BF16)
    mem_b = mem.reshape(B * N_MEM, D).astype(BF16)

    h, hb = layer_norm(x.reshape(T, D), ln0_g, ln0_b)
    v_first = None
    for i in range(DEPTH):
        has_vres = i > 0
        w_in = _regroup_w_in(w_in_first if i == 0 else w_in_rest[i - 1], has_vres)
        proj2 = matmul(hb, w_in, F32, 1024, 1024)
        proj3 = proj2.reshape(B, S, N_PROJ)

        k2, v2 = kv_project(proj2, ckv_g[i], w_uk[i].astype(BF16), w_uv[i].astype(BF16))
        bias5 = indexer_bias(proj3)
        o_a = sparse_attention(proj3, k2.reshape(B, S, W_A), v2.reshape(B, S, W_A), bias5)

        mu = mu_rwkv[i]
        mu_rkv = mu[:3 * W_B].reshape(3, W_B)
        mu_l = [mu[3 * W_B:]] + ([mu_vres[i - 1]] if has_vres else [])
        mu_lora = _rows([jnp.concatenate(mu_l)], 1, LORA_W)
        prep_par = _rows([rw_w0[i], rw_a0[i], rw_v0[i - 1] if has_vres else jnp.zeros((W_B,), F32),
                          rw_k_k[i], rw_k_a[i]], 8, W_B)
        w_lora = _lora_up_matrix(rw_w_up[i], rw_a_up[i], rw_g_up[i],
                                 rw_v_up[i - 1] if has_vres else None)
        r_b, k_b, v_b, g_b, gl_b, *scan_ops = rwkv_prep(proj3, v_first, mu_rkv, mu_lora, prep_par,
                                                        w_lora, gmat)
        if i == 0:
            v_first = v_b
        o_scan = rwkv_scan(*scan_ops, v_b, gl_b)
        post_par = _rows([rw_lnx_g[i], rw_lnx_b[i], rw_r_k[i].reshape(W_B)], 8, W_B)
        flat = lambda z: z.reshape(T, W_B)
        o_b = rwkv_post(flat(o_scan), flat(r_b), flat(k_b), flat(v_b), flat(g_b), post_par, gmat)

        km = matmul(mem_b, w_mk[i].astype(BF16), BF16, 512, 512).reshape(B, N_MEM, W_M)
        vm = matmul(mem_b, w_mv[i].astype(BF16), BF16, 512, 512).reshape(B, N_MEM, W_M)
        o_c = memory_attention(proj3, km, vm)

        merged = gated_merge(o_a.reshape(T, W_A), o_b, o_c.reshape(T, W_M), w_pa[i].astype(BF16),
                             w_pb[i].astype(BF16), w_pc[i].astype(BF16), proj2)
        h, hb = matmul_residual_ln(merged, w_o[i].astype(BF16), h, ln1_g[i], ln1_b[i])

        bias_b = jnp.broadcast_to(router_bias[i].reshape(N_EXPERTS, 1), (N_EXPERTS, 128))
        gate = moe_router(hb, w_router[i].T.astype(BF16), bias_b)
        hidden = moe_hidden(hb, gate, expand, _expert_in_weights(w_e1[i], w_s1[i]),
                            _expert_in_weights(w_e3[i], w_s3[i]))
        h, hb = matmul_residual_ln(hidden, _expert_out_weights(w_e2[i], w_s2[i]), h,
                                   ln2_g[i], ln2_b[i])
    return h.reshape(B, S, D)
```

```python
import functools

import jax
import jax.numpy as jnp
import numpy as np
from jax import lax
from jax.experimental import pallas as pl
from jax.experimental.pallas import tpu as pltpu

F32 = jnp.float32
BF16 = jnp.bfloat16

D_MODEL = 4096
DEPTH = 4
CHUNK = 64
H_A, DH_A = 16, 128
W_A = H_A * DH_A
R_KV = 256
H_IDX, D_IDX = 32, 64
TOPK_MAX = 256
H_B, N_B = 16, 64
W_B = H_B * N_B
W_LORA, A_LORA, G_LORA, V_LORA = 64, 64, 160, 32
LNX_EPS = 64e-5
N_MEM = 256
H_M, DH_M = 4, 256
W_M = H_M * DH_M
N_EXPERTS, N_GROUPS, TOPK_GROUPS, TOP_K = 64, 8, 4, 8
F_EXPERT = 96
ROUTED_SCALE = 2.5
ALPHA = (2 * DEPTH) ** 0.25
LN_EPS = 1e-5

OFF_GATE = 0
OFF_QA = 3 * D_MODEL
OFF_QIDX = OFF_QA + W_A
OFF_RKV = OFF_QIDX + H_IDX * D_IDX
OFF_QM = OFF_RKV + 3 * W_B
OFF_CKV = OFF_QM + W_M
OFF_LORA = OFF_CKV + R_KV
LORA_W = 384
OFF_IDX = OFF_LORA + LORA_W
IDX_W = 128
N_PROJ = 21504

NEG_BIAS = -1e30
VMEM_LIMIT = 56 * 1024 * 1024


def _cparams(sem):
    return pltpu.CompilerParams(dimension_semantics=sem, vmem_limit_bytes=VMEM_LIMIT)


def _split2(x):
    hi = x.astype(BF16)
    lo = (x - hi.astype(F32)).astype(BF16)
    return hi, lo


def _split3(x):
    hi = x.astype(BF16)
    r1 = x - hi.astype(F32)
    mid = r1.astype(BF16)
    lo = (r1 - mid.astype(F32)).astype(BF16)
    return hi, mid, lo


def _mm_kernel(x_ref, w_ref, o_ref):
    o_ref[...] = jnp.dot(x_ref[...], w_ref[...], preferred_element_type=F32).astype(o_ref.dtype)


def matmul(x, w, out_dtype, tm, tn):
    M, K = x.shape
    N = w.shape[1]
    tm, tn = min(tm, M), min(tn, N)
    assert M % tm == 0 and N % tn == 0
    return pl.pallas_call(
        _mm_kernel,
        out_shape=jax.ShapeDtypeStruct((M, N), out_dtype),
        grid=(M // tm, N // tn),
        in_specs=[pl.BlockSpec((tm, K), lambda i, j: (i, 0)),
                  pl.BlockSpec((K, tn), lambda i, j: (0, j))],
        out_specs=pl.BlockSpec((tm, tn), lambda i, j: (i, j)),
        compiler_params=_cparams(("parallel", "arbitrary")),
        name="matmul",
    )(x, w)


def _ln_rows(y, g, b):
    mu = jnp.mean(y, axis=-1, keepdims=True)
    yc = y - mu
    var = jnp.mean(yc * yc, axis=-1, keepdims=True)
    return yc * lax.rsqrt(var + LN_EPS) * g + b


def _ln_kernel(x_ref, g_ref, b_ref, o_ref, ob_ref):
    y = _ln_rows(x_ref[...], g_ref[...], b_ref[...])
    o_ref[...] = y
    ob_ref[...] = y.astype(BF16)


def _res_ln_kernel(h_ref, f_ref, g_ref, b_ref, o_ref, ob_ref):
    y = _ln_rows(ALPHA * h_ref[...] + f_ref[...], g_ref[...], b_ref[...])
    o_ref[...] = y
    ob_ref[...] = y.astype(BF16)


def layer_norm(x, g, b, tm=256):
    M, D = x.shape
    row = pl.BlockSpec((tm, D), lambda i: (i, 0))
    vec = pl.BlockSpec((1, D), lambda i: (0, 0))
    return pl.pallas_call(
        _ln_kernel,
        out_shape=(jax.ShapeDtypeStruct((M, D), F32), jax.ShapeDtypeStruct((M, D), BF16)),
        grid=(M // tm,),
        in_specs=[row, vec, vec],
        out_specs=(row, row),
        compiler_params=_cparams(("parallel",)),
        name="layer_norm",
    )(x, g.reshape(1, D), b.reshape(1, D))


def residual_layer_norm(h, f, g, b, tm=256):
    M, D = h.shape
    row = pl.BlockSpec((tm, D), lambda i: (i, 0))
    vec = pl.BlockSpec((1, D), lambda i: (0, 0))
    return pl.pallas_call(
        _res_ln_kernel,
        out_shape=(jax.ShapeDtypeStruct((M, D), F32), jax.ShapeDtypeStruct((M, D), BF16)),
        grid=(M // tm,),
        in_specs=[row, row, vec, vec],
        out_specs=(row, row),
        compiler_params=_cparams(("parallel",)),
        name="residual_layer_norm",
    )(h, f, g.reshape(1, D), b.reshape(1, D))


def _mm_res_ln_kernel(x_ref, w_ref, h_ref, g_ref, b_ref, o_ref, ob_ref, *, nj, tn):
    j = pl.program_id(1)
    y = jnp.dot(x_ref[...], w_ref[...], preferred_element_type=F32)
    for jj in range(nj):
        @pl.when(j == jj)
        def _(jj=jj):
            o_ref[:, jj * tn:(jj + 1) * tn] = y

    @pl.when(j == nj - 1)
    def _():
        D = nj * tn
        s1 = None
        for jj in range(nj):
            sl = slice(jj * tn, (jj + 1) * tn)
            z = ALPHA * h_ref[:, sl] + o_ref[:, sl]
            o_ref[:, sl] = z
            p = jnp.sum(z, axis=-1, keepdims=True)
            s1 = p if s1 is None else s1 + p
        mu = s1 * (1.0 / D)
        s2 = None
        for jj in range(nj):
            zc = o_ref[:, jj * tn:(jj + 1) * tn] - mu
            p = jnp.sum(zc * zc, axis=-1, keepdims=True)
            s2 = p if s2 is None else s2 + p
        rstd = lax.rsqrt(s2 * (1.0 / D) + LN_EPS)
        for jj in range(nj):
            sl = slice(jj * tn, (jj + 1) * tn)
            z = (o_ref[:, sl] - mu) * rstd * g_ref[:, sl] + b_ref[:, sl]
            o_ref[:, sl] = z
            ob_ref[:, sl] = z.astype(BF16)


def matmul_residual_ln(x, w, h, g, b, tm=512, tn=512):
    M, K = x.shape
    D = w.shape[1]
    nj = D // tn
    row = pl.BlockSpec((tm, D), lambda i, j: (i, 0), pipeline_mode=pl.Buffered(1))
    vec = pl.BlockSpec((1, D), lambda i, j: (0, 0))
    return pl.pallas_call(
        functools.partial(_mm_res_ln_kernel, nj=nj, tn=tn),
        out_shape=(jax.ShapeDtypeStruct((M, D), F32), jax.ShapeDtypeStruct((M, D), BF16)),
        grid=(M // tm, nj),
        in_specs=[pl.BlockSpec((tm, K), lambda i, j: (i, 0), pipeline_mode=pl.Buffered(1)),
                  pl.BlockSpec((K, tn), lambda i, j: (0, j)),
                  row, vec, vec],
        out_specs=(row, row),
        compiler_params=_cparams(("parallel", "arbitrary")),
        name="matmul_residual_ln",
    )(x, w, h, g.reshape(1, D), b.reshape(1, D))


def _kv_kernel(c_ref, g_ref, wk_ref, wv_ref, k_ref, v_ref):
    c = c_ref[...]
    ms = jnp.mean(c * c, axis=-1, keepdims=True)
    cn = (c * lax.rsqrt(ms + LN_EPS) * g_ref[...]).astype(BF16)
    k_ref[...] = jnp.dot(cn, wk_ref[...], preferred_element_type=F32).astype(BF16)
    v_ref[...] = jnp.dot(cn, wv_ref[...], preferred_element_type=F32).astype(BF16)


def kv_project(proj2, ckv_g, w_uk, w_uv, tm=512):
    M = proj2.shape[0]
    out = jax.ShapeDtypeStruct((M, W_A), BF16)
    wspec = pl.BlockSpec((R_KV, W_A), lambda i: (0, 0))
    ospec = pl.BlockSpec((tm, W_A), lambda i: (i, 0))
    return pl.pallas_call(
        _kv_kernel,
        out_shape=(out, out),
        grid=(M // tm,),
        in_specs=[pl.BlockSpec((tm, R_KV), lambda i: (i, OFF_CKV // R_KV)),
                  pl.BlockSpec((1, R_KV), lambda i: (0, 0)), wspec, wspec],
        out_specs=(ospec, ospec),
        compiler_params=_cparams(("parallel",)),
        name="kv_project",
    )(proj2, ckv_g.reshape(1, R_KV), w_uk, w_uv)


IDX_TQ = 128
IDX_TK = 256
KEY_NEG_INF = -2139095041
INT_MIN = -2147483648


def _indexer_kernel(q_ref, iq_ref, ik_ref, bias_ref, q2_ref, wib_ref, key_ref, *, n_kt, ksel):
    tq, tk = IDX_TQ, IDX_TK
    i = pl.program_id(1)
    q = q_ref[...]
    wi = iq_ref[...][:, D_IDX:D_IDX + H_IDX] * (H_IDX ** -0.5)
    for h in range(H_IDX):
        q2_ref[h] = (q[:, h * D_IDX:(h + 1) * D_IDX] * (D_IDX ** -0.5)).astype(BF16)
        wib_ref[h] = jnp.broadcast_to(wi[:, h:h + 1], (tq, 128))

    n_t = ((i + 1) * tq + tk - 1) // tk
    q_chunk = (i * tq + lax.broadcasted_iota(jnp.int32, (tq, tk), 0)) // CHUNK
    col_iota = lax.broadcasted_iota(jnp.int32, (tq, tk), 1)

    def score_tile(j, carry):
        kt = ik_ref[pl.ds(pl.multiple_of(j * tk, tk), tk), :][:, :D_IDX].astype(BF16)
        acc = jnp.zeros((tq, tk), F32)
        for h in range(H_IDX):
            z = lax.dot_general(q2_ref[h], kt, (((1,), (1,)), ((), ())),
                                preferred_element_type=F32)
            w = wib_ref[h]
            acc = acc + jnp.concatenate([w] * (tk // 128), axis=1) * jnp.maximum(z, 0.0)
        admissible = ((j * tk + col_iota) // CHUNK) <= q_chunk
        score = jnp.where(admissible, acc, -jnp.inf)
        bits = lax.bitcast_convert_type(score, jnp.int32)
        key_ref[j] = bits ^ ((bits >> 31) & 0x7FFFFFFF)
        return carry

    lax.fori_loop(0, n_t, score_tile, 0)

    def bit_step(it, thr):
        cand = thr + lax.shift_left(jnp.int32(1), 31 - it)
        cand_b = jnp.concatenate([cand] * (tk // 128), axis=1)

        def count_tile(j, cnt):
            c = (key_ref[j] >= cand_b).astype(jnp.int32)
            for s in range(tk // 128):
                cnt = cnt + c[:, s * 128:(s + 1) * 128]
            return cnt

        cnt = lax.fori_loop(0, n_t, count_tile, jnp.zeros((tq, 128), jnp.int32))
        total = jnp.sum(cnt.astype(F32), axis=1, keepdims=True)
        return jnp.where(total >= float(ksel), cand, thr)

    thr = lax.fori_loop(0, 32, bit_step, jnp.full((tq, 128), INT_MIN, jnp.int32))
    thr_b = jnp.concatenate([thr] * (tk // 128), axis=1)

    def bias_tile(j, carry):
        key = key_ref[j]
        sel = (key >= thr_b) & (key > KEY_NEG_INF)
        bias_ref[j] = jnp.where(sel, 0.0, NEG_BIAS).astype(BF16)
        return carry

    lax.fori_loop(0, n_t, bias_tile, 0)

    def fill_tile(j, carry):
        bias_ref[j] = jnp.full((tq, tk), NEG_BIAS, BF16)
        return carry

    lax.fori_loop(n_t, n_kt, fill_tile, 0)


def indexer_bias(proj3):
    B, S, _ = proj3.shape
    tq, tk = IDX_TQ, IDX_TK
    n_qt, n_kt = S // tq, S // tk
    ksel = min(TOPK_MAX, S // 4)
    return pl.pallas_call(
        functools.partial(_indexer_kernel, n_kt=n_kt, ksel=ksel),
        out_shape=jax.ShapeDtypeStruct((B, n_qt, n_kt, tq, tk), BF16),
        grid=(B, n_qt),
        in_specs=[pl.BlockSpec((None, tq, H_IDX * D_IDX), lambda b, i: (b, i, OFF_QIDX // (H_IDX * D_IDX))),
                  pl.BlockSpec((None, tq, IDX_W), lambda b, i: (b, i, OFF_IDX // IDX_W)),
                  pl.BlockSpec((None, S, IDX_W), lambda b, i: (b, 0, OFF_IDX // IDX_W))],
        out_specs=pl.BlockSpec((None, None, n_kt, tq, tk), lambda b, i: (b, i, 0, 0, 0)),
        scratch_shapes=[pltpu.VMEM((H_IDX, tq, D_IDX), BF16),
                        pltpu.VMEM((H_IDX, tq, 128), F32),
                        pltpu.VMEM((n_kt, tq, tk), jnp.int32)],
        compiler_params=_cparams(("parallel", "parallel")),
        name="indexer_bias",
    )(proj3, proj3, proj3)


ATT_TQ = 256
ATT_HP = 2
ATT_TK = 512


ATT_RG = 16


def _attention_kernel(q_ref, k_ref, v_ref, bias_ref, o_ref, qs_ref, s_ref, p_ref, m_ref, a_ref,
                      acc_ref):
    tq, tk, hp = ATT_TQ, ATT_TK, ATT_HP
    nq = tq // IDX_TQ
    nb = tk // IDX_TK
    i = pl.program_id(2)
    scale = DH_A ** -0.5 * np.log2(np.e)
    nt = (((1,), (1,)), ((), ()))
    streams = [(hh, qq) for hh in range(hp) for qq in range(nq)]
    cols = [slice(hh * DH_A, (hh + 1) * DH_A) for hh in range(hp)]
    rows = [slice(qq * IDX_TQ, (qq + 1) * IDX_TQ) for qq in range(nq)]
    for c, (hh, qq) in enumerate(streams):
        qs_ref[c] = (q_ref[rows[qq], cols[hh]] * scale).astype(BF16)
    m_ref[...] = jnp.full(m_ref.shape, NEG_BIAS, F32)
    acc_ref[...] = jnp.zeros_like(acc_ref)
    p_ref[1] = jnp.zeros(p_ref.shape[1:], BF16)
    a_ref[1] = jnp.zeros(a_ref.shape[1:], F32)
    n_kv = k_ref.shape[0] // tk
    n_t = ((i + 1) * tq + tk - 1) // tk
    ones = jnp.ones((tk, DH_A), BF16)

    def logits(j, slot):
        row0 = pl.multiple_of(j * tk, tk)
        for c, (hh, qq) in enumerate(streams):
            s_ref[slot, c] = lax.dot_general(qs_ref[c], k_ref[pl.ds(row0, tk), cols[hh]], nt,
                                             preferred_element_type=F32)

    def weighted_values(j, slot):
        row0 = pl.multiple_of(j * tk, tk)
        return [jnp.dot(p_ref[slot, c],
                        jnp.concatenate([v_ref[pl.ds(row0, tk), cols[hh]], ones], axis=1),
                        preferred_element_type=F32) for c, (hh, qq) in enumerate(streams)]

    def accumulate(pv, slot):
        for c in range(len(streams)):
            acc_ref[c] = jnp.concatenate([a_ref[slot, c]] * 2, axis=1) * acc_ref[c] + pv[c]

    def stage(j, slot):
        other = 1 - slot
        logits(jnp.minimum(j + 1, n_kv - 1), other)
        pv = weighted_values(jnp.maximum(j - 1, 0), other)
        for c, (hh, qq) in enumerate(streams):
            for r in range(IDX_TQ // ATT_RG):
                rs = slice(r * ATT_RG, (r + 1) * ATT_RG)
                bias = jnp.concatenate([bias_ref[qq, nb * j + t, rs, :] for t in range(nb)], axis=1)
                sc = s_ref[slot, c, rs, :] + bias.astype(F32)
                m_old = m_ref[c, rs, :]
                m_new = jnp.maximum(m_old, jnp.max(sc, axis=-1, keepdims=True))
                a_ref[slot, c, rs, :] = jnp.exp2(m_old - m_new)
                m_ref[c, rs, :] = m_new
                p_ref[slot, c, rs, :] = jnp.exp2(
                    sc - jnp.concatenate([m_new] * (tk // 128), axis=1)).astype(BF16)
        accumulate(pv, other)

    logits(0, 0)

    def body(jj, carry):
        stage(2 * jj, 0)
        stage(2 * jj + 1, 1)
        return carry

    n_pairs = (n_t + 1) // 2
    lax.fori_loop(0, n_pairs, body, 0)
    accumulate(weighted_values(2 * n_pairs - 1, 1), 1)
    for c, (hh, qq) in enumerate(streams):
        o_ref[rows[qq], cols[hh]] = (acc_ref[c, :, :DH_A] / acc_ref[c, :, DH_A:]).astype(BF16)


def sparse_attention(proj3, k3, v3, bias5):
    B, S, _ = proj3.shape
    tq, hp = ATT_TQ, ATT_HP
    wb = hp * DH_A
    n_kt = S // IDX_TK
    n_streams = hp * (tq // IDX_TQ)
    assert S % (2 * ATT_TK) == 0
    return pl.pallas_call(
        _attention_kernel,
        out_shape=jax.ShapeDtypeStruct((B, S, W_A), BF16),
        grid=(B, H_A // hp, S // tq),
        in_specs=[pl.BlockSpec((None, tq, wb), lambda b, h, i: (b, i, OFF_QA // wb + h)),
                  pl.BlockSpec((None, S, wb), lambda b, h, i: (b, 0, h)),
                  pl.BlockSpec((None, S, wb), lambda b, h, i: (b, 0, h)),
                  pl.BlockSpec((None, tq // IDX_TQ, n_kt, IDX_TQ, IDX_TK),
                               lambda b, h, i: (b, i, 0, 0, 0))],
        out_specs=pl.BlockSpec((None, tq, wb), lambda b, h, i: (b, i, h)),
        scratch_shapes=[pltpu.VMEM((n_streams, IDX_TQ, DH_A), BF16),
                        pltpu.VMEM((2, n_streams, IDX_TQ, ATT_TK), F32),
                        pltpu.VMEM((2, n_streams, IDX_TQ, ATT_TK), BF16),
                        pltpu.VMEM((n_streams, IDX_TQ, 128), F32),
                        pltpu.VMEM((2, n_streams, IDX_TQ, 128), F32),
                        pltpu.VMEM((n_streams, IDX_TQ, 2 * DH_A), F32)],
        compiler_params=_cparams(("parallel", "parallel", "parallel")),
        name="sparse_attention",
    )(proj3, k3, v3, bias5)


def _mem_attention_kernel(q_ref, km_ref, vm_ref, o_ref):
    scale = DH_M ** -0.5
    for h in range(H_M):
        sl = slice(h * DH_M, (h + 1) * DH_M)
        q = (q_ref[:, sl] * scale).astype(BF16)
        s = lax.dot_general(q, km_ref[:, sl], (((1,), (1,)), ((), ())), preferred_element_type=F32)
        m = jnp.max(s, axis=-1, keepdims=True)
        p = jnp.exp(s - m)
        p = p / jnp.sum(p, axis=-1, keepdims=True)
        o_ref[:, sl] = jnp.dot(p.astype(BF16), vm_ref[:, sl], preferred_element_type=F32).astype(BF16)


def memory_attention(proj3, km3, vm3, tm=512):
    B, S, _ = proj3.shape
    mspec = pl.BlockSpec((None, N_MEM, W_M), lambda b, i: (b, 0, 0))
    return pl.pallas_call(
        _mem_attention_kernel,
        out_shape=jax.ShapeDtypeStruct((B, S, W_M), BF16),
        grid=(B, S // tm),
        in_specs=[pl.BlockSpec((None, tm, W_M), lambda b, i: (b, i, OFF_QM // W_M)), mspec, mspec],
        out_specs=pl.BlockSpec((None, tm, W_M), lambda b, i: (b, i, 0)),
        compiler_params=_cparams(("parallel", "parallel")),
        name="memory_attention",
    )(proj3, km3, vm3)


def _group_sum(x, g_ref):
    hi, lo = _split2(x)
    return (jnp.dot(hi, g_ref[...], preferred_element_type=F32)
            + jnp.dot(lo, g_ref[...], preferred_element_type=F32))


def _rwkv_prep_kernel(*refs, has_vres):
    refs = list(refs)
    r_ref, k_ref, v_ref, l_ref = refs[:4]
    vf_ref = refs[4] if has_vres else None
    (mu_ref, mul_ref, par_ref, wl_ref, g_ref, tri_ref, same_ref,
     ro_ref, ko_ref, vo_ref, go_ref, gl_ref, at_ref, rt_ref, bt_ref, kt_ref, bt2_ref, kt2_ref,
     prev_ref, prevl_ref) = refs[4 + int(has_vres):]
    s_idx = pl.program_id(1)

    @pl.when(s_idx == 0)
    def _():
        prev_ref[...] = jnp.zeros_like(prev_ref)
        prevl_ref[...] = jnp.zeros_like(prevl_ref)

    def shift(x, carry_row, mu):
        tm = x.shape[0]
        rolled = pltpu.roll(x, shift=1, axis=0)
        first = lax.broadcasted_iota(jnp.int32, x.shape, 0) == 0
        prev = jnp.where(first, jnp.broadcast_to(carry_row, x.shape), rolled)
        return x + (prev - x) * mu, x[tm - 1:tm, :]

    r, r_last = shift(r_ref[...], prev_ref[0:1, :], mu_ref[0:1, :])
    k, k_last = shift(k_ref[...], prev_ref[1:2, :], mu_ref[1:2, :])
    v, v_last = shift(v_ref[...], prev_ref[2:3, :], mu_ref[2:3, :])
    lo, l_last = shift(l_ref[...], prevl_ref[0:1, :], mul_ref[...])
    prev_ref[0:1, :] = r_last
    prev_ref[1:2, :] = k_last
    prev_ref[2:3, :] = v_last
    prevl_ref[0:1, :] = l_last

    lane = lax.broadcasted_iota(jnp.int32, lo.shape, 1)
    lt = jnp.where(lane < W_LORA, jnp.tanh(lo),
                   jnp.where((lane >= W_LORA + A_LORA) & (lane < W_LORA + A_LORA + G_LORA),
                             jax.nn.sigmoid(lo), lo))
    up = jnp.dot(lt.astype(BF16), wl_ref[...], preferred_element_type=F32)
    w0, a0, v0 = par_ref[0:1, :], par_ref[1:2, :], par_ref[2:3, :]
    k_k, k_a = par_ref[3:4, :], par_ref[4:5, :]

    y = -(w0 + up[:, 0:W_B])
    softplus = jnp.maximum(y, 0.0) + jnp.log(1.0 + jnp.exp(-jnp.abs(y)))
    w_log = -softplus - 0.5
    lw = -jnp.exp(w_log)
    a = jax.nn.sigmoid(a0 + up[:, W_B:2 * W_B])
    go_ref[...] = up[:, 2 * W_B:3 * W_B]
    if has_vres:
        v = v + (vf_ref[...] - v) * jax.nn.sigmoid(v0 + up[:, 3 * W_B:4 * W_B])
    kk = k * k_k
    norm = jnp.sqrt(_group_sum(kk * kk, g_ref))
    kk = kk / jnp.maximum(norm, 1e-12)
    k = k * (1.0 + (a - 1.0) * k_a)
    ro_ref[...] = r
    ko_ref[...] = k
    vo_ref[...] = v

    p0, p1, p2 = _split3(lw)
    tri, same = tri_ref[...], same_ref[...]
    cs = (jnp.dot(tri, p0, preferred_element_type=F32) + jnp.dot(tri, p1, preferred_element_type=F32)
          + jnp.dot(tri, p2, preferred_element_type=F32))
    cl = (jnp.dot(same, p0, preferred_element_type=F32) + jnp.dot(same, p1, preferred_element_type=F32)
          + jnp.dot(same, p2, preferred_element_type=F32))
    e_neg = jnp.exp(-cs)
    to_end = jnp.exp(cl - cs)
    kka = kk * a
    at_ref[...] = (-kk * jnp.exp(cs - lw)).astype(BF16)
    rt_ref[...] = (r * jnp.exp(cs)).astype(BF16)
    bt_ref[...] = (kka * e_neg).astype(BF16)
    kt_ref[...] = (k * e_neg).astype(BF16)
    bt2_ref[...] = (kka * to_end).astype(BF16)
    kt2_ref[...] = (k * to_end).astype(BF16)
    gl_ref[...] = jnp.exp(cl)


def _chunk_matrices(tm):
    t = np.arange(tm)
    same = (t[:, None] // CHUNK) == (t[None, :] // CHUNK)
    tri = same & (t[None, :] <= t[:, None])
    return jnp.asarray(tri, BF16), jnp.asarray(same, BF16)


def rwkv_prep(proj3, v_first, mu_rkv, mu_lora, params, w_lora, gmat, tm=256):
    B, S, _ = proj3.shape
    has_vres = v_first is not None
    tok = lambda c: pl.BlockSpec((None, tm, W_B), lambda b, s: (b, s, c))
    full = lambda shp: pl.BlockSpec(shp, lambda b, s: (0,) * len(shp))
    in_specs = [tok(OFF_RKV // W_B), tok(OFF_RKV // W_B + 1), tok(OFF_RKV // W_B + 2),
                pl.BlockSpec((None, tm, LORA_W), lambda b, s: (b, s, OFF_LORA // LORA_W))]
    args = [proj3, proj3, proj3, proj3]
    if has_vres:
        in_specs.append(tok(0))
        args.append(v_first)
    in_specs += [full((3, W_B)), full((1, LORA_W)), full((8, W_B)), full((LORA_W, 4 * W_B)),
                 full((W_B, W_B)), full((tm, tm)), full((tm, tm))]
    args += [mu_rkv, mu_lora, params, w_lora, gmat, *_chunk_matrices(tm)]
    out32 = jax.ShapeDtypeStruct((B, S, W_B), F32)
    out16 = jax.ShapeDtypeStruct((B, S, W_B), BF16)
    return pl.pallas_call(
        functools.partial(_rwkv_prep_kernel, has_vres=has_vres),
        out_shape=(out32,) * 5 + (out16,) * 6,
        grid=(B, S // tm),
        in_specs=in_specs,
        out_specs=(tok(0),) * 11,
        scratch_shapes=[pltpu.VMEM((8, W_B), F32), pltpu.VMEM((8, LORA_W), F32)],
        compiler_params=_cparams(("parallel", "arbitrary")),
        name="rwkv_prep",
    )(*args)


SCAN_HPG = 8


def _rwkv_scan_kernel(at_ref, rt_ref, bt_ref, kt_ref, bt2_ref, kt2_ref, v_ref, gl_ref, o_ref,
                      state_ref):
    C, N, H = CHUNK, N_B, SCAN_HPG

    @pl.when(pl.program_id(2) == 0)
    def _():
        state_ref[...] = jnp.zeros_like(state_ref)

    row = lax.broadcasted_iota(jnp.int32, (C, C), 0)
    col = lax.broadcasted_iota(jnp.int32, (C, C), 1)
    incl = (col <= row).astype(F32)
    strict = (col < row).astype(F32)
    eye = (col == row).astype(F32)
    nt = (((1,), (1,)), ((), ()))
    dot = functools.partial(jnp.dot, preferred_element_type=F32)
    dot_nt = lambda x, y: lax.dot_general(x, y, nt, preferred_element_type=F32)
    hs = range(H)
    sl = [slice(h * N, (h + 1) * N) for h in hs]

    x = [jnp.concatenate([at_ref[:, sl[h]], rt_ref[:, sl[h]]], axis=0) for h in hs]
    v = [v_ref[:, sl[h]] for h in hs]
    vb = [v[h].astype(BF16) for h in hs]
    st = [state_ref[h] for h in hs]
    mb = [dot_nt(x[h], bt_ref[:, sl[h]]) for h in hs]
    mk = [dot_nt(x[h], kt_ref[:, sl[h]]) for h in hs]
    xh = [dot_nt(x[h], st[h].astype(BF16)) for h in hs]
    lab = [mb[h][:C] * strict for h in hs]
    mrb = [(mb[h][C:] * incl).astype(BF16) for h in hs]
    lm = [jnp.concatenate([mk[h][:C] * strict, mk[h][C:] * incl], axis=0).astype(BF16) for h in hs]
    lv = [dot(lm[h], vb[h]) for h in hs]
    t = [eye + lab[h] for h in hs]
    p = lab
    for _ in range(5):
        pb = [p[h].astype(BF16) for h in hs]
        p = [dot(pb[h], pb[h]) for h in hs]
        t = [t[h] + dot(t[h].astype(BF16), p[h].astype(BF16)) for h in hs]
    u = [dot(t[h].astype(BF16), (xh[h][:C] + lv[h][:C]).astype(BF16)) for h in hs]
    o = [xh[h][C:] + dot(mrb[h], u[h].astype(BF16)) + lv[h][C:] for h in hs]
    for h in hs:
        o_ref[:, sl[h]] = o[h]
    uv_t = []
    for h in hs:
        uv = jnp.concatenate([u[h], v[h]], axis=0)
        uv_t.append(jnp.concatenate([uv, jnp.zeros_like(uv)], axis=1).T[:N].astype(BF16))
    for h in hs:
        y2 = jnp.concatenate([bt2_ref[:, sl[h]], kt2_ref[:, sl[h]]], axis=0)
        state_ref[h] = st[h] * gl_ref[0:1, sl[h]] + dot(uv_t[h], y2)


def rwkv_scan(at, rt, bt, kt, bt2, kt2, v, gl):
    B, S, _ = v.shape
    wb = SCAN_HPG * N_B
    spec = pl.BlockSpec((None, CHUNK, wb), lambda b, g, c: (b, c, g))
    return pl.pallas_call(
        _rwkv_scan_kernel,
        out_shape=jax.ShapeDtypeStruct((B, S, W_B), F32),
        grid=(B, H_B // SCAN_HPG, S // CHUNK),
        in_specs=[spec] * 8,
        out_specs=spec,
        scratch_shapes=[pltpu.VMEM((SCAN_HPG, N_B, N_B), F32)],
        compiler_params=_cparams(("parallel", "parallel", "arbitrary")),
        name="rwkv_scan",
    )(at, rt, bt, kt, bt2, kt2, v, gl)


def _rwkv_post_kernel(o_ref, r_ref, k_ref, v_ref, g_ref, par_ref, gm_ref, out_ref):
    o = o_ref[...]
    mu = _group_sum(o, gm_ref) * (1.0 / N_B)
    oc = o - mu
    var = _group_sum(oc * oc, gm_ref) * (1.0 / N_B)
    on = oc * lax.rsqrt(var + LNX_EPS) * par_ref[0:1, :] + par_ref[1:2, :]
    bonus = _group_sum(r_ref[...] * k_ref[...] * par_ref[2:3, :], gm_ref) * v_ref[...]
    out_ref[...] = ((on + bonus) * g_ref[...]).astype(BF16)


def rwkv_post(o, r, k, v, g, params, gmat, tm=256):
    M = o.shape[0]
    row = pl.BlockSpec((tm, W_B), lambda i: (i, 0))
    return pl.pallas_call(
        _rwkv_post_kernel,
        out_shape=jax.ShapeDtypeStruct((M, W_B), BF16),
        grid=(M // tm,),
        in_specs=[row] * 5 + [pl.BlockSpec((8, W_B), lambda i: (0, 0)),
                              pl.BlockSpec((W_B, W_B), lambda i: (0, 0))],
        out_specs=row,
        compiler_params=_cparams(("parallel",)),
        name="rwkv_post",
    )(o, r, k, v, g, params, gmat)


def _merge_kernel(oa_ref, ob_ref, oc_ref, wa_ref, wb_ref, wc_ref, g0_ref, g1_ref, g2_ref, out_ref):
    ya = jnp.dot(oa_ref[...], wa_ref[...], preferred_element_type=F32)
    yb = jnp.dot(ob_ref[...], wb_ref[...], preferred_element_type=F32)
    yc = jnp.dot(oc_ref[...], wc_ref[...], preferred_element_type=F32)
    out = (jax.nn.sigmoid(g0_ref[...]) * ya + jax.nn.sigmoid(g1_ref[...]) * yb
           + jax.nn.sigmoid(g2_ref[...]) * yc)
    out_ref[...] = out.astype(BF16)


def gated_merge(o_a, o_b, o_c, w_pa, w_pb, w_pc, proj2, tm=512, tn=1024):
    M = o_a.shape[0]
    D = D_MODEL
    nj = D // tn
    act = lambda w: pl.BlockSpec((tm, w), lambda i, j: (i, 0))
    wgt = lambda w: pl.BlockSpec((w, tn), lambda i, j: (0, j))
    gate = lambda br: pl.BlockSpec((tm, tn), lambda i, j: (i, OFF_GATE // tn + br * nj + j))
    return pl.pallas_call(
        _merge_kernel,
        out_shape=jax.ShapeDtypeStruct((M, D), BF16),
        grid=(M // tm, nj),
        in_specs=[act(W_A), act(W_B), act(W_M), wgt(W_A), wgt(W_B), wgt(W_M),
                  gate(0), gate(1), gate(2)],
        out_specs=pl.BlockSpec((tm, tn), lambda i, j: (i, j)),
        compiler_params=_cparams(("parallel", "arbitrary")),
        name="gated_merge",
    )(o_a, o_b, o_c, w_pa, w_pb, w_pc, proj2, proj2, proj2)


def _first_index_of_max(x, idx, n):
    m = jnp.max(x, axis=0, keepdims=True)
    first = jnp.min(jnp.where(x == m, idx, n), axis=0, keepdims=True)
    return m, idx == first


def _router_kernel(h_ref, wr_ref, bias_ref, gate_ref):
    E, G = N_EXPERTS, N_GROUPS
    per = E // G
    logits = lax.dot_general(wr_ref[...], h_ref[...], (((1,), (1,)), ((), ())),
                             preferred_element_type=F32)
    tm = logits.shape[1]
    scores = jax.nn.sigmoid(logits)
    biased = scores + jnp.concatenate([bias_ref[...]] * (tm // 128), axis=1)
    idx8 = lax.broadcasted_iota(jnp.int32, (per, tm), 0)
    gscores = []
    for g in range(G):
        xg = biased[g * per:(g + 1) * per, :]
        m1, hit = _first_index_of_max(xg, idx8, per)
        m2 = jnp.max(jnp.where(hit, -jnp.inf, xg), axis=0, keepdims=True)
        gscores.append(m1 + m2)
    gs = jnp.concatenate(gscores, axis=0)
    gidx = lax.broadcasted_iota(jnp.int32, (G, tm), 0)
    gsel = jnp.zeros((G, tm), F32)
    for _ in range(TOPK_GROUPS):
        _, hit = _first_index_of_max(gs, gidx, G)
        gsel = jnp.where(hit, 1.0, gsel)
        gs = jnp.where(hit, -jnp.inf, gs)
    emask = jnp.concatenate(
        [jnp.broadcast_to(gsel[g:g + 1, :], (per, tm)) for g in range(G)], axis=0)
    cand = jnp.where(emask > 0.0, biased, -jnp.inf)
    eidx = lax.broadcasted_iota(jnp.int32, (E, tm), 0)
    picked = jnp.zeros((E, tm), F32)
    for _ in range(TOP_K):
        _, hit = _first_index_of_max(cand, eidx, E)
        picked = jnp.where(hit, scores, picked)
        cand = jnp.where(hit, -jnp.inf, cand)
    gate = picked / jnp.sum(picked, axis=0, keepdims=True) * ROUTED_SCALE
    shared = (eidx == 0).astype(F32)
    gate_ref[...] = jnp.concatenate([gate, shared], axis=0).T


def moe_router(hb, w_router_t, bias_b, tm=256):
    M, D = hb.shape
    return pl.pallas_call(
        _router_kernel,
        out_shape=jax.ShapeDtypeStruct((M, 2 * N_EXPERTS), F32),
        grid=(M // tm,),
        in_specs=[pl.BlockSpec((tm, D), lambda i: (i, 0)),
                  pl.BlockSpec((N_EXPERTS, D), lambda i: (0, 0)),
                  pl.BlockSpec((N_EXPERTS, 128), lambda i: (0, 0))],
        out_specs=pl.BlockSpec((tm, 2 * N_EXPERTS), lambda i: (i, 0)),
        compiler_params=_cparams(("parallel",)),
        name="moe_router",
    )(hb, w_router_t, bias_b)


MOE_F = 6656


def _moe_act_kernel(x_ref, gate_ref, e_ref, w1_ref, w3_ref, o_ref):
    x = x_ref[...]
    h1 = jnp.dot(x, w1_ref[...], preferred_element_type=F32)
    h3 = jnp.dot(x, w3_ref[...], preferred_element_type=F32)
    g0, g1, g2 = _split3(gate_ref[...])
    e = e_ref[...]
    gexp = (jnp.dot(g0, e, preferred_element_type=F32) + jnp.dot(g1, e, preferred_element_type=F32)
            + jnp.dot(g2, e, preferred_element_type=F32))
    o_ref[...] = (h1 * jax.nn.sigmoid(h1) * h3 * gexp).astype(BF16)


def moe_hidden(hb, gate, expand, w1c, w3c, tm=1024, tn=512):
    M, D = hb.shape
    wspec = pl.BlockSpec((D, tn), lambda i, j: (0, j))
    return pl.pallas_call(
        _moe_act_kernel,
        out_shape=jax.ShapeDtypeStruct((M, MOE_F), BF16),
        grid=(M // tm, MOE_F // tn),
        in_specs=[pl.BlockSpec((tm, D), lambda i, j: (i, 0)),
                  pl.BlockSpec((tm, 2 * N_EXPERTS), lambda i, j: (i, 0)),
                  pl.BlockSpec((2 * N_EXPERTS, tn), lambda i, j: (0, j)),
                  wspec, wspec],
        out_specs=pl.BlockSpec((tm, tn), lambda i, j: (i, j)),
        compiler_params=_cparams(("parallel", "arbitrary")),
        name="moe_hidden",
    )(hb, gate, expand, w1c, w3c)


def _regroup_w_in(w, has_vres):
    D = w.shape[0]
    o = np.cumsum((0, W_A, R_KV, H_IDX * D_IDX, D_IDX, H_IDX, 3 * W_B, W_LORA, A_LORA, G_LORA, W_M,
                   3 * D_MODEL)).tolist()
    q_a, c_kv, q_idx = w[:, o[0]:o[1]], w[:, o[1]:o[2]], w[:, o[2]:o[3]]
    idx = w[:, o[3]:o[5]]
    rkv, lora = w[:, o[5]:o[6]], w[:, o[6]:o[9]]
    q_m, gates = w[:, o[9]:o[10]], w[:, o[10]:o[11]]
    zeros = lambda n: jnp.zeros((D, n), w.dtype)
    vres = w[:, o[11]:o[11] + V_LORA] if has_vres else zeros(V_LORA)
    n_lora = W_LORA + A_LORA + G_LORA + V_LORA
    parts = [gates, q_a, q_idx, rkv, q_m, c_kv, lora, vres, zeros(LORA_W - n_lora),
             idx, zeros(IDX_W - D_IDX - H_IDX)]
    used = sum(p.shape[1] for p in parts)
    parts.append(zeros(N_PROJ - used))
    return jnp.concatenate(parts, axis=1).astype(BF16)


def _lora_up_matrix(w_up, a_up, g_up, v_up):
    m = jnp.zeros((LORA_W, 4 * W_B), F32)
    m = m.at[0:W_LORA, 0:W_B].set(w_up)
    m = m.at[W_LORA:W_LORA + A_LORA, W_B:2 * W_B].set(a_up)
    m = m.at[W_LORA + A_LORA:W_LORA + A_LORA + G_LORA, 2 * W_B:3 * W_B].set(g_up)
    if v_up is not None:
        r0 = W_LORA + A_LORA + G_LORA
        m = m.at[r0:r0 + V_LORA, 3 * W_B:4 * W_B].set(v_up)
    return m.astype(BF16)


def _rows(vectors, n_rows, width):
    m = jnp.zeros((n_rows, width), F32)
    for i, vec in enumerate(vectors):
        m = m.at[i, :vec.shape[0]].set(vec)
    return m


def _expert_in_weights(w_e, w_s):
    D = w_e.shape[1]
    routed = jnp.transpose(w_e, (1, 0, 2)).reshape(D, N_EXPERTS * F_EXPERT)
    pad = jnp.zeros((D, MOE_F - N_EXPERTS * F_EXPERT - w_s.shape[1]), w_e.dtype)
    return jnp.concatenate([routed, w_s, pad], axis=1).astype(BF16)


def _expert_out_weights(w_e2, w_s2):
    D = w_e2.shape[2]
    routed = w_e2.reshape(N_EXPERTS * F_EXPERT, D)
    pad = jnp.zeros((MOE_F - N_EXPERTS * F_EXPERT - w_s2.shape[0], D), w_e2.dtype)
    return jnp.concatenate([routed, w_s2, pad], axis=0).astype(BF16)


def kernel(x, mem, ln0_g, ln0_b, w_in_first, w_in_rest, ckv_g, w_uk, w_uv, mu_rwkv, mu_vres, rw_w0, rw_w_up, rw_a0, rw_a_up, rw_g_up, rw_v0, rw_v_up, rw_k_k, rw_k_a, rw_r_k, rw_lnx_g, rw_lnx_b, w_mk, w_mv, w_pa, w_pb, w_pc, w_o, ln1_g, ln1_b, w_router, router_bias, w_e1, w_e3, w_e2, w_s1, w_s3, w_s2, ln2_g, ln2_b):
    B, S, D = x.shape
    T = B * S
    gmat = jnp.kron(jnp.eye(H_B, dtype=F32), jnp.ones((N_B, N_B), F32)).astype(BF16)
    unit = np.arange(MOE_F)
    owner = np.where(unit < (N_EXPERTS + 1) * F_EXPERT, unit // F_EXPERT, -1)
    expand = jnp.asarray(owner[None, :] == np.arange(2 * N_EXPERTS)[:, None], BF16)
    mem_b = mem.reshape(B * N_MEM, D).astype(BF16)

    h, hb = layer_norm(x.reshape(T, D), ln0_g, ln0_b)
    v_first = None
    for i in range(DEPTH):
        has_vres = i > 0
        w_in = _regroup_w_in(w_in_first if i == 0 else w_in_rest[i - 1], has_vres)
        proj2 = matmul(hb, w_in, F32, 1024, 1024)
        proj3 = proj2.reshape(B, S, N_PROJ)

        k2, v2 = kv_project(proj2, ckv_g[i], w_uk[i].astype(BF16), w_uv[i].astype(BF16))
        bias5 = indexer_bias(proj3)
        o_a = sparse_attention(proj3, k2.reshape(B, S, W_A), v2.reshape(B, S, W_A), bias5)

        mu = mu_rwkv[i]
        mu_rkv = mu[:3 * W_B].reshape(3, W_B)
        mu_l = [mu[3 * W_B:]] + ([mu_vres[i - 1]] if has_vres else [])
        mu_lora = _rows([jnp.concatenate(mu_l)], 1, LORA_W)
        prep_par = _rows([rw_w0[i], rw_a0[i], rw_v0[i - 1] if has_vres else jnp.zeros((W_B,), F32),
                          rw_k_k[i], rw_k_a[i]], 8, W_B)
        w_lora = _lora_up_matrix(rw_w_up[i], rw_a_up[i], rw_g_up[i],
                                 rw_v_up[i - 1] if has_vres else None)
        r_b, k_b, v_b, g_b, gl_b, *scan_ops = rwkv_prep(proj3, v_first, mu_rkv, mu_lora, prep_par,
                                                        w_lora, gmat)
        if i == 0:
            v_first = v_b
        o_scan = rwkv_scan(*scan_ops, v_b, gl_b)
        post_par = _rows([rw_lnx_g[i], rw_lnx_b[i], rw_r_k[i].reshape(W_B)], 8, W_B)
        flat = lambda z: z.reshape(T, W_B)
        o_b = rwkv_post(flat(o_scan), flat(r_b), flat(k_b), flat(v_b), flat(g_b), post_par, gmat)

        km = matmul(mem_b, w_mk[i].astype(BF16), BF16, 512, 512).reshape(B, N_MEM, W_M)
        vm = matmul(mem_b, w_mv[i].astype(BF16), BF16, 512, 512).reshape(B, N_MEM, W_M)
        o_c = memory_attention(proj3, km, vm)

        merged = gated_merge(o_a.reshape(T, W_A), o_b, o_c.reshape(T, W_M), w_pa[i].astype(BF16),
                             w_pb[i].astype(BF16), w_pc[i].astype(BF16), proj2)
        h, hb = matmul_residual_ln(merged, w_o[i].astype(BF16), h, ln1_g[i], ln1_b[i])

        bias_b = jnp.broadcast_to(router_bias[i].reshape(N_EXPERTS, 1), (N_EXPERTS, 128))
        gate = moe_router(hb, w_router[i].T.astype(BF16), bias_b)
        hidden = moe_hidden(hb, gate, expand, _expert_in_weights(w_e1[i], w_s1[i]),
                            _expert_in_weights(w_e3[i], w_s3[i]))
        h, hb = matmul_residual_ln(hidden, _expert_out_weights(w_e2[i], w_s2[i]), h,
                                   ln2_g[i], ln2_b[i])
    return h.reshape(B, S, D)
```

```python
import functools

import jax
import jax.numpy as jnp
import numpy as np
from jax import lax
from jax.experimental import pallas as pl
from jax.experimental.pallas import tpu as pltpu

F32 = jnp.float32
BF16 = jnp.bfloat16

D_MODEL = 4096
DEPTH = 4
CHUNK = 64
H_A, DH_A = 16, 128
W_A = H_A * DH_A
R_KV = 256
H_IDX, D_IDX = 32, 64
TOPK_MAX = 256
H_B, N_B = 16, 64
W_B = H_B * N_B
W_LORA, A_LORA, G_LORA, V_LORA = 64, 64, 160, 32
LNX_EPS = 64e-5
N_MEM = 256
H_M, DH_M = 4, 256
W_M = H_M * DH_M
N_EXPERTS, N_GROUPS, TOPK_GROUPS, TOP_K = 64, 8, 4, 8
F_EXPERT = 96
ROUTED_SCALE = 2.5
ALPHA = (2 * DEPTH) ** 0.25
LN_EPS = 1e-5

OFF_GATE = 0
OFF_QA = 3 * D_MODEL
OFF_QIDX = OFF_QA + W_A
OFF_RKV = OFF_QIDX + H_IDX * D_IDX
OFF_QM = OFF_RKV + 3 * W_B
OFF_CKV = OFF_QM + W_M
OFF_LORA = OFF_CKV + R_KV
LORA_W = 384
OFF_IDX = OFF_LORA + LORA_W
IDX_W = 128
N_PROJ = 21504

NEG_BIAS = -1e30
VMEM_LIMIT = 56 * 1024 * 1024


def _cparams(sem):
    return pltpu.CompilerParams(dimension_semantics=sem, vmem_limit_bytes=VMEM_LIMIT)


def _split2(x):
    hi = x.astype(BF16)
    lo = (x - hi.astype(F32)).astype(BF16)
    return hi, lo


def _split3(x):
    hi = x.astype(BF16)
    r1 = x - hi.astype(F32)
    mid = r1.astype(BF16)
    lo = (r1 - mid.astype(F32)).astype(BF16)
    return hi, mid, lo


def _mm_kernel(x_ref, w_ref, o_ref):
    o_ref[...] = jnp.dot(x_ref[...], w_ref[...], preferred_element_type=F32).astype(o_ref.dtype)


def matmul(x, w, out_dtype, tm, tn):
    M, K = x.shape
    N = w.shape[1]
    tm, tn = min(tm, M), min(tn, N)
    assert M % tm == 0 and N % tn == 0
    return pl.pallas_call(
        _mm_kernel,
        out_shape=jax.ShapeDtypeStruct((M, N), out_dtype),
        grid=(M // tm, N // tn),
        in_specs=[pl.BlockSpec((tm, K), lambda i, j: (i, 0)),
                  pl.BlockSpec((K, tn), lambda i, j: (0, j))],
        out_specs=pl.BlockSpec((tm, tn), lambda i, j: (i, j)),
        compiler_params=_cparams(("parallel", "arbitrary")),
        name="matmul",
    )(x, w)


def _ln_rows(y, g, b):
    mu = jnp.mean(y, axis=-1, keepdims=True)
    yc = y - mu
    var = jnp.mean(yc * yc, axis=-1, keepdims=True)
    return yc * lax.rsqrt(var + LN_EPS) * g + b


def _ln_kernel(x_ref, g_ref, b_ref, o_ref, ob_ref):
    y = _ln_rows(x_ref[...], g_ref[...], b_ref[...])
    o_ref[...] = y
    ob_ref[...] = y.astype(BF16)


def _res_ln_kernel(h_ref, f_ref, g_ref, b_ref, o_ref, ob_ref):
    y = _ln_rows(ALPHA * h_ref[...] + f_ref[...], g_ref[...], b_ref[...])
    o_ref[...] = y
    ob_ref[...] = y.astype(BF16)


def layer_norm(x, g, b, tm=256):
    M, D = x.shape
    row = pl.BlockSpec((tm, D), lambda i: (i, 0))
    vec = pl.BlockSpec((1, D), lambda i: (0, 0))
    return pl.pallas_call(
        _ln_kernel,
        out_shape=(jax.ShapeDtypeStruct((M, D), F32), jax.ShapeDtypeStruct((M, D), BF16)),
        grid=(M // tm,),
        in_specs=[row, vec, vec],
        out_specs=(row, row),
        compiler_params=_cparams(("parallel",)),
        name="layer_norm",
    )(x, g.reshape(1, D), b.reshape(1, D))


def residual_layer_norm(h, f, g, b, tm=256):
    M, D = h.shape
    row = pl.BlockSpec((tm, D), lambda i: (i, 0))
    vec = pl.BlockSpec((1, D), lambda i: (0, 0))
    return pl.pallas_call(
        _res_ln_kernel,
        out_shape=(jax.ShapeDtypeStruct((M, D), F32), jax.ShapeDtypeStruct((M, D), BF16)),
        grid=(M // tm,),
        in_specs=[row, row, vec, vec],
        out_specs=(row, row),
        compiler_params=_cparams(("parallel",)),
        name="residual_layer_norm",
    )(h, f, g.reshape(1, D), b.reshape(1, D))


def _mm_res_ln_kernel(x_ref, w_ref, h_ref, g_ref, b_ref, o_ref, ob_ref, *, nj, tn):
    j = pl.program_id(1)
    y = jnp.dot(x_ref[...], w_ref[...], preferred_element_type=F32)
    for jj in range(nj):
        @pl.when(j == jj)
        def _(jj=jj):
            o_ref[:, jj * tn:(jj + 1) * tn] = y

    @pl.when(j == nj - 1)
    def _():
        D = nj * tn
        s1 = None
        for jj in range(nj):
            sl = slice(jj * tn, (jj + 1) * tn)
            z = ALPHA * h_ref[:, sl] + o_ref[:, sl]
            o_ref[:, sl] = z
            p = jnp.sum(z, axis=-1, keepdims=True)
            s1 = p if s1 is None else s1 + p
        mu = s1 * (1.0 / D)
        s2 = None
        for jj in range(nj):
            zc = o_ref[:, jj * tn:(jj + 1) * tn] - mu
            p = jnp.sum(zc * zc, axis=-1, keepdims=True)
            s2 = p if s2 is None else s2 + p
        rstd = lax.rsqrt(s2 * (1.0 / D) + LN_EPS)
        for jj in range(nj):
            sl = slice(jj * tn, (jj + 1) * tn)
            z = (o_ref[:, sl] - mu) * rstd * g_ref[:, sl] + b_ref[:, sl]
            o_ref[:, sl] = z
            ob_ref[:, sl] = z.astype(BF16)


def matmul_residual_ln(x, w, h, g, b, tm=512, tn=512):
    M, K = x.shape
    D = w.shape[1]
    nj = D // tn
    row = pl.BlockSpec((tm, D), lambda i, j: (i, 0), pipeline_mode=pl.Buffered(1))
    vec = pl.BlockSpec((1, D), lambda i, j: (0, 0))
    return pl.pallas_call(
        functools.partial(_mm_res_ln_kernel, nj=nj, tn=tn),
        out_shape=(jax.ShapeDtypeStruct((M, D), F32), jax.ShapeDtypeStruct((M, D), BF16)),
        grid=(M // tm, nj),
        in_specs=[pl.BlockSpec((tm, K), lambda i, j: (i, 0), pipeline_mode=pl.Buffered(1)),
                  pl.BlockSpec((K, tn), lambda i, j: (0, j)),
                  row, vec, vec],
        out_specs=(row, row),
        compiler_params=_cparams(("parallel", "arbitrary")),
        name="matmul_residual_ln",
    )(x, w, h, g.reshape(1, D), b.reshape(1, D))


def _kv_kernel(c_ref, g_ref, wk_ref, wv_ref, k_ref, v_ref):
    c = c_ref[...]
    ms = jnp.mean(c * c, axis=-1, keepdims=True)
    cn = (c * lax.rsqrt(ms + LN_EPS) * g_ref[...]).astype(BF16)
    k_ref[...] = jnp.dot(cn, wk_ref[...], preferred_element_type=F32).astype(BF16)
    v_ref[...] = jnp.dot(cn, wv_ref[...], preferred_element_type=F32).astype(BF16)


def kv_project(proj2, ckv_g, w_uk, w_uv, tm=512):
    M = proj2.shape[0]
    out = jax.ShapeDtypeStruct((M, W_A), BF16)
    wspec = pl.BlockSpec((R_KV, W_A), lambda i: (0, 0))
    ospec = pl.BlockSpec((tm, W_A), lambda i: (i, 0))
    return pl.pallas_call(
        _kv_kernel,
        out_shape=(out, out),
        grid=(M // tm,),
        in_specs=[pl.BlockSpec((tm, R_KV), lambda i: (i, OFF_CKV // R_KV)),
                  pl.BlockSpec((1, R_KV), lambda i: (0, 0)), wspec, wspec],
        out_specs=(ospec, ospec),
        compiler_params=_cparams(("parallel",)),
        name="kv_project",
    )(proj2, ckv_g.reshape(1, R_KV), w_uk, w_uv)


IDX_TQ = 128
IDX_TK = 256
KEY_NEG_INF = -2139095041
INT_MIN = -2147483648


def _indexer_kernel(q_ref, iq_ref, ik_ref, bias_ref, qt_ref, wt_ref, key_ref, *, n_kt, ksel):
    tq, tk = IDX_TQ, IDX_TK
    i = pl.program_id(1)
    qt_ref[...] = (q_ref[...].T * (D_IDX ** -0.5)).astype(BF16)
    wt_ref[...] = iq_ref[...].T * (H_IDX ** -0.5)

    n_t = ((i + 1) * tq + tk - 1) // tk
    q_chunk = (i * tq + lax.broadcasted_iota(jnp.int32, (tk, tq), 1)) // CHUNK
    row_iota = lax.broadcasted_iota(jnp.int32, (tk, tq), 0)

    def score_tile(j, carry):
        kt = ik_ref[pl.ds(pl.multiple_of(j * tk, tk), tk), :][:, :D_IDX].astype(BF16)
        acc = jnp.zeros((tk, tq), F32)
        for h2 in range(H_IDX // 2):
            w2 = jnp.concatenate([qt_ref[(2 * h2 + t) * D_IDX:(2 * h2 + t + 1) * D_IDX, :]
                                  for t in range(2)], axis=1)
            z = jnp.dot(kt, w2, preferred_element_type=F32)
            for t in range(2):
                h = 2 * h2 + t
                acc = acc + wt_ref[D_IDX + h:D_IDX + h + 1, :] * jnp.maximum(
                    z[:, t * tq:(t + 1) * tq], 0.0)
        admissible = ((j * tk + row_iota) // CHUNK) <= q_chunk
        score = jnp.where(admissible, acc, -jnp.inf)
        bits = lax.bitcast_convert_type(score, jnp.int32)
        key_ref[j] = bits ^ ((bits >> 31) & 0x7FFFFFFF)
        return carry

    lax.fori_loop(0, n_t, score_tile, 0)

    def bit_step(it, thr):
        cand = thr + lax.shift_left(jnp.int32(1), 31 - it)

        def count_tile(j, cnts):
            c = (key_ref[j] >= cand).astype(jnp.int32)
            cnts = list(cnts)
            for r in range(tk // 8):
                cnts[r % len(cnts)] = cnts[r % len(cnts)] + c[r * 8:(r + 1) * 8, :]
            return tuple(cnts)

        cnts = lax.fori_loop(0, n_t, count_tile, (jnp.zeros((8, tq), jnp.int32),) * 4)
        cnt = (cnts[0] + cnts[1]) + (cnts[2] + cnts[3])
        total = jnp.sum(cnt.astype(F32), axis=0, keepdims=True)
        return jnp.where(total >= float(ksel), cand, thr)

    thr = lax.fori_loop(0, 32, bit_step, jnp.full((1, tq), INT_MIN, jnp.int32))

    def bias_tile(j, carry):
        key = key_ref[j]
        sel = (key >= thr) & (key > KEY_NEG_INF)
        bias_ref[j] = jnp.where(sel, 0.0, NEG_BIAS).T.astype(BF16)
        return carry

    lax.fori_loop(0, n_t, bias_tile, 0)

    def fill_tile(j, carry):
        bias_ref[j] = jnp.full((tq, tk), NEG_BIAS, BF16)
        return carry

    lax.fori_loop(n_t, n_kt, fill_tile, 0)


def indexer_bias(proj3):
    B, S, _ = proj3.shape
    tq, tk = IDX_TQ, IDX_TK
    n_qt, n_kt = S // tq, S // tk
    ksel = min(TOPK_MAX, S // 4)
    return pl.pallas_call(
        functools.partial(_indexer_kernel, n_kt=n_kt, ksel=ksel),
        out_shape=jax.ShapeDtypeStruct((B, n_qt, n_kt, tq, tk), BF16),
        grid=(B, n_qt),
        in_specs=[pl.BlockSpec((None, tq, H_IDX * D_IDX), lambda b, i: (b, i, OFF_QIDX // (H_IDX * D_IDX))),
                  pl.BlockSpec((None, tq, IDX_W), lambda b, i: (b, i, OFF_IDX // IDX_W)),
                  pl.BlockSpec((None, S, IDX_W), lambda b, i: (b, 0, OFF_IDX // IDX_W))],
        out_specs=pl.BlockSpec((None, None, n_kt, tq, tk), lambda b, i: (b, i, 0, 0, 0)),
        scratch_shapes=[pltpu.VMEM((H_IDX * D_IDX, tq), BF16),
                        pltpu.VMEM((IDX_W, tq), F32),
                        pltpu.VMEM((n_kt, tk, tq), jnp.int32)],
        compiler_params=_cparams(("parallel", "parallel")),
        name="indexer_bias",
    )(proj3, proj3, proj3)


ATT_TQ = 256
ATT_HP = 2
ATT_TK = 512


ATT_RG = 16


def _attention_kernel(q_ref, k_ref, v_ref, bias_ref, o_ref, qs_ref, s_ref, p_ref, m_ref, a_ref,
                      acc_ref):
    tq, tk, hp = ATT_TQ, ATT_TK, ATT_HP
    nq = tq // IDX_TQ
    nb = tk // IDX_TK
    i = pl.program_id(2)
    scale = DH_A ** -0.5 * np.log2(np.e)
    nt = (((1,), (1,)), ((), ()))
    streams = [(hh, qq) for hh in range(hp) for qq in range(nq)]
    cols = [slice(hh * DH_A, (hh + 1) * DH_A) for hh in range(hp)]
    rows = [slice(qq * IDX_TQ, (qq + 1) * IDX_TQ) for qq in range(nq)]
    for c, (hh, qq) in enumerate(streams):
        qs_ref[c] = (q_ref[rows[qq], cols[hh]] * scale).astype(BF16)
    m_ref[...] = jnp.full(m_ref.shape, NEG_BIAS, F32)
    acc_ref[...] = jnp.zeros_like(acc_ref)
    p_ref[1] = jnp.zeros(p_ref.shape[1:], BF16)
    a_ref[1] = jnp.zeros(a_ref.shape[1:], F32)
    n_kv = k_ref.shape[0] // tk
    n_t = ((i + 1) * tq + tk - 1) // tk
    ones = jnp.ones((tk, DH_A), BF16)

    def logits(j, slot):
        row0 = pl.multiple_of(j * tk, tk)
        for c, (hh, qq) in enumerate(streams):
            s_ref[slot, c] = lax.dot_general(qs_ref[c], k_ref[pl.ds(row0, tk), cols[hh]], nt,
                                             preferred_element_type=F32)

    def weighted_values(j, slot):
        row0 = pl.multiple_of(j * tk, tk)
        return [jnp.dot(p_ref[slot, c],
                        jnp.concatenate([v_ref[pl.ds(row0, tk), cols[hh]], ones], axis=1),
                        preferred_element_type=F32) for c, (hh, qq) in enumerate(streams)]

    def accumulate(pv, slot):
        for c in range(len(streams)):
            acc_ref[c] = jnp.concatenate([a_ref[slot, c]] * 2, axis=1) * acc_ref[c] + pv[c]

    def stage(j, slot):
        other = 1 - slot
        logits(jnp.minimum(j + 1, n_kv - 1), other)
        pv = weighted_values(jnp.maximum(j - 1, 0), other)
        for c, (hh, qq) in enumerate(streams):
            for r in range(IDX_TQ // ATT_RG):
                rs = slice(r * ATT_RG, (r + 1) * ATT_RG)
                bias = jnp.concatenate([bias_ref[qq, nb * j + t, rs, :] for t in range(nb)], axis=1)
                sc = s_ref[slot, c, rs, :] + bias.astype(F32)
                m_old = m_ref[c, rs, :]
                m_new = jnp.maximum(m_old, jnp.max(sc, axis=-1, keepdims=True))
                a_ref[slot, c, rs, :] = jnp.exp2(m_old - m_new)
                m_ref[c, rs, :] = m_new
                p_ref[slot, c, rs, :] = jnp.exp2(
                    sc - jnp.concatenate([m_new] * (tk // 128), axis=1)).astype(BF16)
        accumulate(pv, other)

    logits(0, 0)

    def body(jj, carry):
        stage(2 * jj, 0)
        stage(2 * jj + 1, 1)
        return carry

    n_pairs = (n_t + 1) // 2
    lax.fori_loop(0, n_pairs, body, 0)
    accumulate(weighted_values(2 * n_pairs - 1, 1), 1)
    for c, (hh, qq) in enumerate(streams):
        o_ref[rows[qq], cols[hh]] = (acc_ref[c, :, :DH_A] / acc_ref[c, :, DH_A:]).astype(BF16)


def sparse_attention(proj3, k3, v3, bias5):
    B, S, _ = proj3.shape
    tq, hp = ATT_TQ, ATT_HP
    wb = hp * DH_A
    n_kt = S // IDX_TK
    n_streams = hp * (tq // IDX_TQ)
    assert S % (2 * ATT_TK) == 0
    return pl.pallas_call(
        _attention_kernel,
        out_shape=jax.ShapeDtypeStruct((B, S, W_A), BF16),
        grid=(B, H_A // hp, S // tq),
        in_specs=[pl.BlockSpec((None, tq, wb), lambda b, h, i: (b, i, OFF_QA // wb + h)),
                  pl.BlockSpec((None, S, wb), lambda b, h, i: (b, 0, h)),
                  pl.BlockSpec((None, S, wb), lambda b, h, i: (b, 0, h)),
                  pl.BlockSpec((None, tq // IDX_TQ, n_kt, IDX_TQ, IDX_TK),
                               lambda b, h, i: (b, i, 0, 0, 0))],
        out_specs=pl.BlockSpec((None, tq, wb), lambda b, h, i: (b, i, h)),
        scratch_shapes=[pltpu.VMEM((n_streams, IDX_TQ, DH_A), BF16),
                        pltpu.VMEM((2, n_streams, IDX_TQ, ATT_TK), F32),
                        pltpu.VMEM((2, n_streams, IDX_TQ, ATT_TK), BF16),
                        pltpu.VMEM((n_streams, IDX_TQ, 128), F32),
                        pltpu.VMEM((2, n_streams, IDX_TQ, 128), F32),
                        pltpu.VMEM((n_streams, IDX_TQ, 2 * DH_A), F32)],
        compiler_params=_cparams(("parallel", "parallel", "parallel")),
        name="sparse_attention",
    )(proj3, k3, v3, bias5)


def _mem_attention_kernel(q_ref, km_ref, vm_ref, o_ref):
    scale = DH_M ** -0.5
    for h in range(H_M):
        sl = slice(h * DH_M, (h + 1) * DH_M)
        q = (q_ref[:, sl] * scale).astype(BF16)
        s = lax.dot_general(q, km_ref[:, sl], (((1,), (1,)), ((), ())), preferred_element_type=F32)
        m = jnp.max(s, axis=-1, keepdims=True)
        p = jnp.exp(s - m)
        p = p / jnp.sum(p, axis=-1, keepdims=True)
        o_ref[:, sl] = jnp.dot(p.astype(BF16), vm_ref[:, sl], preferred_element_type=F32).astype(BF16)


def memory_attention(proj3, km3, vm3, tm=512):
    B, S, _ = proj3.shape
    mspec = pl.BlockSpec((None, N_MEM, W_M), lambda b, i: (b, 0, 0))
    return pl.pallas_call(
        _mem_attention_kernel,
        out_shape=jax.ShapeDtypeStruct((B, S, W_M), BF16),
        grid=(B, S // tm),
        in_specs=[pl.BlockSpec((None, tm, W_M), lambda b, i: (b, i, OFF_QM // W_M)), mspec, mspec],
        out_specs=pl.BlockSpec((None, tm, W_M), lambda b, i: (b, i, 0)),
        compiler_params=_cparams(("parallel", "parallel")),
        name="memory_attention",
    )(proj3, km3, vm3)


def _group_sum(x, g_ref):
    hi, lo = _split2(x)
    return (jnp.dot(hi, g_ref[...], preferred_element_type=F32)
            + jnp.dot(lo, g_ref[...], preferred_element_type=F32))


def _rwkv_prep_kernel(*refs, has_vres):
    refs = list(refs)
    r_ref, k_ref, v_ref, l_ref = refs[:4]
    vf_ref = refs[4] if has_vres else None
    (mu_ref, mul_ref, par_ref, wl_ref, g_ref, tri_ref, same_ref,
     ro_ref, ko_ref, vo_ref, go_ref, gl_ref, at_ref, rt_ref, bt_ref, kt_ref, bt2_ref, kt2_ref,
     prev_ref, prevl_ref) = refs[4 + int(has_vres):]
    s_idx = pl.program_id(1)

    @pl.when(s_idx == 0)
    def _():
        prev_ref[...] = jnp.zeros_like(prev_ref)
        prevl_ref[...] = jnp.zeros_like(prevl_ref)

    def shift(x, carry_row, mu):
        tm = x.shape[0]
        rolled = pltpu.roll(x, shift=1, axis=0)
        first = lax.broadcasted_iota(jnp.int32, x.shape, 0) == 0
        prev = jnp.where(first, jnp.broadcast_to(carry_row, x.shape), rolled)
        return x + (prev - x) * mu, x[tm - 1:tm, :]

    r, r_last = shift(r_ref[...], prev_ref[0:1, :], mu_ref[0:1, :])
    k, k_last = shift(k_ref[...], prev_ref[1:2, :], mu_ref[1:2, :])
    v, v_last = shift(v_ref[...], prev_ref[2:3, :], mu_ref[2:3, :])
    lo, l_last = shift(l_ref[...], prevl_ref[0:1, :], mul_ref[...])
    prev_ref[0:1, :] = r_last
    prev_ref[1:2, :] = k_last
    prev_ref[2:3, :] = v_last
    prevl_ref[0:1, :] = l_last

    lane = lax.broadcasted_iota(jnp.int32, lo.shape, 1)
    lt = jnp.where(lane < W_LORA, jnp.tanh(lo),
                   jnp.where((lane >= W_LORA + A_LORA) & (lane < W_LORA + A_LORA + G_LORA),
                             jax.nn.sigmoid(lo), lo))
    up = jnp.dot(lt.astype(BF16), wl_ref[...], preferred_element_type=F32)
    w0, a0, v0 = par_ref[0:1, :], par_ref[1:2, :], par_ref[2:3, :]
    k_k, k_a = par_ref[3:4, :], par_ref[4:5, :]

    y = -(w0 + up[:, 0:W_B])
    softplus = jnp.maximum(y, 0.0) + jnp.log(1.0 + jnp.exp(-jnp.abs(y)))
    w_log = -softplus - 0.5
    lw = -jnp.exp(w_log)
    a = jax.nn.sigmoid(a0 + up[:, W_B:2 * W_B])
    go_ref[...] = up[:, 2 * W_B:3 * W_B]
    if has_vres:
        v = v + (vf_ref[...] - v) * jax.nn.sigmoid(v0 + up[:, 3 * W_B:4 * W_B])
    kk = k * k_k
    norm = jnp.sqrt(_group_sum(kk * kk, g_ref))
    kk = kk / jnp.maximum(norm, 1e-12)
    k = k * (1.0 + (a - 1.0) * k_a)
    ro_ref[...] = r
    ko_ref[...] = k
    vo_ref[...] = v

    p0, p1, p2 = _split3(lw)
    tri, same = tri_ref[...], same_ref[...]
    cs = (jnp.dot(tri, p0, preferred_element_type=F32) + jnp.dot(tri, p1, preferred_element_type=F32)
          + jnp.dot(tri, p2, preferred_element_type=F32))
    cl = (jnp.dot(same, p0, preferred_element_type=F32) + jnp.dot(same, p1, preferred_element_type=F32)
          + jnp.dot(same, p2, preferred_element_type=F32))
    e_neg = jnp.exp(-cs)
    to_end = jnp.exp(cl - cs)
    kka = kk * a
    at_ref[...] = (-kk * jnp.exp(cs - lw)).astype(BF16)
    rt_ref[...] = (r * jnp.exp(cs)).astype(BF16)
    bt_ref[...] = (kka * e_neg).astype(BF16)
    kt_ref[...] = (k * e_neg).astype(BF16)
    bt2_ref[...] = (kka * to_end).astype(BF16)
    kt2_ref[...] = (k * to_end).astype(BF16)
    gl_ref[...] = jnp.exp(cl)


def _chunk_matrices(tm):
    t = np.arange(tm)
    same = (t[:, None] // CHUNK) == (t[None, :] // CHUNK)
    tri = same & (t[None, :] <= t[:, None])
    return jnp.asarray(tri, BF16), jnp.asarray(same, BF16)


def rwkv_prep(proj3, v_first, mu_rkv, mu_lora, params, w_lora, gmat, tm=256):
    B, S, _ = proj3.shape
    has_vres = v_first is not None
    tok = lambda c: pl.BlockSpec((None, tm, W_B), lambda b, s: (b, s, c))
    full = lambda shp: pl.BlockSpec(shp, lambda b, s: (0,) * len(shp))
    in_specs = [tok(OFF_RKV // W_B), tok(OFF_RKV // W_B + 1), tok(OFF_RKV // W_B + 2),
                pl.BlockSpec((None, tm, LORA_W), lambda b, s: (b, s, OFF_LORA // LORA_W))]
    args = [proj3, proj3, proj3, proj3]
    if has_vres:
        in_specs.append(tok(0))
        args.append(v_first)
    in_specs += [full((3, W_B)), full((1, LORA_W)), full((8, W_B)), full((LORA_W, 4 * W_B)),
                 full((W_B, W_B)), full((tm, tm)), full((tm, tm))]
    args += [mu_rkv, mu_lora, params, w_lora, gmat, *_chunk_matrices(tm)]
    out32 = jax.ShapeDtypeStruct((B, S, W_B), F32)
    out16 = jax.ShapeDtypeStruct((B, S, W_B), BF16)
    return pl.pallas_call(
        functools.partial(_rwkv_prep_kernel, has_vres=has_vres),
        out_shape=(out32,) * 5 + (out16,) * 6,
        grid=(B, S // tm),
        in_specs=in_specs,
        out_specs=(tok(0),) * 11,
        scratch_shapes=[pltpu.VMEM((8, W_B), F32), pltpu.VMEM((8, LORA_W), F32)],
        compiler_params=_cparams(("parallel", "arbitrary")),
        name="rwkv_prep",
    )(*args)


SCAN_HPG = 16


def _rwkv_scan_kernel(at_ref, rt_ref, bt_ref, kt_ref, bt2_ref, kt2_ref, v_ref, gl_ref, o_ref,
                      state_ref):
    C, N, H = CHUNK, N_B, SCAN_HPG

    @pl.when(pl.program_id(2) == 0)
    def _():
        state_ref[...] = jnp.zeros_like(state_ref)

    row = lax.broadcasted_iota(jnp.int32, (C, C), 0)
    col = lax.broadcasted_iota(jnp.int32, (C, C), 1)
    incl = (col <= row).astype(F32)
    strict = (col < row).astype(F32)
    eye = (col == row).astype(F32)
    nt = (((1,), (1,)), ((), ()))
    dot = functools.partial(jnp.dot, preferred_element_type=F32)
    dot_nt = lambda x, y: lax.dot_general(x, y, nt, preferred_element_type=F32)
    hs = range(H)
    sl = [slice(h * N, (h + 1) * N) for h in hs]

    x = [jnp.concatenate([at_ref[:, sl[h]], rt_ref[:, sl[h]]], axis=0) for h in hs]
    v = [v_ref[:, sl[h]] for h in hs]
    vb = [v[h].astype(BF16) for h in hs]
    st = [state_ref[h] for h in hs]
    mb = [dot_nt(x[h], bt_ref[:, sl[h]]) for h in hs]
    mk = [dot_nt(x[h], kt_ref[:, sl[h]]) for h in hs]
    xh = [dot_nt(x[h], st[h].astype(BF16)) for h in hs]
    lab = [mb[h][:C] * strict for h in hs]
    mrb = [(mb[h][C:] * incl).astype(BF16) for h in hs]
    lm = [jnp.concatenate([mk[h][:C] * strict, mk[h][C:] * incl], axis=0).astype(BF16) for h in hs]
    lv = [dot(lm[h], vb[h]) for h in hs]
    t = [eye + lab[h] for h in hs]
    p = lab
    for _ in range(5):
        pb = [p[h].astype(BF16) for h in hs]
        p = [dot(pb[h], pb[h]) for h in hs]
        t = [t[h] + dot(t[h].astype(BF16), p[h].astype(BF16)) for h in hs]
    u = [dot(t[h].astype(BF16), (xh[h][:C] + lv[h][:C]).astype(BF16)) for h in hs]
    o = [xh[h][C:] + dot(mrb[h], u[h].astype(BF16)) + lv[h][C:] for h in hs]
    for h in hs:
        o_ref[:, sl[h]] = o[h]
    uv_t = []
    for h in hs:
        uv = jnp.concatenate([u[h], v[h]], axis=0)
        uv_t.append(jnp.concatenate([uv, jnp.zeros_like(uv)], axis=1).T[:N].astype(BF16))
    for h in hs:
        y2 = jnp.concatenate([bt2_ref[:, sl[h]], kt2_ref[:, sl[h]]], axis=0)
        state_ref[h] = st[h] * gl_ref[0:1, sl[h]] + dot(uv_t[h], y2)


def rwkv_scan(at, rt, bt, kt, bt2, kt2, v, gl):
    B, S, _ = v.shape
    wb = SCAN_HPG * N_B
    spec = pl.BlockSpec((None, CHUNK, wb), lambda b, g, c: (b, c, g))
    return pl.pallas_call(
        _rwkv_scan_kernel,
        out_shape=jax.ShapeDtypeStruct((B, S, W_B), F32),
        grid=(B, H_B // SCAN_HPG, S // CHUNK),
        in_specs=[spec] * 8,
        out_specs=spec,
        scratch_shapes=[pltpu.VMEM((SCAN_HPG, N_B, N_B), F32)],
        compiler_params=_cparams(("parallel", "parallel", "arbitrary")),
        name="rwkv_scan",
    )(at, rt, bt, kt, bt2, kt2, v, gl)


def _rwkv_post_kernel(o_ref, r_ref, k_ref, v_ref, g_ref, par_ref, gm_ref, out_ref):
    o = o_ref[...]
    mu = _group_sum(o, gm_ref) * (1.0 / N_B)
    oc = o - mu
    var = _group_sum(oc * oc, gm_ref) * (1.0 / N_B)
    on = oc * lax.rsqrt(var + LNX_EPS) * par_ref[0:1, :] + par_ref[1:2, :]
    bonus = _group_sum(r_ref[...] * k_ref[...] * par_ref[2:3, :], gm_ref) * v_ref[...]
    out_ref[...] = ((on + bonus) * g_ref[...]).astype(BF16)


def rwkv_post(o, r, k, v, g, params, gmat, tm=256):
    M = o.shape[0]
    row = pl.BlockSpec((tm, W_B), lambda i: (i, 0))
    return pl.pallas_call(
        _rwkv_post_kernel,
        out_shape=jax.ShapeDtypeStruct((M, W_B), BF16),
        grid=(M // tm,),
        in_specs=[row] * 5 + [pl.BlockSpec((8, W_B), lambda i: (0, 0)),
                              pl.BlockSpec((W_B, W_B), lambda i: (0, 0))],
        out_specs=row,
        compiler_params=_cparams(("parallel",)),
        name="rwkv_post",
    )(o, r, k, v, g, params, gmat)


def _merge_kernel(oa_ref, ob_ref, oc_ref, wa_ref, wb_ref, wc_ref, g0_ref, g1_ref, g2_ref, out_ref):
    ya = jnp.dot(oa_ref[...], wa_ref[...], preferred_element_type=F32)
    yb = jnp.dot(ob_ref[...], wb_ref[...], preferred_element_type=F32)
    yc = jnp.dot(oc_ref[...], wc_ref[...], preferred_element_type=F32)
    out = (jax.nn.sigmoid(g0_ref[...]) * ya + jax.nn.sigmoid(g1_ref[...]) * yb
           + jax.nn.sigmoid(g2_ref[...]) * yc)
    out_ref[...] = out.astype(BF16)


def gated_merge(o_a, o_b, o_c, w_pa, w_pb, w_pc, proj2, tm=512, tn=1024):
    M = o_a.shape[0]
    D = D_MODEL
    nj = D // tn
    act = lambda w: pl.BlockSpec((tm, w), lambda i, j: (i, 0))
    wgt = lambda w: pl.BlockSpec((w, tn), lambda i, j: (0, j))
    gate = lambda br: pl.BlockSpec((tm, tn), lambda i, j: (i, OFF_GATE // tn + br * nj + j))
    return pl.pallas_call(
        _merge_kernel,
        out_shape=jax.ShapeDtypeStruct((M, D), BF16),
        grid=(M // tm, nj),
        in_specs=[act(W_A), act(W_B), act(W_M), wgt(W_A), wgt(W_B), wgt(W_M),
                  gate(0), gate(1), gate(2)],
        out_specs=pl.BlockSpec((tm, tn), lambda i, j: (i, j)),
        compiler_params=_cparams(("parallel", "arbitrary")),
        name="gated_merge",
    )(o_a, o_b, o_c, w_pa, w_pb, w_pc, proj2, proj2, proj2)


def _first_index_of_max(x, idx, n):
    m = jnp.max(x, axis=0, keepdims=True)
    first = jnp.min(jnp.where(x == m, idx, n), axis=0, keepdims=True)
    return m, idx == first


def _router_kernel(h_ref, wr_ref, bias_ref, gate_ref):
    E, G = N_EXPERTS, N_GROUPS
    per = E // G
    logits = lax.dot_general(wr_ref[...], h_ref[...], (((1,), (1,)), ((), ())),
                             preferred_element_type=F32)
    tm = logits.shape[1]
    scores = jax.nn.sigmoid(logits)
    biased = scores + jnp.concatenate([bias_ref[...]] * (tm // 128), axis=1)
    idx8 = lax.broadcasted_iota(jnp.int32, (per, tm), 0)
    gscores = []
    for g in range(G):
        xg = biased[g * per:(g + 1) * per, :]
        m1, hit = _first_index_of_max(xg, idx8, per)
        m2 = jnp.max(jnp.where(hit, -jnp.inf, xg), axis=0, keepdims=True)
        gscores.append(m1 + m2)
    gs = jnp.concatenate(gscores, axis=0)
    gidx = lax.broadcasted_iota(jnp.int32, (G, tm), 0)
    gsel = jnp.zeros((G, tm), F32)
    for _ in range(TOPK_GROUPS):
        _, hit = _first_index_of_max(gs, gidx, G)
        gsel = jnp.where(hit, 1.0, gsel)
        gs = jnp.where(hit, -jnp.inf, gs)
    emask = jnp.concatenate(
        [jnp.broadcast_to(gsel[g:g + 1, :], (per, tm)) for g in range(G)], axis=0)
    cand = jnp.where(emask > 0.0, biased, -jnp.inf)
    eidx = lax.broadcasted_iota(jnp.int32, (E, tm), 0)
    picked = jnp.zeros((E, tm), F32)
    for _ in range(TOP_K):
        _, hit = _first_index_of_max(cand, eidx, E)
        picked = jnp.where(hit, scores, picked)
        cand = jnp.where(hit, -jnp.inf, cand)
    gate = picked / jnp.sum(picked, axis=0, keepdims=True) * ROUTED_SCALE
    shared = (eidx == 0).astype(F32)
    gate_ref[...] = jnp.concatenate([gate, shared], axis=0).T


def moe_router(hb, w_router_t, bias_b, tm=256):
    M, D = hb.shape
    return pl.pallas_call(
        _router_kernel,
        out_shape=jax.ShapeDtypeStruct((M, 2 * N_EXPERTS), F32),
        grid=(M // tm,),
        in_specs=[pl.BlockSpec((tm, D), lambda i: (i, 0)),
                  pl.BlockSpec((N_EXPERTS, D), lambda i: (0, 0)),
                  pl.BlockSpec((N_EXPERTS, 128), lambda i: (0, 0))],
        out_specs=pl.BlockSpec((tm, 2 * N_EXPERTS), lambda i: (i, 0)),
        compiler_params=_cparams(("parallel",)),
        name="moe_router",
    )(hb, w_router_t, bias_b)


MOE_F = 6656


def _moe_act_kernel(x_ref, gate_ref, e_ref, w1_ref, w3_ref, o_ref):
    x = x_ref[...]
    h1 = jnp.dot(x, w1_ref[...], preferred_element_type=F32)
    h3 = jnp.dot(x, w3_ref[...], preferred_element_type=F32)
    g0, g1, g2 = _split3(gate_ref[...])
    e = e_ref[...]
    gexp = (jnp.dot(g0, e, preferred_element_type=F32) + jnp.dot(g1, e, preferred_element_type=F32)
            + jnp.dot(g2, e, preferred_element_type=F32))
    o_ref[...] = (h1 * jax.nn.sigmoid(h1) * h3 * gexp).astype(BF16)


def moe_hidden(hb, gate, expand, w1c, w3c, tm=1024, tn=512):
    M, D = hb.shape
    wspec = pl.BlockSpec((D, tn), lambda i, j: (0, j))
    return pl.pallas_call(
        _moe_act_kernel,
        out_shape=jax.ShapeDtypeStruct((M, MOE_F), BF16),
        grid=(M // tm, MOE_F // tn),
        in_specs=[pl.BlockSpec((tm, D), lambda i, j: (i, 0)),
                  pl.BlockSpec((tm, 2 * N_EXPERTS), lambda i, j: (i, 0)),
                  pl.BlockSpec((2 * N_EXPERTS, tn), lambda i, j: (0, j)),
                  wspec, wspec],
        out_specs=pl.BlockSpec((tm, tn), lambda i, j: (i, j)),
        compiler_params=_cparams(("parallel", "arbitrary")),
        name="moe_hidden",
    )(hb, gate, expand, w1c, w3c)


def _regroup_w_in(w, has_vres):
    D = w.shape[0]
    o = np.cumsum((0, W_A, R_KV, H_IDX * D_IDX, D_IDX, H_IDX, 3 * W_B, W_LORA, A_LORA, G_LORA, W_M,
                   3 * D_MODEL)).tolist()
    q_a, c_kv, q_idx = w[:, o[0]:o[1]], w[:, o[1]:o[2]], w[:, o[2]:o[3]]
    idx = w[:, o[3]:o[5]]
    rkv, lora = w[:, o[5]:o[6]], w[:, o[6]:o[9]]
    q_m, gates = w[:, o[9]:o[10]], w[:, o[10]:o[11]]
    zeros = lambda n: jnp.zeros((D, n), w.dtype)
    vres = w[:, o[11]:o[11] + V_LORA] if has_vres else zeros(V_LORA)
    n_lora = W_LORA + A_LORA + G_LORA + V_LORA
    parts = [gates, q_a, q_idx, rkv, q_m, c_kv, lora, vres, zeros(LORA_W - n_lora),
             idx, zeros(IDX_W - D_IDX - H_IDX)]
    used = sum(p.shape[1] for p in parts)
    parts.append(zeros(N_PROJ - used))
    return jnp.concatenate(parts, axis=1).astype(BF16)


def _lora_up_matrix(w_up, a_up, g_up, v_up):
    m = jnp.zeros((LORA_W, 4 * W_B), F32)
    m = m.at[0:W_LORA, 0:W_B].set(w_up)
    m = m.at[W_LORA:W_LORA + A_LORA, W_B:2 * W_B].set(a_up)
    m = m.at[W_LORA + A_LORA:W_LORA + A_LORA + G_LORA, 2 * W_B:3 * W_B].set(g_up)
    if v_up is not None:
        r0 = W_LORA + A_LORA + G_LORA
        m = m.at[r0:r0 + V_LORA, 3 * W_B:4 * W_B].set(v_up)
    return m.astype(BF16)


def _rows(vectors, n_rows, width):
    m = jnp.zeros((n_rows, width), F32)
    for i, vec in enumerate(vectors):
        m = m.at[i, :vec.shape[0]].set(vec)
    return m


def _expert_in_weights(w_e, w_s):
    D = w_e.shape[1]
    routed = jnp.transpose(w_e, (1, 0, 2)).reshape(D, N_EXPERTS * F_EXPERT)
    pad = jnp.zeros((D, MOE_F - N_EXPERTS * F_EXPERT - w_s.shape[1]), w_e.dtype)
    return jnp.concatenate([routed, w_s, pad], axis=1).astype(BF16)


def _expert_out_weights(w_e2, w_s2):
    D = w_e2.shape[2]
    routed = w_e2.reshape(N_EXPERTS * F_EXPERT, D)
    pad = jnp.zeros((MOE_F - N_EXPERTS * F_EXPERT - w_s2.shape[0], D), w_e2.dtype)
    return jnp.concatenate([routed, w_s2, pad], axis=0).astype(BF16)


def kernel(x, mem, ln0_g, ln0_b, w_in_first, w_in_rest, ckv_g, w_uk, w_uv, mu_rwkv, mu_vres, rw_w0, rw_w_up, rw_a0, rw_a_up, rw_g_up, rw_v0, rw_v_up, rw_k_k, rw_k_a, rw_r_k, rw_lnx_g, rw_lnx_b, w_mk, w_mv, w_pa, w_pb, w_pc, w_o, ln1_g, ln1_b, w_router, router_bias, w_e1, w_e3, w_e2, w_s1, w_s3, w_s2, ln2_g, ln2_b):
    B, S, D = x.shape
    T = B * S
    gmat = jnp.kron(jnp.eye(H_B, dtype=F32), jnp.ones((N_B, N_B), F32)).astype(BF16)
    unit = np.arange(MOE_F)
    owner = np.where(unit < (N_EXPERTS + 1) * F_EXPERT, unit // F_EXPERT, -1)
    expand = jnp.asarray(owner[None, :] == np.arange(2 * N_EXPERTS)[:, None], BF16)
    mem_b = mem.reshape(B * N_MEM, D).astype(BF16)

    h, hb = layer_norm(x.reshape(T, D), ln0_g, ln0_b)
    v_first = None
    for i in range(DEPTH):
        has_vres = i > 0
        w_in = _regroup_w_in(w_in_first if i == 0 else w_in_rest[i - 1], has_vres)
        proj2 = matmul(hb, w_in, F32, 1024, 1024)
        proj3 = proj2.reshape(B, S, N_PROJ)

        k2, v2 = kv_project(proj2, ckv_g[i], w_uk[i].astype(BF16), w_uv[i].astype(BF16))
        bias5 = indexer_bias(proj3)
        o_a = sparse_attention(proj3, k2.reshape(B, S, W_A), v2.reshape(B, S, W_A), bias5)

        mu = mu_rwkv[i]
        mu_rkv = mu[:3 * W_B].reshape(3, W_B)
        mu_l = [mu[3 * W_B:]] + ([mu_vres[i - 1]] if has_vres else [])
        mu_lora = _rows([jnp.concatenate(mu_l)], 1, LORA_W)
        prep_par = _rows([rw_w0[i], rw_a0[i], rw_v0[i - 1] if has_vres else jnp.zeros((W_B,), F32),
                          rw_k_k[i], rw_k_a[i]], 8, W_B)
        w_lora = _lora_up_matrix(rw_w_up[i], rw_a_up[i], rw_g_up[i],
                                 rw_v_up[i - 1] if has_vres else None)
        r_b, k_b, v_b, g_b, gl_b, *scan_ops = rwkv_prep(proj3, v_first, mu_rkv, mu_lora, prep_par,
                                                        w_lora, gmat)
        if i == 0:
            v_first = v_b
        o_scan = rwkv_scan(*scan_ops, v_b, gl_b)
        post_par = _rows([rw_lnx_g[i], rw_lnx_b[i], rw_r_k[i].reshape(W_B)], 8, W_B)
        flat = lambda z: z.reshape(T, W_B)
        o_b = rwkv_post(flat(o_scan), flat(r_b), flat(k_b), flat(v_b), flat(g_b), post_par, gmat)

        km = matmul(mem_b, w_mk[i].astype(BF16), BF16, 512, 512).reshape(B, N_MEM, W_M)
        vm = matmul(mem_b, w_mv[i].astype(BF16), BF16, 512, 512).reshape(B, N_MEM, W_M)
        o_c = memory_attention(proj3, km, vm)

        merged = gated_merge(o_a.reshape(T, W_A), o_b, o_c.reshape(T, W_M), w_pa[i].astype(BF16),
                             w_pb[i].astype(BF16), w_pc[i].astype(BF16), proj2)
        h, hb = matmul_residual_ln(merged, w_o[i].astype(BF16), h, ln1_g[i], ln1_b[i])

        bias_b = jnp.broadcast_to(router_bias[i].reshape(N_EXPERTS, 1), (N_EXPERTS, 128))
        gate = moe_router(hb, w_router[i].T.astype(BF16), bias_b)
        hidden = moe_hidden(hb, gate, expand, _expert_in_weights(w_e1[i], w_s1[i]),
                            _expert_in_weights(w_e3[i], w_s3[i]))
        h, hb = matmul_residual_ln(hidden, _expert_out_weights(w_e2[i], w_s2[i]), h,
                                   ln2_g[i], ln2_b[i])
    return h.reshape(B, S, D)
```

```python
import functools

import jax
import jax.numpy as jnp
import numpy as np
from jax import lax
from jax.experimental import pallas as pl
from jax.experimental.pallas import tpu as pltpu

F32 = jnp.float32
BF16 = jnp.bfloat16

D_MODEL = 4096
DEPTH = 4
CHUNK = 64
H_A, DH_A = 16, 128
W_A = H_A * DH_A
R_KV = 256
H_IDX, D_IDX = 32, 64
TOPK_MAX = 256
H_B, N_B = 16, 64
W_B = H_B * N_B
W_LORA, A_LORA, G_LORA, V_LORA = 64, 64, 160, 32
LNX_EPS = 64e-5
N_MEM = 256
H_M, DH_M = 4, 256
W_M = H_M * DH_M
N_EXPERTS, N_GROUPS, TOPK_GROUPS, TOP_K = 64, 8, 4, 8
F_EXPERT = 96
ROUTED_SCALE = 2.5
ALPHA = (2 * DEPTH) ** 0.25
LN_EPS = 1e-5

OFF_GATE = 0
OFF_QA = 3 * D_MODEL
OFF_QIDX = OFF_QA + W_A
OFF_RKV = OFF_QIDX + H_IDX * D_IDX
OFF_QM = OFF_RKV + 3 * W_B
OFF_CKV = OFF_QM + W_M
OFF_LORA = OFF_CKV + R_KV
LORA_W = 384
OFF_IDX = OFF_LORA + LORA_W
IDX_W = 128
N_PROJ = 21504

NEG_BIAS = -1e30
VMEM_LIMIT = 56 * 1024 * 1024


def _cparams(sem):
    return pltpu.CompilerParams(dimension_semantics=sem, vmem_limit_bytes=VMEM_LIMIT)


def _split2(x):
    hi = x.astype(BF16)
    lo = (x - hi.astype(F32)).astype(BF16)
    return hi, lo


def _split3(x):
    hi = x.astype(BF16)
    r1 = x - hi.astype(F32)
    mid = r1.astype(BF16)
    lo = (r1 - mid.astype(F32)).astype(BF16)
    return hi, mid, lo


def _mm_kernel(x_ref, w_ref, o_ref):
    o_ref[...] = jnp.dot(x_ref[...], w_ref[...], preferred_element_type=F32).astype(o_ref.dtype)


def matmul(x, w, out_dtype, tm, tn):
    M, K = x.shape
    N = w.shape[1]
    tm, tn = min(tm, M), min(tn, N)
    assert M % tm == 0 and N % tn == 0
    return pl.pallas_call(
        _mm_kernel,
        out_shape=jax.ShapeDtypeStruct((M, N), out_dtype),
        grid=(M // tm, N // tn),
        in_specs=[pl.BlockSpec((tm, K), lambda i, j: (i, 0)),
                  pl.BlockSpec((K, tn), lambda i, j: (0, j))],
        out_specs=pl.BlockSpec((tm, tn), lambda i, j: (i, j)),
        compiler_params=_cparams(("parallel", "arbitrary")),
        name="matmul",
    )(x, w)


def _ln_rows(y, g, b):
    mu = jnp.mean(y, axis=-1, keepdims=True)
    yc = y - mu
    var = jnp.mean(yc * yc, axis=-1, keepdims=True)
    return yc * lax.rsqrt(var + LN_EPS) * g + b


def _ln_kernel(x_ref, g_ref, b_ref, o_ref, ob_ref):
    y = _ln_rows(x_ref[...], g_ref[...], b_ref[...])
    o_ref[...] = y
    ob_ref[...] = y.astype(BF16)


def _res_ln_kernel(h_ref, f_ref, g_ref, b_ref, o_ref, ob_ref):
    y = _ln_rows(ALPHA * h_ref[...] + f_ref[...], g_ref[...], b_ref[...])
    o_ref[...] = y
    ob_ref[...] = y.astype(BF16)


def layer_norm(x, g, b, tm=256):
    M, D = x.shape
    row = pl.BlockSpec((tm, D), lambda i: (i, 0))
    vec = pl.BlockSpec((1, D), lambda i: (0, 0))
    return pl.pallas_call(
        _ln_kernel,
        out_shape=(jax.ShapeDtypeStruct((M, D), F32), jax.ShapeDtypeStruct((M, D), BF16)),
        grid=(M // tm,),
        in_specs=[row, vec, vec],
        out_specs=(row, row),
        compiler_params=_cparams(("parallel",)),
        name="layer_norm",
    )(x, g.reshape(1, D), b.reshape(1, D))


def residual_layer_norm(h, f, g, b, tm=256):
    M, D = h.shape
    row = pl.BlockSpec((tm, D), lambda i: (i, 0))
    vec = pl.BlockSpec((1, D), lambda i: (0, 0))
    return pl.pallas_call(
        _res_ln_kernel,
        out_shape=(jax.ShapeDtypeStruct((M, D), F32), jax.ShapeDtypeStruct((M, D), BF16)),
        grid=(M // tm,),
        in_specs=[row, row, vec, vec],
        out_specs=(row, row),
        compiler_params=_cparams(("parallel",)),
        name="residual_layer_norm",
    )(h, f, g.reshape(1, D), b.reshape(1, D))


def _mm_res_ln_kernel(x_ref, w_ref, h_ref, g_ref, b_ref, o_ref, ob_ref, *, nj, tn):
    j = pl.program_id(1)
    y = jnp.dot(x_ref[...], w_ref[...], preferred_element_type=F32)
    for jj in range(nj):
        @pl.when(j == jj)
        def _(jj=jj):
            o_ref[:, jj * tn:(jj + 1) * tn] = y

    @pl.when(j == nj - 1)
    def _():
        D = nj * tn
        s1 = None
        for jj in range(nj):
            sl = slice(jj * tn, (jj + 1) * tn)
            z = ALPHA * h_ref[:, sl] + o_ref[:, sl]
            o_ref[:, sl] = z
            p = jnp.sum(z, axis=-1, keepdims=True)
            s1 = p if s1 is None else s1 + p
        mu = s1 * (1.0 / D)
        s2 = None
        for jj in range(nj):
            zc = o_ref[:, jj * tn:(jj + 1) * tn] - mu
            p = jnp.sum(zc * zc, axis=-1, keepdims=True)
            s2 = p if s2 is None else s2 + p
        rstd = lax.rsqrt(s2 * (1.0 / D) + LN_EPS)
        for jj in range(nj):
            sl = slice(jj * tn, (jj + 1) * tn)
            z = (o_ref[:, sl] - mu) * rstd * g_ref[:, sl] + b_ref[:, sl]
            o_ref[:, sl] = z
            ob_ref[:, sl] = z.astype(BF16)


def matmul_residual_ln(x, w, h, g, b, tm=512, tn=512):
    M, K = x.shape
    D = w.shape[1]
    nj = D // tn
    row = pl.BlockSpec((tm, D), lambda i, j: (i, 0), pipeline_mode=pl.Buffered(1))
    vec = pl.BlockSpec((1, D), lambda i, j: (0, 0))
    return pl.pallas_call(
        functools.partial(_mm_res_ln_kernel, nj=nj, tn=tn),
        out_shape=(jax.ShapeDtypeStruct((M, D), F32), jax.ShapeDtypeStruct((M, D), BF16)),
        grid=(M // tm, nj),
        in_specs=[pl.BlockSpec((tm, K), lambda i, j: (i, 0), pipeline_mode=pl.Buffered(1)),
                  pl.BlockSpec((K, tn), lambda i, j: (0, j)),
                  row, vec, vec],
        out_specs=(row, row),
        compiler_params=_cparams(("parallel", "arbitrary")),
        name="matmul_residual_ln",
    )(x, w, h, g.reshape(1, D), b.reshape(1, D))


def _kv_kernel(c_ref, g_ref, wk_ref, wv_ref, k_ref, v_ref):
    c = c_ref[...]
    ms = jnp.mean(c * c, axis=-1, keepdims=True)
    cn = (c * lax.rsqrt(ms + LN_EPS) * g_ref[...]).astype(BF16)
    k_ref[...] = jnp.dot(cn, wk_ref[...], preferred_element_type=F32).astype(BF16)
    v_ref[...] = jnp.dot(cn, wv_ref[...], preferred_element_type=F32).astype(BF16)


def kv_project(proj2, ckv_g, w_uk, w_uv, tm=512):
    M = proj2.shape[0]
    out = jax.ShapeDtypeStruct((M, W_A), BF16)
    wspec = pl.BlockSpec((R_KV, W_A), lambda i: (0, 0))
    ospec = pl.BlockSpec((tm, W_A), lambda i: (i, 0))
    return pl.pallas_call(
        _kv_kernel,
        out_shape=(out, out),
        grid=(M // tm,),
        in_specs=[pl.BlockSpec((tm, R_KV), lambda i: (i, OFF_CKV // R_KV)),
                  pl.BlockSpec((1, R_KV), lambda i: (0, 0)), wspec, wspec],
        out_specs=(ospec, ospec),
        compiler_params=_cparams(("parallel",)),
        name="kv_project",
    )(proj2, ckv_g.reshape(1, R_KV), w_uk, w_uv)


IDX_TQ = 128
IDX_TK = 256
KEY_NEG_INF = -2139095041
INT_MIN = -2147483648


def _indexer_kernel(q_ref, iq_ref, ik_ref, below_ref, bias_ref, qt_ref, wt_ref, key_ref, *, n_kt,
                    ksel):
    tq, tk = IDX_TQ, IDX_TK
    i = pl.program_id(1)
    qt_ref[...] = (q_ref[...].T * (D_IDX ** -0.5)).astype(BF16)
    wt_ref[...] = iq_ref[...].T * (H_IDX ** -0.5)

    n_t = ((i + 1) * tq + tk - 1) // tk
    q_chunk = (i * tq + lax.broadcasted_iota(jnp.int32, (tk, tq), 1)) // CHUNK
    row_iota = lax.broadcasted_iota(jnp.int32, (tk, tq), 0)

    def score_tile(j, carry):
        kt = ik_ref[pl.ds(pl.multiple_of(j * tk, tk), tk), :][:, :D_IDX].astype(BF16)
        acc = jnp.zeros((tk, tq), F32)
        for h2 in range(H_IDX // 2):
            w2 = jnp.concatenate([qt_ref[(2 * h2 + t) * D_IDX:(2 * h2 + t + 1) * D_IDX, :]
                                  for t in range(2)], axis=1)
            z = jnp.dot(kt, w2, preferred_element_type=F32)
            for t in range(2):
                h = 2 * h2 + t
                acc = acc + wt_ref[D_IDX + h:D_IDX + h + 1, :] * jnp.maximum(
                    z[:, t * tq:(t + 1) * tq], 0.0)
        admissible = ((j * tk + row_iota) // CHUNK) <= q_chunk
        score = jnp.where(admissible, acc, -jnp.inf)
        bits = lax.bitcast_convert_type(score, jnp.int32)
        key_ref[j] = bits ^ ((bits >> 31) & 0x7FFFFFFF)
        return carry

    lax.fori_loop(0, n_t, score_tile, 0)

    def count_keys(pred):
        def count_tile(j, cnts):
            c = pred(key_ref[j]).astype(jnp.int32)
            cnts = list(cnts)
            for r in range(tk // 8):
                cnts[r % len(cnts)] = cnts[r % len(cnts)] + c[r * 8:(r + 1) * 8, :]
            return tuple(cnts)

        cnts = lax.fori_loop(0, n_t, count_tile, (jnp.zeros((8, tq), jnp.int32),) * 4)
        cnt = (cnts[0] + cnts[1]) + (cnts[2] + cnts[3])
        return jnp.sum(cnt.astype(F32), axis=0, keepdims=True)

    def bit_step(it, thr):
        cand = thr + lax.shift_left(jnp.int32(1), 31 - it)
        return jnp.where(count_keys(lambda key: key >= cand) >= float(ksel), cand, thr)

    thr = lax.fori_loop(0, 32, bit_step, jnp.full((1, tq), INT_MIN, jnp.int32))
    ties_needed = float(ksel) - count_keys(lambda key: key > thr)
    below = below_ref[...]

    def bias_tile(j, ties_seen):
        key = key_ref[j]
        tied = key == thr
        tied_b = jnp.where(tied, 1.0, 0.0).astype(BF16)
        rank = ties_seen + jnp.dot(below, tied_b, preferred_element_type=F32)
        sel = ((key > thr) | (tied & (rank < ties_needed))) & (key > KEY_NEG_INF)
        bias_ref[j] = jnp.where(sel, 0.0, NEG_BIAS).T.astype(BF16)
        return ties_seen + jnp.sum(tied_b.astype(F32), axis=0, keepdims=True)

    def bias_pair(jj, ties_seen):
        return bias_tile(2 * jj + 1, bias_tile(2 * jj, ties_seen))

    ties_seen = lax.fori_loop(0, n_t // 2, bias_pair, jnp.zeros((1, tq), F32))

    @pl.when(n_t % 2 == 1)
    def _():
        bias_tile(n_t - 1, ties_seen)

    def fill_tile(j, carry):
        bias_ref[j] = jnp.full((tq, tk), NEG_BIAS, BF16)
        return carry

    lax.fori_loop(n_t, n_kt, fill_tile, 0)


def indexer_bias(proj3):
    B, S, _ = proj3.shape
    tq, tk = IDX_TQ, IDX_TK
    n_qt, n_kt = S // tq, S // tk
    ksel = min(TOPK_MAX, S // 4)
    below = jnp.asarray(np.tril(np.ones((tk, tk)), -1), BF16)
    return pl.pallas_call(
        functools.partial(_indexer_kernel, n_kt=n_kt, ksel=ksel),
        out_shape=jax.ShapeDtypeStruct((B, n_qt, n_kt, tq, tk), BF16),
        grid=(B, n_qt),
        in_specs=[pl.BlockSpec((None, tq, H_IDX * D_IDX), lambda b, i: (b, i, OFF_QIDX // (H_IDX * D_IDX))),
                  pl.BlockSpec((None, tq, IDX_W), lambda b, i: (b, i, OFF_IDX // IDX_W)),
                  pl.BlockSpec((None, S, IDX_W), lambda b, i: (b, 0, OFF_IDX // IDX_W)),
                  pl.BlockSpec((tk, tk), lambda b, i: (0, 0))],
        out_specs=pl.BlockSpec((None, None, n_kt, tq, tk), lambda b, i: (b, i, 0, 0, 0)),
        scratch_shapes=[pltpu.VMEM((H_IDX * D_IDX, tq), BF16),
                        pltpu.VMEM((IDX_W, tq), F32),
                        pltpu.VMEM((n_kt, tk, tq), jnp.int32)],
        compiler_params=_cparams(("parallel", "parallel")),
        name="indexer_bias",
    )(proj3, proj3, proj3, below)


ATT_TQ = 256
ATT_HP = 2
ATT_TK = 512


ATT_RG = 16


def _attention_kernel(q_ref, k_ref, v_ref, bias_ref, o_ref, qs_ref, s_ref, p_ref, m_ref, a_ref,
                      acc_ref):
    tq, tk, hp = ATT_TQ, ATT_TK, ATT_HP
    nq = tq // IDX_TQ
    nb = tk // IDX_TK
    i = pl.program_id(2)
    scale = DH_A ** -0.5 * np.log2(np.e)
    nt = (((1,), (1,)), ((), ()))
    streams = [(hh, qq) for hh in range(hp) for qq in range(nq)]
    cols = [slice(hh * DH_A, (hh + 1) * DH_A) for hh in range(hp)]
    rows = [slice(qq * IDX_TQ, (qq + 1) * IDX_TQ) for qq in range(nq)]
    for c, (hh, qq) in enumerate(streams):
        qs_ref[c] = (q_ref[rows[qq], cols[hh]] * scale).astype(BF16)
    m_ref[...] = jnp.full(m_ref.shape, NEG_BIAS, F32)
    acc_ref[...] = jnp.zeros_like(acc_ref)
    p_ref[1] = jnp.zeros(p_ref.shape[1:], BF16)
    a_ref[1] = jnp.zeros(a_ref.shape[1:], F32)
    n_kv = k_ref.shape[0] // tk
    n_t = ((i + 1) * tq + tk - 1) // tk
    ones = jnp.ones((tk, DH_A), BF16)

    def logits(j, slot):
        row0 = pl.multiple_of(j * tk, tk)
        for c, (hh, qq) in enumerate(streams):
            s_ref[slot, c] = lax.dot_general(qs_ref[c], k_ref[pl.ds(row0, tk), cols[hh]], nt,
                                             preferred_element_type=F32)

    def weighted_values(j, slot):
        row0 = pl.multiple_of(j * tk, tk)
        return [jnp.dot(p_ref[slot, c],
                        jnp.concatenate([v_ref[pl.ds(row0, tk), cols[hh]], ones], axis=1),
                        preferred_element_type=F32) for c, (hh, qq) in enumerate(streams)]

    def accumulate(pv, slot):
        for c in range(len(streams)):
            acc_ref[c] = jnp.concatenate([a_ref[slot, c]] * 2, axis=1) * acc_ref[c] + pv[c]

    def stage(j, slot):
        other = 1 - slot
        logits(jnp.minimum(j + 1, n_kv - 1), other)
        pv = weighted_values(jnp.maximum(j - 1, 0), other)
        for c, (hh, qq) in enumerate(streams):
            for r in range(IDX_TQ // ATT_RG):
                rs = slice(r * ATT_RG, (r + 1) * ATT_RG)
                bias = jnp.concatenate([bias_ref[qq, nb * j + t, rs, :] for t in range(nb)], axis=1)
                sc = s_ref[slot, c, rs, :] + bias.astype(F32)
                m_old = m_ref[c, rs, :]
                m_new = jnp.maximum(m_old, jnp.max(sc, axis=-1, keepdims=True))
                a_ref[slot, c, rs, :] = jnp.exp2(m_old - m_new)
                m_ref[c, rs, :] = m_new
                p_ref[slot, c, rs, :] = jnp.exp2(
                    sc - jnp.concatenate([m_new] * (tk // 128), axis=1)).astype(BF16)
        accumulate(pv, other)

    logits(0, 0)

    def body(jj, carry):
        stage(2 * jj, 0)
        stage(2 * jj + 1, 1)
        return carry

    n_pairs = (n_t + 1) // 2
    lax.fori_loop(0, n_pairs, body, 0)
    accumulate(weighted_values(2 * n_pairs - 1, 1), 1)
    for c, (hh, qq) in enumerate(streams):
        o_ref[rows[qq], cols[hh]] = (acc_ref[c, :, :DH_A] / acc_ref[c, :, DH_A:]).astype(BF16)


def sparse_attention(proj3, k3, v3, bias5):
    B, S, _ = proj3.shape
    tq, hp = ATT_TQ, ATT_HP
    wb = hp * DH_A
    n_kt = S // IDX_TK
    n_streams = hp * (tq // IDX_TQ)
    assert S % (2 * ATT_TK) == 0
    return pl.pallas_call(
        _attention_kernel,
        out_shape=jax.ShapeDtypeStruct((B, S, W_A), BF16),
        grid=(B, H_A // hp, S // tq),
        in_specs=[pl.BlockSpec((None, tq, wb), lambda b, h, i: (b, i, OFF_QA // wb + h)),
                  pl.BlockSpec((None, S, wb), lambda b, h, i: (b, 0, h)),
                  pl.BlockSpec((None, S, wb), lambda b, h, i: (b, 0, h)),
                  pl.BlockSpec((None, tq // IDX_TQ, n_kt, IDX_TQ, IDX_TK),
                               lambda b, h, i: (b, i, 0, 0, 0))],
        out_specs=pl.BlockSpec((None, tq, wb), lambda b, h, i: (b, i, h)),
        scratch_shapes=[pltpu.VMEM((n_streams, IDX_TQ, DH_A), BF16),
                        pltpu.VMEM((2, n_streams, IDX_TQ, ATT_TK), F32),
                        pltpu.VMEM((2, n_streams, IDX_TQ, ATT_TK), BF16),
                        pltpu.VMEM((n_streams, IDX_TQ, 128), F32),
                        pltpu.VMEM((2, n_streams, IDX_TQ, 128), F32),
                        pltpu.VMEM((n_streams, IDX_TQ, 2 * DH_A), F32)],
        compiler_params=_cparams(("parallel", "parallel", "parallel")),
        name="sparse_attention",
    )(proj3, k3, v3, bias5)


def _mem_attention_kernel(q_ref, km_ref, vm_ref, o_ref):
    scale = DH_M ** -0.5
    for h in range(H_M):
        sl = slice(h * DH_M, (h + 1) * DH_M)
        q = (q_ref[:, sl] * scale).astype(BF16)
        s = lax.dot_general(q, km_ref[:, sl], (((1,), (1,)), ((), ())), preferred_element_type=F32)
        m = jnp.max(s, axis=-1, keepdims=True)
        p = jnp.exp(s - m)
        p = p / jnp.sum(p, axis=-1, keepdims=True)
        o_ref[:, sl] = jnp.dot(p.astype(BF16), vm_ref[:, sl], preferred_element_type=F32).astype(BF16)


def memory_attention(proj3, km3, vm3, tm=512):
    B, S, _ = proj3.shape
    mspec = pl.BlockSpec((None, N_MEM, W_M), lambda b, i: (b, 0, 0))
    return pl.pallas_call(
        _mem_attention_kernel,
        out_shape=jax.ShapeDtypeStruct((B, S, W_M), BF16),
        grid=(B, S // tm),
        in_specs=[pl.BlockSpec((None, tm, W_M), lambda b, i: (b, i, OFF_QM // W_M)), mspec, mspec],
        out_specs=pl.BlockSpec((None, tm, W_M), lambda b, i: (b, i, 0)),
        compiler_params=_cparams(("parallel", "parallel")),
        name="memory_attention",
    )(proj3, km3, vm3)


def _group_sum(x, g_ref):
    hi, lo = _split2(x)
    return (jnp.dot(hi, g_ref[...], preferred_element_type=F32)
            + jnp.dot(lo, g_ref[...], preferred_element_type=F32))


def _rwkv_prep_kernel(*refs, has_vres):
    refs = list(refs)
    r_ref, k_ref, v_ref, l_ref = refs[:4]
    vf_ref = refs[4] if has_vres else None
    (mu_ref, mul_ref, par_ref, wl_ref, g_ref, tri_ref, same_ref,
     ro_ref, ko_ref, vo_ref, go_ref, gl_ref, at_ref, rt_ref, bt_ref, kt_ref, bt2_ref, kt2_ref,
     prev_ref, prevl_ref) = refs[4 + int(has_vres):]
    s_idx = pl.program_id(1)

    @pl.when(s_idx == 0)
    def _():
        prev_ref[...] = jnp.zeros_like(prev_ref)
        prevl_ref[...] = jnp.zeros_like(prevl_ref)

    def shift(x, carry_row, mu):
        tm = x.shape[0]
        rolled = pltpu.roll(x, shift=1, axis=0)
        first = lax.broadcasted_iota(jnp.int32, x.shape, 0) == 0
        prev = jnp.where(first, jnp.broadcast_to(carry_row, x.shape), rolled)
        return x + (prev - x) * mu, x[tm - 1:tm, :]

    r, r_last = shift(r_ref[...], prev_ref[0:1, :], mu_ref[0:1, :])
    k, k_last = shift(k_ref[...], prev_ref[1:2, :], mu_ref[1:2, :])
    v, v_last = shift(v_ref[...], prev_ref[2:3, :], mu_ref[2:3, :])
    lo, l_last = shift(l_ref[...], prevl_ref[0:1, :], mul_ref[...])
    prev_ref[0:1, :] = r_last
    prev_ref[1:2, :] = k_last
    prev_ref[2:3, :] = v_last
    prevl_ref[0:1, :] = l_last

    lane = lax.broadcasted_iota(jnp.int32, lo.shape, 1)
    lt = jnp.where(lane < W_LORA, jnp.tanh(lo),
                   jnp.where((lane >= W_LORA + A_LORA) & (lane < W_LORA + A_LORA + G_LORA),
                             jax.nn.sigmoid(lo), lo))
    up = jnp.dot(lt.astype(BF16), wl_ref[...], preferred_element_type=F32)
    w0, a0, v0 = par_ref[0:1, :], par_ref[1:2, :], par_ref[2:3, :]
    k_k, k_a = par_ref[3:4, :], par_ref[4:5, :]

    y = -(w0 + up[:, 0:W_B])
    softplus = jnp.maximum(y, 0.0) + jnp.log(1.0 + jnp.exp(-jnp.abs(y)))
    w_log = -softplus - 0.5
    lw = -jnp.exp(w_log)
    a = jax.nn.sigmoid(a0 + up[:, W_B:2 * W_B])
    go_ref[...] = up[:, 2 * W_B:3 * W_B]
    if has_vres:
        v = v + (vf_ref[...] - v) * jax.nn.sigmoid(v0 + up[:, 3 * W_B:4 * W_B])
    kk = k * k_k
    norm = jnp.sqrt(_group_sum(kk * kk, g_ref))
    kk = kk / jnp.maximum(norm, 1e-12)
    k = k * (1.0 + (a - 1.0) * k_a)
    ro_ref[...] = r
    ko_ref[...] = k
    vo_ref[...] = v

    p0, p1, p2 = _split3(lw)
    tri, same = tri_ref[...], same_ref[...]
    cs = (jnp.dot(tri, p0, preferred_element_type=F32) + jnp.dot(tri, p1, preferred_element_type=F32)
          + jnp.dot(tri, p2, preferred_element_type=F32))
    cl = (jnp.dot(same, p0, preferred_element_type=F32) + jnp.dot(same, p1, preferred_element_type=F32)
          + jnp.dot(same, p2, preferred_element_type=F32))
    e_neg = jnp.exp(-cs)
    to_end = jnp.exp(cl - cs)
    kka = kk * a
    at_ref[...] = (-kk * jnp.exp(cs - lw)).astype(BF16)
    rt_ref[...] = (r * jnp.exp(cs)).astype(BF16)
    bt_ref[...] = (kka * e_neg).astype(BF16)
    kt_ref[...] = (k * e_neg).astype(BF16)
    bt2_ref[...] = (kka * to_end).astype(BF16)
    kt2_ref[...] = (k * to_end).astype(BF16)
    gl_ref[...] = jnp.exp(cl)


def _chunk_matrices(tm):
    t = np.arange(tm)
    same = (t[:, None] // CHUNK) == (t[None, :] // CHUNK)
    tri = same & (t[None, :] <= t[:, None])
    return jnp.asarray(tri, BF16), jnp.asarray(same, BF16)


def rwkv_prep(proj3, v_first, mu_rkv, mu_lora, params, w_lora, gmat, tm=256):
    B, S, _ = proj3.shape
    has_vres = v_first is not None
    tok = lambda c: pl.BlockSpec((None, tm, W_B), lambda b, s: (b, s, c))
    full = lambda shp: pl.BlockSpec(shp, lambda b, s: (0,) * len(shp))
    in_specs = [tok(OFF_RKV // W_B), tok(OFF_RKV // W_B + 1), tok(OFF_RKV // W_B + 2),
                pl.BlockSpec((None, tm, LORA_W), lambda b, s: (b, s, OFF_LORA // LORA_W))]
    args = [proj3, proj3, proj3, proj3]
    if has_vres:
        in_specs.append(tok(0))
        args.append(v_first)
    in_specs += [full((3, W_B)), full((1, LORA_W)), full((8, W_B)), full((LORA_W, 4 * W_B)),
                 full((W_B, W_B)), full((tm, tm)), full((tm, tm))]
    args += [mu_rkv, mu_lora, params, w_lora, gmat, *_chunk_matrices(tm)]
    out32 = jax.ShapeDtypeStruct((B, S, W_B), F32)
    out16 = jax.ShapeDtypeStruct((B, S, W_B), BF16)
    return pl.pallas_call(
        functools.partial(_rwkv_prep_kernel, has_vres=has_vres),
        out_shape=(out32,) * 5 + (out16,) * 6,
        grid=(B, S // tm),
        in_specs=in_specs,
        out_specs=(tok(0),) * 11,
        scratch_shapes=[pltpu.VMEM((8, W_B), F32), pltpu.VMEM((8, LORA_W), F32)],
        compiler_params=_cparams(("parallel", "arbitrary")),
        name="rwkv_prep",
    )(*args)


SCAN_HPG = 16


def _rwkv_scan_kernel(at_ref, rt_ref, bt_ref, kt_ref, bt2_ref, kt2_ref, v_ref, gl_ref, o_ref,
                      state_ref):
    C, N, H = CHUNK, N_B, SCAN_HPG

    @pl.when(pl.program_id(2) == 0)
    def _():
        state_ref[...] = jnp.zeros_like(state_ref)

    row = lax.broadcasted_iota(jnp.int32, (C, C), 0)
    col = lax.broadcasted_iota(jnp.int32, (C, C), 1)
    incl = (col <= row).astype(F32)
    strict = (col < row).astype(F32)
    eye = (col == row).astype(F32)
    nt = (((1,), (1,)), ((), ()))
    dot = functools.partial(jnp.dot, preferred_element_type=F32)
    dot_nt = lambda x, y: lax.dot_general(x, y, nt, preferred_element_type=F32)
    hs = range(H)
    sl = [slice(h * N, (h + 1) * N) for h in hs]

    x = [jnp.concatenate([at_ref[:, sl[h]], rt_ref[:, sl[h]]], axis=0) for h in hs]
    v = [v_ref[:, sl[h]] for h in hs]
    vb = [v[h].astype(BF16) for h in hs]
    st = [state_ref[h] for h in hs]
    mb = [dot_nt(x[h], bt_ref[:, sl[h]]) for h in hs]
    mk = [dot_nt(x[h], kt_ref[:, sl[h]]) for h in hs]
    xh = [dot_nt(x[h], st[h].astype(BF16)) for h in hs]
    lab = [mb[h][:C] * strict for h in hs]
    mrb = [(mb[h][C:] * incl).astype(BF16) for h in hs]
    lm = [jnp.concatenate([mk[h][:C] * strict, mk[h][C:] * incl], axis=0).astype(BF16) for h in hs]
    lv = [dot(lm[h], vb[h]) for h in hs]
    t = [eye + lab[h] for h in hs]
    p = lab
    for _ in range(5):
        pb = [p[h].astype(BF16) for h in hs]
        p = [dot(pb[h], pb[h]) for h in hs]
        t = [t[h] + dot(t[h].astype(BF16), p[h].astype(BF16)) for h in hs]
    u = [dot(t[h].astype(BF16), (xh[h][:C] + lv[h][:C]).astype(BF16)) for h in hs]
    o = [xh[h][C:] + dot(mrb[h], u[h].astype(BF16)) + lv[h][C:] for h in hs]
    for h in hs:
        o_ref[:, sl[h]] = o[h]
    uv_t = []
    for h in hs:
        uv = jnp.concatenate([u[h], v[h]], axis=0)
        uv_t.append(jnp.concatenate([uv, jnp.zeros_like(uv)], axis=1).T[:N].astype(BF16))
    for h in hs:
        y2 = jnp.concatenate([bt2_ref[:, sl[h]], kt2_ref[:, sl[h]]], axis=0)
        state_ref[h] = st[h] * gl_ref[0:1, sl[h]] + dot(uv_t[h], y2)


def rwkv_scan(at, rt, bt, kt, bt2, kt2, v, gl):
    B, S, _ = v.shape
    wb = SCAN_HPG * N_B
    spec = pl.BlockSpec((None, CHUNK, wb), lambda b, g, c: (b, c, g))
    return pl.pallas_call(
        _rwkv_scan_kernel,
        out_shape=jax.ShapeDtypeStruct((B, S, W_B), F32),
        grid=(B, H_B // SCAN_HPG, S // CHUNK),
        in_specs=[spec] * 8,
        out_specs=spec,
        scratch_shapes=[pltpu.VMEM((SCAN_HPG, N_B, N_B), F32)],
        compiler_params=_cparams(("parallel", "parallel", "arbitrary")),
        name="rwkv_scan",
    )(at, rt, bt, kt, bt2, kt2, v, gl)


def _rwkv_post_kernel(o_ref, r_ref, k_ref, v_ref, g_ref, par_ref, gm_ref, out_ref):
    o = o_ref[...]
    mu = _group_sum(o, gm_ref) * (1.0 / N_B)
    oc = o - mu
    var = _group_sum(oc * oc, gm_ref) * (1.0 / N_B)
    on = oc * lax.rsqrt(var + LNX_EPS) * par_ref[0:1, :] + par_ref[1:2, :]
    bonus = _group_sum(r_ref[...] * k_ref[...] * par_ref[2:3, :], gm_ref) * v_ref[...]
    out_ref[...] = ((on + bonus) * g_ref[...]).astype(BF16)


def rwkv_post(o, r, k, v, g, params, gmat, tm=256):
    M = o.shape[0]
    row = pl.BlockSpec((tm, W_B), lambda i: (i, 0))
    return pl.pallas_call(
        _rwkv_post_kernel,
        out_shape=jax.ShapeDtypeStruct((M, W_B), BF16),
        grid=(M // tm,),
        in_specs=[row] * 5 + [pl.BlockSpec((8, W_B), lambda i: (0, 0)),
                              pl.BlockSpec((W_B, W_B), lambda i: (0, 0))],
        out_specs=row,
        compiler_params=_cparams(("parallel",)),
        name="rwkv_post",
    )(o, r, k, v, g, params, gmat)


def _merge_kernel(oa_ref, ob_ref, oc_ref, wa_ref, wb_ref, wc_ref, g0_ref, g1_ref, g2_ref, out_ref):
    ya = jnp.dot(oa_ref[...], wa_ref[...], preferred_element_type=F32)
    yb = jnp.dot(ob_ref[...], wb_ref[...], preferred_element_type=F32)
    yc = jnp.dot(oc_ref[...], wc_ref[...], preferred_element_type=F32)
    out = (jax.nn.sigmoid(g0_ref[...]) * ya + jax.nn.sigmoid(g1_ref[...]) * yb
           + jax.nn.sigmoid(g2_ref[...]) * yc)
    out_ref[...] = out.astype(BF16)


def gated_merge(o_a, o_b, o_c, w_pa, w_pb, w_pc, proj2, tm=512, tn=1024):
    M = o_a.shape[0]
    D = D_MODEL
    nj = D // tn
    act = lambda w: pl.BlockSpec((tm, w), lambda i, j: (i, 0))
    wgt = lambda w: pl.BlockSpec((w, tn), lambda i, j: (0, j))
    gate = lambda br: pl.BlockSpec((tm, tn), lambda i, j: (i, OFF_GATE // tn + br * nj + j))
    return pl.pallas_call(
        _merge_kernel,
        out_shape=jax.ShapeDtypeStruct((M, D), BF16),
        grid=(M // tm, nj),
        in_specs=[act(W_A), act(W_B), act(W_M), wgt(W_A), wgt(W_B), wgt(W_M),
                  gate(0), gate(1), gate(2)],
        out_specs=pl.BlockSpec((tm, tn), lambda i, j: (i, j)),
        compiler_params=_cparams(("parallel", "arbitrary")),
        name="gated_merge",
    )(o_a, o_b, o_c, w_pa, w_pb, w_pc, proj2, proj2, proj2)


def _first_index_of_max(x, idx, n):
    m = jnp.max(x, axis=0, keepdims=True)
    first = jnp.min(jnp.where(x == m, idx, n), axis=0, keepdims=True)
    return m, idx == first


def _router_kernel(h_ref, wr_ref, bias_ref, gate_ref):
    E, G = N_EXPERTS, N_GROUPS
    per = E // G
    logits = lax.dot_general(wr_ref[...], h_ref[...], (((1,), (1,)), ((), ())),
                             preferred_element_type=F32)
    tm = logits.shape[1]
    scores = jax.nn.sigmoid(logits)
    biased = scores + jnp.concatenate([bias_ref[...]] * (tm // 128), axis=1)
    idx8 = lax.broadcasted_iota(jnp.int32, (per, tm), 0)
    gscores = []
    for g in range(G):
        xg = biased[g * per:(g + 1) * per, :]
        m1, hit = _first_index_of_max(xg, idx8, per)
        m2 = jnp.max(jnp.where(hit, -jnp.inf, xg), axis=0, keepdims=True)
        gscores.append(m1 + m2)
    gs = jnp.concatenate(gscores, axis=0)
    gidx = lax.broadcasted_iota(jnp.int32, (G, tm), 0)
    gsel = jnp.zeros((G, tm), F32)
    for _ in range(TOPK_GROUPS):
        _, hit = _first_index_of_max(gs, gidx, G)
        gsel = jnp.where(hit, 1.0, gsel)
        gs = jnp.where(hit, -jnp.inf, gs)
    emask = jnp.concatenate(
        [jnp.broadcast_to(gsel[g:g + 1, :], (per, tm)) for g in range(G)], axis=0)
    cand = jnp.where(emask > 0.0, biased, -jnp.inf)
    eidx = lax.broadcasted_iota(jnp.int32, (E, tm), 0)
    picked = jnp.zeros((E, tm), F32)
    for _ in range(TOP_K):
        _, hit = _first_index_of_max(cand, eidx, E)
        picked = jnp.where(hit, scores, picked)
        cand = jnp.where(hit, -jnp.inf, cand)
    gate = picked / jnp.sum(picked, axis=0, keepdims=True) * ROUTED_SCALE
    shared = (eidx == 0).astype(F32)
    gate_ref[...] = jnp.concatenate([gate, shared], axis=0).T


def moe_router(hb, w_router_t, bias_b, tm=256):
    M, D = hb.shape
    return pl.pallas_call(
        _router_kernel,
        out_shape=jax.ShapeDtypeStruct((M, 2 * N_EXPERTS), F32),
        grid=(M // tm,),
        in_specs=[pl.BlockSpec((tm, D), lambda i: (i, 0)),
                  pl.BlockSpec((N_EXPERTS, D), lambda i: (0, 0)),
                  pl.BlockSpec((N_EXPERTS, 128), lambda i: (0, 0))],
        out_specs=pl.BlockSpec((tm, 2 * N_EXPERTS), lambda i: (i, 0)),
        compiler_params=_cparams(("parallel",)),
        name="moe_router",
    )(hb, w_router_t, bias_b)


MOE_F = 6656


def _moe_act_kernel(x_ref, gate_ref, e_ref, w1_ref, w3_ref, o_ref):
    x = x_ref[...]
    h1 = jnp.dot(x, w1_ref[...], preferred_element_type=F32)
    h3 = jnp.dot(x, w3_ref[...], preferred_element_type=F32)
    g0, g1, g2 = _split3(gate_ref[...])
    e = e_ref[...]
    gexp = (jnp.dot(g0, e, preferred_element_type=F32) + jnp.dot(g1, e, preferred_element_type=F32)
            + jnp.dot(g2, e, preferred_element_type=F32))
    o_ref[...] = (h1 * jax.nn.sigmoid(h1) * h3 * gexp).astype(BF16)


def moe_hidden(hb, gate, expand, w1c, w3c, tm=1024, tn=512):
    M, D = hb.shape
    wspec = pl.BlockSpec((D, tn), lambda i, j: (0, j))
    return pl.pallas_call(
        _moe_act_kernel,
        out_shape=jax.ShapeDtypeStruct((M, MOE_F), BF16),
        grid=(M // tm, MOE_F // tn),
        in_specs=[pl.BlockSpec((tm, D), lambda i, j: (i, 0)),
                  pl.BlockSpec((tm, 2 * N_EXPERTS), lambda i, j: (i, 0)),
                  pl.BlockSpec((2 * N_EXPERTS, tn), lambda i, j: (0, j)),
                  wspec, wspec],
        out_specs=pl.BlockSpec((tm, tn), lambda i, j: (i, j)),
        compiler_params=_cparams(("parallel", "arbitrary")),
        name="moe_hidden",
    )(hb, gate, expand, w1c, w3c)


def _regroup_w_in(w, has_vres):
    D = w.shape[0]
    o = np.cumsum((0, W_A, R_KV, H_IDX * D_IDX, D_IDX, H_IDX, 3 * W_B, W_LORA, A_LORA, G_LORA, W_M,
                   3 * D_MODEL)).tolist()
    q_a, c_kv, q_idx = w[:, o[0]:o[1]], w[:, o[1]:o[2]], w[:, o[2]:o[3]]
    idx = w[:, o[3]:o[5]]
    rkv, lora = w[:, o[5]:o[6]], w[:, o[6]:o[9]]
    q_m, gates = w[:, o[9]:o[10]], w[:, o[10]:o[11]]
    zeros = lambda n: jnp.zeros((D, n), w.dtype)
    vres = w[:, o[11]:o[11] + V_LORA] if has_vres else zeros(V_LORA)
    n_lora = W_LORA + A_LORA + G_LORA + V_LORA
    parts = [gates, q_a, q_idx, rkv, q_m, c_kv, lora, vres, zeros(LORA_W - n_lora),
             idx, zeros(IDX_W - D_IDX - H_IDX)]
    used = sum(p.shape[1] for p in parts)
    parts.append(zeros(N_PROJ - used))
    return jnp.concatenate(parts, axis=1).astype(BF16)


def _lora_up_matrix(w_up, a_up, g_up, v_up):
    m = jnp.zeros((LORA_W, 4 * W_B), F32)
    m = m.at[0:W_LORA, 0:W_B].set(w_up)
    m = m.at[W_LORA:W_LORA + A_LORA, W_B:2 * W_B].set(a_up)
    m = m.at[W_LORA + A_LORA:W_LORA + A_LORA + G_LORA, 2 * W_B:3 * W_B].set(g_up)
    if v_up is not None:
        r0 = W_LORA + A_LORA + G_LORA
        m = m.at[r0:r0 + V_LORA, 3 * W_B:4 * W_B].set(v_up)
    return m.astype(BF16)


def _rows(vectors, n_rows, width):
    m = jnp.zeros((n_rows, width), F32)
    for i, vec in enumerate(vectors):
        m = m.at[i, :vec.shape[0]].set(vec)
    return m


def _expert_in_weights(w_e, w_s):
    D = w_e.shape[1]
    routed = jnp.transpose(w_e, (1, 0, 2)).reshape(D, N_EXPERTS * F_EXPERT)
    pad = jnp.zeros((D, MOE_F - N_EXPERTS * F_EXPERT - w_s.shape[1]), w_e.dtype)
    return jnp.concatenate([routed, w_s, pad], axis=1).astype(BF16)


def _expert_out_weights(w_e2, w_s2):
    D = w_e2.shape[2]
    routed = w_e2.reshape(N_EXPERTS * F_EXPERT, D)
    pad = jnp.zeros((MOE_F - N_EXPERTS * F_EXPERT - w_s2.shape[0], D), w_e2.dtype)
    return jnp.concatenate([routed, w_s2, pad], axis=0).astype(BF16)


def kernel(x, mem, ln0_g, ln0_b, w_in_first, w_in_rest, ckv_g, w_uk, w_uv, mu_rwkv, mu_vres, rw_w0, rw_w_up, rw_a0, rw_a_up, rw_g_up, rw_v0, rw_v_up, rw_k_k, rw_k_a, rw_r_k, rw_lnx_g, rw_lnx_b, w_mk, w_mv, w_pa, w_pb, w_pc, w_o, ln1_g, ln1_b, w_router, router_bias, w_e1, w_e3, w_e2, w_s1, w_s3, w_s2, ln2_g, ln2_b):
    B, S, D = x.shape
    T = B * S
    gmat = jnp.kron(jnp.eye(H_B, dtype=F32), jnp.ones((N_B, N_B), F32)).astype(BF16)
    unit = np.arange(MOE_F)
    owner = np.where(unit < (N_EXPERTS + 1) * F_EXPERT, unit // F_EXPERT, -1)
    expand = jnp.asarray(owner[None, :] == np.arange(2 * N_EXPERTS)[:, None], BF16)
    mem_b = mem.reshape(B * N_MEM, D).astype(BF16)

    h, hb = layer_norm(x.reshape(T, D), ln0_g, ln0_b)
    v_first = None
    for i in range(DEPTH):
        has_vres = i > 0
        w_in = _regroup_w_in(w_in_first if i == 0 else w_in_rest[i - 1], has_vres)
        proj2 = matmul(hb, w_in, F32, 1024, 1024)
        proj3 = proj2.reshape(B, S, N_PROJ)

        k2, v2 = kv_project(proj2, ckv_g[i], w_uk[i].astype(BF16), w_uv[i].astype(BF16))
        bias5 = indexer_bias(proj3)
        o_a = sparse_attention(proj3, k2.reshape(B, S, W_A), v2.reshape(B, S, W_A), bias5)

        mu = mu_rwkv[i]
        mu_rkv = mu[:3 * W_B].reshape(3, W_B)
        mu_l = [mu[3 * W_B:]] + ([mu_vres[i - 1]] if has_vres else [])
        mu_lora = _rows([jnp.concatenate(mu_l)], 1, LORA_W)
        prep_par = _rows([rw_w0[i], rw_a0[i], rw_v0[i - 1] if has_vres else jnp.zeros((W_B,), F32),
                          rw_k_k[i], rw_k_a[i]], 8, W_B)
        w_lora = _lora_up_matrix(rw_w_up[i], rw_a_up[i], rw_g_up[i],
                                 rw_v_up[i - 1] if has_vres else None)
        r_b, k_b, v_b, g_b, gl_b, *scan_ops = rwkv_prep(proj3, v_first, mu_rkv, mu_lora, prep_par,
                                                        w_lora, gmat)
        if i == 0:
            v_first = v_b
        o_scan = rwkv_scan(*scan_ops, v_b, gl_b)
        post_par = _rows([rw_lnx_g[i], rw_lnx_b[i], rw_r_k[i].reshape(W_B)], 8, W_B)
        flat = lambda z: z.reshape(T, W_B)
        o_b = rwkv_post(flat(o_scan), flat(r_b), flat(k_b), flat(v_b), flat(g_b), post_par, gmat)

        km = matmul(mem_b, w_mk[i].astype(BF16), BF16, 512, 512).reshape(B, N_MEM, W_M)
        vm = matmul(mem_b, w_mv[i].astype(BF16), BF16, 512, 512).reshape(B, N_MEM, W_M)
        o_c = memory_attention(proj3, km, vm)

        merged = gated_merge(o_a.reshape(T, W_A), o_b, o_c.reshape(T, W_M), w_pa[i].astype(BF16),
                             w_pb[i].astype(BF16), w_pc[i].astype(BF16), proj2)
        h, hb = matmul_residual_ln(merged, w_o[i].astype(BF16), h, ln1_g[i], ln1_b[i])

        bias_b = jnp.broadcast_to(router_bias[i].reshape(N_EXPERTS, 1), (N_EXPERTS, 128))
        gate = moe_router(hb, w_router[i].T.astype(BF16), bias_b)
        hidden = moe_hidden(hb, gate, expand, _expert_in_weights(w_e1[i], w_s1[i]),
                            _expert_in_weights(w_e3[i], w_s3[i]))
        h, hb = matmul_residual_ln(hidden, _expert_out_weights(w_e2[i], w_s2[i]), h,
                                   ln2_g[i], ln2_b[i])
    return h.reshape(B, S, D)
```

```python
import functools

import jax
import jax.numpy as jnp
import numpy as np
from jax import lax
from jax.experimental import pallas as pl
from jax.experimental.pallas import tpu as pltpu

F32 = jnp.float32
BF16 = jnp.bfloat16

D_MODEL = 4096
DEPTH = 4
CHUNK = 64
H_A, DH_A = 16, 128
W_A = H_A * DH_A
R_KV = 256
H_IDX, D_IDX = 32, 64
TOPK_MAX = 256
H_B, N_B = 16, 64
W_B = H_B * N_B
W_LORA, A_LORA, G_LORA, V_LORA = 64, 64, 160, 32
LNX_EPS = 64e-5
N_MEM = 256
H_M, DH_M = 4, 256
W_M = H_M * DH_M
N_EXPERTS, N_GROUPS, TOPK_GROUPS, TOP_K = 64, 8, 4, 8
F_EXPERT = 96
ROUTED_SCALE = 2.5
ALPHA = (2 * DEPTH) ** 0.25
LN_EPS = 1e-5

OFF_GATE = 0
OFF_QA = 3 * D_MODEL
OFF_QIDX = OFF_QA + W_A
OFF_RKV = OFF_QIDX + H_IDX * D_IDX
OFF_QM = OFF_RKV + 3 * W_B
OFF_CKV = OFF_QM + W_M
OFF_LORA = OFF_CKV + R_KV
LORA_W = 384
OFF_IDX = OFF_LORA + LORA_W
IDX_W = 128
N_PROJ = 21504

NEG_BIAS = -1e30
VMEM_LIMIT = 56 * 1024 * 1024


def _cparams(sem):
    return pltpu.CompilerParams(dimension_semantics=sem, vmem_limit_bytes=VMEM_LIMIT)


def _split2(x):
    hi = x.astype(BF16)
    lo = (x - hi.astype(F32)).astype(BF16)
    return hi, lo


def _split3(x):
    hi = x.astype(BF16)
    r1 = x - hi.astype(F32)
    mid = r1.astype(BF16)
    lo = (r1 - mid.astype(F32)).astype(BF16)
    return hi, mid, lo


def _mm_kernel(x_ref, w_ref, o_ref):
    o_ref[...] = jnp.dot(x_ref[...], w_ref[...], preferred_element_type=F32).astype(o_ref.dtype)


def matmul(x, w, out_dtype, tm, tn):
    M, K = x.shape
    N = w.shape[1]
    tm, tn = min(tm, M), min(tn, N)
    assert M % tm == 0 and N % tn == 0
    return pl.pallas_call(
        _mm_kernel,
        out_shape=jax.ShapeDtypeStruct((M, N), out_dtype),
        grid=(M // tm, N // tn),
        in_specs=[pl.BlockSpec((tm, K), lambda i, j: (i, 0)),
                  pl.BlockSpec((K, tn), lambda i, j: (0, j))],
        out_specs=pl.BlockSpec((tm, tn), lambda i, j: (i, j)),
        compiler_params=_cparams(("parallel", "arbitrary")),
        name="matmul",
    )(x, w)


def _ln_rows(y, g, b):
    mu = jnp.mean(y, axis=-1, keepdims=True)
    yc = y - mu
    var = jnp.mean(yc * yc, axis=-1, keepdims=True)
    return yc * lax.rsqrt(var + LN_EPS) * g + b


def _ln_kernel(x_ref, g_ref, b_ref, o_ref, ob_ref):
    y = _ln_rows(x_ref[...], g_ref[...], b_ref[...])
    o_ref[...] = y
    ob_ref[...] = y.astype(BF16)


def _res_ln_kernel(h_ref, f_ref, g_ref, b_ref, o_ref, ob_ref):
    y = _ln_rows(ALPHA * h_ref[...] + f_ref[...], g_ref[...], b_ref[...])
    o_ref[...] = y
    ob_ref[...] = y.astype(BF16)


def layer_norm(x, g, b, tm=256):
    M, D = x.shape
    row = pl.BlockSpec((tm, D), lambda i: (i, 0))
    vec = pl.BlockSpec((1, D), lambda i: (0, 0))
    return pl.pallas_call(
        _ln_kernel,
        out_shape=(jax.ShapeDtypeStruct((M, D), F32), jax.ShapeDtypeStruct((M, D), BF16)),
        grid=(M // tm,),
        in_specs=[row, vec, vec],
        out_specs=(row, row),
        compiler_params=_cparams(("parallel",)),
        name="layer_norm",
    )(x, g.reshape(1, D), b.reshape(1, D))


def residual_layer_norm(h, f, g, b, tm=256):
    M, D = h.shape
    row = pl.BlockSpec((tm, D), lambda i: (i, 0))
    vec = pl.BlockSpec((1, D), lambda i: (0, 0))
    return pl.pallas_call(
        _res_ln_kernel,
        out_shape=(jax.ShapeDtypeStruct((M, D), F32), jax.ShapeDtypeStruct((M, D), BF16)),
        grid=(M // tm,),
        in_specs=[row, row, vec, vec],
        out_specs=(row, row),
        compiler_params=_cparams(("parallel",)),
        name="residual_layer_norm",
    )(h, f, g.reshape(1, D), b.reshape(1, D))


def _mm_res_ln_kernel(x_ref, w_ref, h_ref, g_ref, b_ref, o_ref, ob_ref, *, nj, tn):
    j = pl.program_id(1)
    y = jnp.dot(x_ref[...], w_ref[...], preferred_element_type=F32)
    for jj in range(nj):
        @pl.when(j == jj)
        def _(jj=jj):
            o_ref[:, jj * tn:(jj + 1) * tn] = y

    @pl.when(j == nj - 1)
    def _():
        D = nj * tn
        s1 = None
        for jj in range(nj):
            sl = slice(jj * tn, (jj + 1) * tn)
            z = ALPHA * h_ref[:, sl] + o_ref[:, sl]
            o_ref[:, sl] = z
            p = jnp.sum(z, axis=-1, keepdims=True)
            s1 = p if s1 is None else s1 + p
        mu = s1 * (1.0 / D)
        s2 = None
        for jj in range(nj):
            zc = o_ref[:, jj * tn:(jj + 1) * tn] - mu
            p = jnp.sum(zc * zc, axis=-1, keepdims=True)
            s2 = p if s2 is None else s2 + p
        rstd = lax.rsqrt(s2 * (1.0 / D) + LN_EPS)
        for jj in range(nj):
            sl = slice(jj * tn, (jj + 1) * tn)
            z = (o_ref[:, sl] - mu) * rstd * g_ref[:, sl] + b_ref[:, sl]
            o_ref[:, sl] = z
            ob_ref[:, sl] = z.astype(BF16)


def matmul_residual_ln(x, w, h, g, b, tm=512, tn=512):
    M, K = x.shape
    D = w.shape[1]
    nj = D // tn
    row = pl.BlockSpec((tm, D), lambda i, j: (i, 0), pipeline_mode=pl.Buffered(1))
    vec = pl.BlockSpec((1, D), lambda i, j: (0, 0))
    return pl.pallas_call(
        functools.partial(_mm_res_ln_kernel, nj=nj, tn=tn),
        out_shape=(jax.ShapeDtypeStruct((M, D), F32), jax.ShapeDtypeStruct((M, D), BF16)),
        grid=(M // tm, nj),
        in_specs=[pl.BlockSpec((tm, K), lambda i, j: (i, 0), pipeline_mode=pl.Buffered(1)),
                  pl.BlockSpec((K, tn), lambda i, j: (0, j)),
                  row, vec, vec],
        out_specs=(row, row),
        compiler_params=_cparams(("parallel", "arbitrary")),
        name="matmul_residual_ln",
    )(x, w, h, g.reshape(1, D), b.reshape(1, D))


def _kv_kernel(c_ref, g_ref, wk_ref, wvt_ref, k_ref, vt_ref):
    c = c_ref[...]
    ms = jnp.mean(c * c, axis=-1, keepdims=True)
    cn = (c * lax.rsqrt(ms + LN_EPS) * g_ref[...]).astype(BF16)
    k_ref[...] = jnp.dot(cn, wk_ref[...], preferred_element_type=F32).astype(BF16)
    vt_ref[...] = lax.dot_general(wvt_ref[...], cn, (((1,), (1,)), ((), ())),
                                  preferred_element_type=F32).astype(BF16)


def kv_project(proj2, ckv_g, w_uk, w_uv_t):
    M = proj2.shape[0]
    tm = ATT_TK
    return pl.pallas_call(
        _kv_kernel,
        out_shape=(jax.ShapeDtypeStruct((M, W_A), BF16),
                   jax.ShapeDtypeStruct((M // tm, W_A, tm), BF16)),
        grid=(M // tm,),
        in_specs=[pl.BlockSpec((tm, R_KV), lambda i: (i, OFF_CKV // R_KV)),
                  pl.BlockSpec((1, R_KV), lambda i: (0, 0)),
                  pl.BlockSpec((R_KV, W_A), lambda i: (0, 0)),
                  pl.BlockSpec((W_A, R_KV), lambda i: (0, 0))],
        out_specs=(pl.BlockSpec((tm, W_A), lambda i: (i, 0)),
                   pl.BlockSpec((None, W_A, tm), lambda i: (i, 0, 0))),
        compiler_params=_cparams(("parallel",)),
        name="kv_project",
    )(proj2, ckv_g.reshape(1, R_KV), w_uk, w_uv_t)


IDX_TQ = 128
IDX_TK = 256
KEY_NEG_INF = -2139095041
INT_MIN = -2147483648


def _indexer_kernel(q_ref, iq_ref, ik_ref, below_ref, bias_ref, qt_ref, wt_ref, key_ref, *, n_kt,
                    ksel):
    tq, tk = IDX_TQ, IDX_TK
    i = pl.program_id(1)
    qt_ref[...] = (q_ref[...].T * (D_IDX ** -0.5)).astype(BF16)
    wt_ref[...] = iq_ref[...].T * (H_IDX ** -0.5)

    n_t = ((i + 1) * tq + tk - 1) // tk
    q_chunk = (i * tq + lax.broadcasted_iota(jnp.int32, (tk, tq), 1)) // CHUNK
    row_iota = lax.broadcasted_iota(jnp.int32, (tk, tq), 0)

    def score_tile(j, carry):
        kt = ik_ref[pl.ds(pl.multiple_of(j * tk, tk), tk), :][:, :D_IDX].astype(BF16)
        acc = jnp.zeros((tk, tq), F32)
        for h2 in range(H_IDX // 2):
            w2 = jnp.concatenate([qt_ref[(2 * h2 + t) * D_IDX:(2 * h2 + t + 1) * D_IDX, :]
                                  for t in range(2)], axis=1)
            z = jnp.dot(kt, w2, preferred_element_type=F32)
            for t in range(2):
                h = 2 * h2 + t
                acc = acc + wt_ref[D_IDX + h:D_IDX + h + 1, :] * jnp.maximum(
                    z[:, t * tq:(t + 1) * tq], 0.0)
        admissible = ((j * tk + row_iota) // CHUNK) <= q_chunk
        score = jnp.where(admissible, acc, -jnp.inf)
        bits = lax.bitcast_convert_type(score, jnp.int32)
        key_ref[j] = bits ^ ((bits >> 31) & 0x7FFFFFFF)
        return carry

    lax.fori_loop(0, n_t, score_tile, 0)

    def count_keys(pred):
        def count_tile(j, cnts):
            c = pred(key_ref[j]).astype(jnp.int32)
            cnts = list(cnts)
            for r in range(tk // 8):
                cnts[r % len(cnts)] = cnts[r % len(cnts)] + c[r * 8:(r + 1) * 8, :]
            return tuple(cnts)

        cnts = lax.fori_loop(0, n_t, count_tile, (jnp.zeros((8, tq), jnp.int32),) * 4)
        cnt = (cnts[0] + cnts[1]) + (cnts[2] + cnts[3])
        return jnp.sum(cnt.astype(F32), axis=0, keepdims=True)

    def bit_step(it, thr):
        cand = thr + lax.shift_left(jnp.int32(1), 31 - it)
        return jnp.where(count_keys(lambda key: key >= cand) >= float(ksel), cand, thr)

    thr = lax.fori_loop(0, 32, bit_step, jnp.full((1, tq), INT_MIN, jnp.int32))
    ties_needed = float(ksel) - count_keys(lambda key: key > thr)
    below = below_ref[...]

    def bias_tile(j, ties_seen):
        key = key_ref[j]
        tied = key == thr
        tied_b = jnp.where(tied, 1.0, 0.0).astype(BF16)
        rank = ties_seen + jnp.dot(below, tied_b, preferred_element_type=F32)
        sel = ((key > thr) | (tied & (rank < ties_needed))) & (key > KEY_NEG_INF)
        bias_ref[j] = jnp.where(sel, 0.0, NEG_BIAS).astype(BF16)
        return ties_seen + jnp.sum(tied_b.astype(F32), axis=0, keepdims=True)

    def bias_pair(jj, ties_seen):
        return bias_tile(2 * jj + 1, bias_tile(2 * jj, ties_seen))

    ties_seen = lax.fori_loop(0, n_t // 2, bias_pair, jnp.zeros((1, tq), F32))

    @pl.when(n_t % 2 == 1)
    def _():
        bias_tile(n_t - 1, ties_seen)

    def fill_tile(j, carry):
        bias_ref[j] = jnp.full((tk, tq), NEG_BIAS, BF16)
        return carry

    lax.fori_loop(n_t, n_kt, fill_tile, 0)


def indexer_bias(proj3):
    B, S, _ = proj3.shape
    tq, tk = IDX_TQ, IDX_TK
    n_qt, n_kt = S // tq, S // tk
    ksel = min(TOPK_MAX, S // 4)
    below = jnp.asarray(np.tril(np.ones((tk, tk)), -1), BF16)
    return pl.pallas_call(
        functools.partial(_indexer_kernel, n_kt=n_kt, ksel=ksel),
        out_shape=jax.ShapeDtypeStruct((B, n_qt, n_kt, tk, tq), BF16),
        grid=(B, n_qt),
        in_specs=[pl.BlockSpec((None, tq, H_IDX * D_IDX), lambda b, i: (b, i, OFF_QIDX // (H_IDX * D_IDX))),
                  pl.BlockSpec((None, tq, IDX_W), lambda b, i: (b, i, OFF_IDX // IDX_W)),
                  pl.BlockSpec((None, S, IDX_W), lambda b, i: (b, 0, OFF_IDX // IDX_W)),
                  pl.BlockSpec((tk, tk), lambda b, i: (0, 0))],
        out_specs=pl.BlockSpec((None, None, n_kt, tk, tq), lambda b, i: (b, i, 0, 0, 0)),
        scratch_shapes=[pltpu.VMEM((H_IDX * D_IDX, tq), BF16),
                        pltpu.VMEM((IDX_W, tq), F32),
                        pltpu.VMEM((n_kt, tk, tq), jnp.int32)],
        compiler_params=_cparams(("parallel", "parallel")),
        name="indexer_bias",
    )(proj3, proj3, proj3, below)


ATT_TQ = 256
ATT_HP = 2
ATT_TK = 512
ATT_RC = 64


def _fold_rows(x, op):
    parts = [x[r:r + 8, :] for r in range(0, x.shape[0], 8)]
    while len(parts) > 1:
        parts = [op(parts[a], parts[a + 1]) for a in range(0, len(parts), 2)]
    return parts[0]


def _attention_kernel(q_ref, k_ref, vt_ref, bias_ref, o_ref, qt_ref, s_ref, p_ref, m_ref, l_ref,
                      a_ref, acc_ref):
    tq, tk, hp, rc = ATT_TQ, ATT_TK, ATT_HP, ATT_RC
    nq = tq // IDX_TQ
    nb = tk // IDX_TK
    i = pl.program_id(2)
    scale = DH_A ** -0.5 * np.log2(np.e)
    cols = [slice(hh * DH_A, (hh + 1) * DH_A) for hh in range(hp)]
    lanes = [slice(qq * IDX_TQ, (qq + 1) * IDX_TQ) for qq in range(nq)]
    for hh in range(hp):
        qt_ref[hh] = (q_ref[:, cols[hh]] * scale).T.astype(BF16)
    m_ref[...] = jnp.full(m_ref.shape, NEG_BIAS, F32)
    l_ref[...] = jnp.zeros_like(l_ref)
    acc_ref[...] = jnp.zeros_like(acc_ref)
    p_ref[1] = jnp.zeros(p_ref.shape[1:], BF16)
    a_ref[1] = jnp.zeros(a_ref.shape[1:], F32)
    n_kv = k_ref.shape[0] // tk
    n_t = ((i + 1) * tq + tk - 1) // tk

    def logits(j, slot):
        row0 = pl.multiple_of(j * tk, tk)
        for hh in range(hp):
            s_ref[slot, hh] = jnp.dot(k_ref[pl.ds(row0, tk), cols[hh]], qt_ref[hh],
                                      preferred_element_type=F32)

    def weighted_values(j, slot):
        return [jnp.dot(vt_ref[j, cols[hh], :], p_ref[slot, hh], preferred_element_type=F32)
                for hh in range(hp)]

    def accumulate(pv, slot):
        for hh in range(hp):
            acc_ref[hh] = a_ref[slot, hh] * acc_ref[hh] + pv[hh]

    def stage(j, slot):
        other = 1 - slot
        logits(jnp.minimum(j + 1, n_kv - 1), other)
        pv = weighted_values(jnp.maximum(j - 1, 0), other)
        for hh in range(hp):
            for qq in range(nq):
                ln = lanes[qq]
                top = None
                for r in range(0, tk, rc):
                    bias = bias_ref[qq, nb * j + r // IDX_TK, r % IDX_TK:r % IDX_TK + rc, :]
                    sc = s_ref[slot, hh, r:r + rc, ln] + bias.astype(F32)
                    s_ref[slot, hh, r:r + rc, ln] = sc
                    part = _fold_rows(sc, jnp.maximum)
                    top = part if top is None else jnp.maximum(top, part)
                m_old = m_ref[hh, :, ln]
                m_new = jnp.maximum(m_old, jnp.max(top, axis=0, keepdims=True))
                rescale = jnp.exp2(m_old - m_new)
                a_ref[slot, hh, :, ln] = rescale
                m_ref[hh, :, ln] = m_new
                total = None
                for r in range(0, tk, rc):
                    p = jnp.exp2(s_ref[slot, hh, r:r + rc, ln] - m_new)
                    p_ref[slot, hh, r:r + rc, ln] = p.astype(BF16)
                    part = _fold_rows(p, jnp.add)
                    total = part if total is None else total + part
                l_ref[hh, :, ln] = rescale * l_ref[hh, :, ln] + jnp.sum(total, axis=0, keepdims=True)
        accumulate(pv, other)

    logits(0, 0)

    def body(jj, carry):
        stage(2 * jj, 0)
        stage(2 * jj + 1, 1)
        return carry

    n_pairs = (n_t + 1) // 2
    lax.fori_loop(0, n_pairs, body, 0)
    accumulate(weighted_values(2 * n_pairs - 1, 1), 1)
    for hh in range(hp):
        o_ref[:, cols[hh]] = (acc_ref[hh] / l_ref[hh]).T.astype(BF16)


def sparse_attention(proj3, k3, vt4, bias5):
    B, S, _ = proj3.shape
    tq, tk, hp = ATT_TQ, ATT_TK, ATT_HP
    wb = hp * DH_A
    n_kt = S // IDX_TK
    n_kv = S // tk
    assert S % (2 * tk) == 0
    return pl.pallas_call(
        _attention_kernel,
        out_shape=jax.ShapeDtypeStruct((B, S, W_A), BF16),
        grid=(B, H_A // hp, S // tq),
        in_specs=[pl.BlockSpec((None, tq, wb), lambda b, h, i: (b, i, OFF_QA // wb + h)),
                  pl.BlockSpec((None, S, wb), lambda b, h, i: (b, 0, h)),
                  pl.BlockSpec((None, n_kv, wb, tk), lambda b, h, i: (b, 0, h, 0)),
                  pl.BlockSpec((None, tq // IDX_TQ, n_kt, IDX_TK, IDX_TQ),
                               lambda b, h, i: (b, i, 0, 0, 0))],
        out_specs=pl.BlockSpec((None, tq, wb), lambda b, h, i: (b, i, h)),
        scratch_shapes=[pltpu.VMEM((hp, DH_A, tq), BF16),
                        pltpu.VMEM((2, hp, tk, tq), F32),
                        pltpu.VMEM((2, hp, tk, tq), BF16),
                        pltpu.VMEM((hp, 1, tq), F32),
                        pltpu.VMEM((hp, 1, tq), F32),
                        pltpu.VMEM((2, hp, 1, tq), F32),
                        pltpu.VMEM((hp, DH_A, tq), F32)],
        compiler_params=_cparams(("parallel", "parallel", "parallel")),
        name="sparse_attention",
    )(proj3, k3, vt4, bias5)


def _mem_attention_kernel(q_ref, km_ref, vm_ref, o_ref):
    scale = DH_M ** -0.5
    for h in range(H_M):
        sl = slice(h * DH_M, (h + 1) * DH_M)
        q = (q_ref[:, sl] * scale).astype(BF16)
        s = lax.dot_general(q, km_ref[:, sl], (((1,), (1,)), ((), ())), preferred_element_type=F32)
        m = jnp.max(s, axis=-1, keepdims=True)
        p = jnp.exp(s - m)
        p = p / jnp.sum(p, axis=-1, keepdims=True)
        o_ref[:, sl] = jnp.dot(p.astype(BF16), vm_ref[:, sl], preferred_element_type=F32).astype(BF16)


def memory_attention(proj3, km3, vm3, tm=512):
    B, S, _ = proj3.shape
    mspec = pl.BlockSpec((None, N_MEM, W_M), lambda b, i: (b, 0, 0))
    return pl.pallas_call(
        _mem_attention_kernel,
        out_shape=jax.ShapeDtypeStruct((B, S, W_M), BF16),
        grid=(B, S // tm),
        in_specs=[pl.BlockSpec((None, tm, W_M), lambda b, i: (b, i, OFF_QM // W_M)), mspec, mspec],
        out_specs=pl.BlockSpec((None, tm, W_M), lambda b, i: (b, i, 0)),
        compiler_params=_cparams(("parallel", "parallel")),
        name="memory_attention",
    )(proj3, km3, vm3)


def _group_sum(x, g_ref):
    hi, lo = _split2(x)
    return (jnp.dot(hi, g_ref[...], preferred_element_type=F32)
            + jnp.dot(lo, g_ref[...], preferred_element_type=F32))


def _rwkv_prep_kernel(*refs, has_vres):
    refs = list(refs)
    r_ref, k_ref, v_ref, l_ref = refs[:4]
    vf_ref = refs[4] if has_vres else None
    (mu_ref, mul_ref, par_ref, wl_ref, g_ref, tri_ref, same_ref,
     ro_ref, ko_ref, vo_ref, go_ref, gl_ref, at_ref, rt_ref, bt_ref, kt_ref, bt2_ref, kt2_ref,
     prev_ref, prevl_ref) = refs[4 + int(has_vres):]
    s_idx = pl.program_id(1)

    @pl.when(s_idx == 0)
    def _():
        prev_ref[...] = jnp.zeros_like(prev_ref)
        prevl_ref[...] = jnp.zeros_like(prevl_ref)

    def shift(x, carry_row, mu):
        tm = x.shape[0]
        rolled = pltpu.roll(x, shift=1, axis=0)
        first = lax.broadcasted_iota(jnp.int32, x.shape, 0) == 0
        prev = jnp.where(first, jnp.broadcast_to(carry_row, x.shape), rolled)
        return x + (prev - x) * mu, x[tm - 1:tm, :]

    r, r_last = shift(r_ref[...], prev_ref[0:1, :], mu_ref[0:1, :])
    k, k_last = shift(k_ref[...], prev_ref[1:2, :], mu_ref[1:2, :])
    v, v_last = shift(v_ref[...], prev_ref[2:3, :], mu_ref[2:3, :])
    lo, l_last = shift(l_ref[...], prevl_ref[0:1, :], mul_ref[...])
    prev_ref[0:1, :] = r_last
    prev_ref[1:2, :] = k_last
    prev_ref[2:3, :] = v_last
    prevl_ref[0:1, :] = l_last

    lane = lax.broadcasted_iota(jnp.int32, lo.shape, 1)
    lt = jnp.where(lane < W_LORA, jnp.tanh(lo),
                   jnp.where((lane >= W_LORA + A_LORA) & (lane < W_LORA + A_LORA + G_LORA),
                             jax.nn.sigmoid(lo), lo))
    up = jnp.dot(lt.astype(BF16), wl_ref[...], preferred_element_type=F32)
    w0, a0, v0 = par_ref[0:1, :], par_ref[1:2, :], par_ref[2:3, :]
    k_k, k_a = par_ref[3:4, :], par_ref[4:5, :]

    y = -(w0 + up[:, 0:W_B])
    softplus = jnp.maximum(y, 0.0) + jnp.log(1.0 + jnp.exp(-jnp.abs(y)))
    w_log = -softplus - 0.5
    lw = -jnp.exp(w_log)
    a = jax.nn.sigmoid(a0 + up[:, W_B:2 * W_B])
    go_ref[...] = up[:, 2 * W_B:3 * W_B]
    if has_vres:
        v = v + (vf_ref[...] - v) * jax.nn.sigmoid(v0 + up[:, 3 * W_B:4 * W_B])
    kk = k * k_k
    norm = jnp.sqrt(_group_sum(kk * kk, g_ref))
    kk = kk / jnp.maximum(norm, 1e-12)
    k = k * (1.0 + (a - 1.0) * k_a)
    ro_ref[...] = r
    ko_ref[...] = k
    vo_ref[...] = v

    p0, p1, p2 = _split3(lw)
    tri, same = tri_ref[...], same_ref[...]
    cs = (jnp.dot(tri, p0, preferred_element_type=F32) + jnp.dot(tri, p1, preferred_element_type=F32)
          + jnp.dot(tri, p2, preferred_element_type=F32))
    cl = (jnp.dot(same, p0, preferred_element_type=F32) + jnp.dot(same, p1, preferred_element_type=F32)
          + jnp.dot(same, p2, preferred_element_type=F32))
    e_neg = jnp.exp(-cs)
    to_end = jnp.exp(cl - cs)
    kka = kk * a
    at_ref[...] = (-kk * jnp.exp(cs - lw)).astype(BF16)
    rt_ref[...] = (r * jnp.exp(cs)).astype(BF16)
    bt_ref[...] = (kka * e_neg).astype(BF16)
    kt_ref[...] = (k * e_neg).astype(BF16)
    bt2_ref[...] = (kka * to_end).astype(BF16)
    kt2_ref[...] = (k * to_end).astype(BF16)
    gl_ref[...] = jnp.exp(cl)


def _chunk_matrices(tm):
    t = np.arange(tm)
    same = (t[:, None] // CHUNK) == (t[None, :] // CHUNK)
    tri = same & (t[None, :] <= t[:, None])
    return jnp.asarray(tri, BF16), jnp.asarray(same, BF16)


def rwkv_prep(proj3, v_first, mu_rkv, mu_lora, params, w_lora, gmat, tm=256):
    B, S, _ = proj3.shape
    has_vres = v_first is not None
    tok = lambda c: pl.BlockSpec((None, tm, W_B), lambda b, s: (b, s, c))
    full = lambda shp: pl.BlockSpec(shp, lambda b, s: (0,) * len(shp))
    in_specs = [tok(OFF_RKV // W_B), tok(OFF_RKV // W_B + 1), tok(OFF_RKV // W_B + 2),
                pl.BlockSpec((None, tm, LORA_W), lambda b, s: (b, s, OFF_LORA // LORA_W))]
    args = [proj3, proj3, proj3, proj3]
    if has_vres:
        in_specs.append(tok(0))
        args.append(v_first)
    in_specs += [full((3, W_B)), full((1, LORA_W)), full((8, W_B)), full((LORA_W, 4 * W_B)),
                 full((W_B, W_B)), full((tm, tm)), full((tm, tm))]
    args += [mu_rkv, mu_lora, params, w_lora, gmat, *_chunk_matrices(tm)]
    out32 = jax.ShapeDtypeStruct((B, S, W_B), F32)
    out16 = jax.ShapeDtypeStruct((B, S, W_B), BF16)
    return pl.pallas_call(
        functools.partial(_rwkv_prep_kernel, has_vres=has_vres),
        out_shape=(out32,) * 5 + (out16,) * 6,
        grid=(B, S // tm),
        in_specs=in_specs,
        out_specs=(tok(0),) * 11,
        scratch_shapes=[pltpu.VMEM((8, W_B), F32), pltpu.VMEM((8, LORA_W), F32)],
        compiler_params=_cparams(("parallel", "arbitrary")),
        name="rwkv_prep",
    )(*args)


SCAN_HPG = 16


def _rwkv_scan_kernel(at_ref, rt_ref, bt_ref, kt_ref, bt2_ref, kt2_ref, v_ref, gl_ref, o_ref,
                      state_ref):
    C, N, H = CHUNK, N_B, SCAN_HPG

    @pl.when(pl.program_id(2) == 0)
    def _():
        state_ref[...] = jnp.zeros_like(state_ref)

    row = lax.broadcasted_iota(jnp.int32, (C, C), 0)
    col = lax.broadcasted_iota(jnp.int32, (C, C), 1)
    incl = (col <= row).astype(F32)
    strict = (col < row).astype(F32)
    eye = (col == row).astype(F32)
    nt = (((1,), (1,)), ((), ()))
    dot = functools.partial(jnp.dot, preferred_element_type=F32)
    dot_nt = lambda x, y: lax.dot_general(x, y, nt, preferred_element_type=F32)
    hs = range(H)
    sl = [slice(h * N, (h + 1) * N) for h in hs]

    x = [jnp.concatenate([at_ref[:, sl[h]], rt_ref[:, sl[h]]], axis=0) for h in hs]
    v = [v_ref[:, sl[h]] for h in hs]
    vb = [v[h].astype(BF16) for h in hs]
    st = [state_ref[h] for h in hs]
    mb = [dot_nt(x[h], bt_ref[:, sl[h]]) for h in hs]
    mk = [dot_nt(x[h], kt_ref[:, sl[h]]) for h in hs]
    xh = [dot_nt(x[h], st[h].astype(BF16)) for h in hs]
    lab = [mb[h][:C] * strict for h in hs]
    mrb = [(mb[h][C:] * incl).astype(BF16) for h in hs]
    lm = [jnp.concatenate([mk[h][:C] * strict, mk[h][C:] * incl], axis=0).astype(BF16) for h in hs]
    lv = [dot(lm[h], vb[h]) for h in hs]
    t = [eye + lab[h] for h in hs]
    p = lab
    for _ in range(5):
        pb = [p[h].astype(BF16) for h in hs]
        p = [dot(pb[h], pb[h]) for h in hs]
        t = [t[h] + dot(t[h].astype(BF16), p[h].astype(BF16)) for h in hs]
    u = [dot(t[h].astype(BF16), (xh[h][:C] + lv[h][:C]).astype(BF16)) for h in hs]
    o = [xh[h][C:] + dot(mrb[h], u[h].astype(BF16)) + lv[h][C:] for h in hs]
    for h in hs:
        o_ref[:, sl[h]] = o[h]
    uv_t = []
    for h in hs:
        uv = jnp.concatenate([u[h], v[h]], axis=0)
        uv_t.append(jnp.concatenate([uv, jnp.zeros_like(uv)], axis=1).T[:N].astype(BF16))
    for h in hs:
        y2 = jnp.concatenate([bt2_ref[:, sl[h]], kt2_ref[:, sl[h]]], axis=0)
        state_ref[h] = st[h] * gl_ref[0:1, sl[h]] + dot(uv_t[h], y2)


def rwkv_scan(at, rt, bt, kt, bt2, kt2, v, gl):
    B, S, _ = v.shape
    wb = SCAN_HPG * N_B
    spec = pl.BlockSpec((None, CHUNK, wb), lambda b, g, c: (b, c, g))
    return pl.pallas_call(
        _rwkv_scan_kernel,
        out_shape=jax.ShapeDtypeStruct((B, S, W_B), F32),
        grid=(B, H_B // SCAN_HPG, S // CHUNK),
        in_specs=[spec] * 8,
        out_specs=spec,
        scratch_shapes=[pltpu.VMEM((SCAN_HPG, N_B, N_B), F32)],
        compiler_params=_cparams(("parallel", "parallel", "arbitrary")),
        name="rwkv_scan",
    )(at, rt, bt, kt, bt2, kt2, v, gl)


def _rwkv_post_kernel(o_ref, r_ref, k_ref, v_ref, g_ref, par_ref, gm_ref, out_ref):
    o = o_ref[...]
    mu = _group_sum(o, gm_ref) * (1.0 / N_B)
    oc = o - mu
    var = _group_sum(oc * oc, gm_ref) * (1.0 / N_B)
    on = oc * lax.rsqrt(var + LNX_EPS) * par_ref[0:1, :] + par_ref[1:2, :]
    bonus = _group_sum(r_ref[...] * k_ref[...] * par_ref[2:3, :], gm_ref) * v_ref[...]
    out_ref[...] = ((on + bonus) * g_ref[...]).astype(BF16)


def rwkv_post(o, r, k, v, g, params, gmat, tm=256):
    M = o.shape[0]
    row = pl.BlockSpec((tm, W_B), lambda i: (i, 0))
    return pl.pallas_call(
        _rwkv_post_kernel,
        out_shape=jax.ShapeDtypeStruct((M, W_B), BF16),
        grid=(M // tm,),
        in_specs=[row] * 5 + [pl.BlockSpec((8, W_B), lambda i: (0, 0)),
                              pl.BlockSpec((W_B, W_B), lambda i: (0, 0))],
        out_specs=row,
        compiler_params=_cparams(("parallel",)),
        name="rwkv_post",
    )(o, r, k, v, g, params, gmat)


def _merge_kernel(oa_ref, ob_ref, oc_ref, wa_ref, wb_ref, wc_ref, g0_ref, g1_ref, g2_ref, out_ref):
    ya = jnp.dot(oa_ref[...], wa_ref[...], preferred_element_type=F32)
    yb = jnp.dot(ob_ref[...], wb_ref[...], preferred_element_type=F32)
    yc = jnp.dot(oc_ref[...], wc_ref[...], preferred_element_type=F32)
    out = (jax.nn.sigmoid(g0_ref[...]) * ya + jax.nn.sigmoid(g1_ref[...]) * yb
           + jax.nn.sigmoid(g2_ref[...]) * yc)
    out_ref[...] = out.astype(BF16)


def gated_merge(o_a, o_b, o_c, w_pa, w_pb, w_pc, proj2, tm=512, tn=1024):
    M = o_a.shape[0]
    D = D_MODEL
    nj = D // tn
    act = lambda w: pl.BlockSpec((tm, w), lambda i, j: (i, 0))
    wgt = lambda w: pl.BlockSpec((w, tn), lambda i, j: (0, j))
    gate = lambda br: pl.BlockSpec((tm, tn), lambda i, j: (i, OFF_GATE // tn + br * nj + j))
    return pl.pallas_call(
        _merge_kernel,
        out_shape=jax.ShapeDtypeStruct((M, D), BF16),
        grid=(M // tm, nj),
        in_specs=[act(W_A), act(W_B), act(W_M), wgt(W_A), wgt(W_B), wgt(W_M),
                  gate(0), gate(1), gate(2)],
        out_specs=pl.BlockSpec((tm, tn), lambda i, j: (i, j)),
        compiler_params=_cparams(("parallel", "arbitrary")),
        name="gated_merge",
    )(o_a, o_b, o_c, w_pa, w_pb, w_pc, proj2, proj2, proj2)


def _first_index_of_max(x, idx, n):
    m = jnp.max(x, axis=0, keepdims=True)
    first = jnp.min(jnp.where(x == m, idx, n), axis=0, keepdims=True)
    return m, idx == first


def _router_kernel(h_ref, wr_ref, bias_ref, gate_ref):
    E, G = N_EXPERTS, N_GROUPS
    per = E // G
    logits = lax.dot_general(wr_ref[...], h_ref[...], (((1,), (1,)), ((), ())),
                             preferred_element_type=F32)
    tm = logits.shape[1]
    scores = jax.nn.sigmoid(logits)
    biased = scores + jnp.concatenate([bias_ref[...]] * (tm // 128), axis=1)
    idx8 = lax.broadcasted_iota(jnp.int32, (per, tm), 0)
    gscores = []
    for g in range(G):
        xg = biased[g * per:(g + 1) * per, :]
        m1, hit = _first_index_of_max(xg, idx8, per)
        m2 = jnp.max(jnp.where(hit, -jnp.inf, xg), axis=0, keepdims=True)
        gscores.append(m1 + m2)
    gs = jnp.concatenate(gscores, axis=0)
    gidx = lax.broadcasted_iota(jnp.int32, (G, tm), 0)
    gsel = jnp.zeros((G, tm), F32)
    for _ in range(TOPK_GROUPS):
        _, hit = _first_index_of_max(gs, gidx, G)
        gsel = jnp.where(hit, 1.0, gsel)
        gs = jnp.where(hit, -jnp.inf, gs)
    emask = jnp.concatenate(
        [jnp.broadcast_to(gsel[g:g + 1, :], (per, tm)) for g in range(G)], axis=0)
    cand = jnp.where(emask > 0.0, biased, -jnp.inf)
    eidx = lax.broadcasted_iota(jnp.int32, (E, tm), 0)
    picked = jnp.zeros((E, tm), F32)
    for _ in range(TOP_K):
        _, hit = _first_index_of_max(cand, eidx, E)
        picked = jnp.where(hit, scores, picked)
        cand = jnp.where(hit, -jnp.inf, cand)
    gate = picked / jnp.sum(picked, axis=0, keepdims=True) * ROUTED_SCALE
    shared = (eidx == 0).astype(F32)
    gate_ref[...] = jnp.concatenate([gate, shared], axis=0).T


def moe_router(hb, w_router_t, bias_b, tm=256):
    M, D = hb.shape
    return pl.pallas_call(
        _router_kernel,
        out_shape=jax.ShapeDtypeStruct((M, 2 * N_EXPERTS), F32),
        grid=(M // tm,),
        in_specs=[pl.BlockSpec((tm, D), lambda i: (i, 0)),
                  pl.BlockSpec((N_EXPERTS, D), lambda i: (0, 0)),
                  pl.BlockSpec((N_EXPERTS, 128), lambda i: (0, 0))],
        out_specs=pl.BlockSpec((tm, 2 * N_EXPERTS), lambda i: (i, 0)),
        compiler_params=_cparams(("parallel",)),
        name="moe_router",
    )(hb, w_router_t, bias_b)


MOE_F = 6656


def _moe_act_kernel(x_ref, gate_ref, e_ref, w1_ref, w3_ref, o_ref):
    x = x_ref[...]
    h1 = jnp.dot(x, w1_ref[...], preferred_element_type=F32)
    h3 = jnp.dot(x, w3_ref[...], preferred_element_type=F32)
    g0, g1, g2 = _split3(gate_ref[...])
    e = e_ref[...]
    gexp = (jnp.dot(g0, e, preferred_element_type=F32) + jnp.dot(g1, e, preferred_element_type=F32)
            + jnp.dot(g2, e, preferred_element_type=F32))
    o_ref[...] = (h1 * jax.nn.sigmoid(h1) * h3 * gexp).astype(BF16)


def moe_hidden(hb, gate, expand, w1c, w3c, tm=1024, tn=512):
    M, D = hb.shape
    wspec = pl.BlockSpec((D, tn), lambda i, j: (0, j))
    return pl.pallas_call(
        _moe_act_kernel,
        out_shape=jax.ShapeDtypeStruct((M, MOE_F), BF16),
        grid=(M // tm, MOE_F // tn),
        in_specs=[pl.BlockSpec((tm, D), lambda i, j: (i, 0)),
                  pl.BlockSpec((tm, 2 * N_EXPERTS), lambda i, j: (i, 0)),
                  pl.BlockSpec((2 * N_EXPERTS, tn), lambda i, j: (0, j)),
                  wspec, wspec],
        out_specs=pl.BlockSpec((tm, tn), lambda i, j: (i, j)),
        compiler_params=_cparams(("parallel", "arbitrary")),
        name="moe_hidden",
    )(hb, gate, expand, w1c, w3c)


def _regroup_w_in(w, has_vres):
    D = w.shape[0]
    o = np.cumsum((0, W_A, R_KV, H_IDX * D_IDX, D_IDX, H_IDX, 3 * W_B, W_LORA, A_LORA, G_LORA, W_M,
                   3 * D_MODEL)).tolist()
    q_a, c_kv, q_idx = w[:, o[0]:o[1]], w[:, o[1]:o[2]], w[:, o[2]:o[3]]
    idx = w[:, o[3]:o[5]]
    rkv, lora = w[:, o[5]:o[6]], w[:, o[6]:o[9]]
    q_m, gates = w[:, o[9]:o[10]], w[:, o[10]:o[11]]
    zeros = lambda n: jnp.zeros((D, n), w.dtype)
    vres = w[:, o[11]:o[11] + V_LORA] if has_vres else zeros(V_LORA)
    n_lora = W_LORA + A_LORA + G_LORA + V_LORA
    parts = [gates, q_a, q_idx, rkv, q_m, c_kv, lora, vres, zeros(LORA_W - n_lora),
             idx, zeros(IDX_W - D_IDX - H_IDX)]
    used = sum(p.shape[1] for p in parts)
    parts.append(zeros(N_PROJ - used))
    return jnp.concatenate(parts, axis=1).astype(BF16)


def _lora_up_matrix(w_up, a_up, g_up, v_up):
    m = jnp.zeros((LORA_W, 4 * W_B), F32)
    m = m.at[0:W_LORA, 0:W_B].set(w_up)
    m = m.at[W_LORA:W_LORA + A_LORA, W_B:2 * W_B].set(a_up)
    m = m.at[W_LORA + A_LORA:W_LORA + A_LORA + G_LORA, 2 * W_B:3 * W_B].set(g_up)
    if v_up is not None:
        r0 = W_LORA + A_LORA + G_LORA
        m = m.at[r0:r0 + V_LORA, 3 * W_B:4 * W_B].set(v_up)
    return m.astype(BF16)


def _rows(vectors, n_rows, width):
    m = jnp.zeros((n_rows, width), F32)
    for i, vec in enumerate(vectors):
        m = m.at[i, :vec.shape[0]].set(vec)
    return m


def _expert_in_weights(w_e, w_s):
    D = w_e.shape[1]
    routed = jnp.transpose(w_e, (1, 0, 2)).reshape(D, N_EXPERTS * F_EXPERT)
    pad = jnp.zeros((D, MOE_F - N_EXPERTS * F_EXPERT - w_s.shape[1]), w_e.dtype)
    return jnp.concatenate([routed, w_s, pad], axis=1).astype(BF16)


def _expert_out_weights(w_e2, w_s2):
    D = w_e2.shape[2]
    routed = w_e2.reshape(N_EXPERTS * F_EXPERT, D)
    pad = jnp.zeros((MOE_F - N_EXPERTS * F_EXPERT - w_s2.shape[0], D), w_e2.dtype)
    return jnp.concatenate([routed, w_s2, pad], axis=0).astype(BF16)


def kernel(x, mem, ln0_g, ln0_b, w_in_first, w_in_rest, ckv_g, w_uk, w_uv, mu_rwkv, mu_vres, rw_w0, rw_w_up, rw_a0, rw_a_up, rw_g_up, rw_v0, rw_v_up, rw_k_k, rw_k_a, rw_r_k, rw_lnx_g, rw_lnx_b, w_mk, w_mv, w_pa, w_pb, w_pc, w_o, ln1_g, ln1_b, w_router, router_bias, w_e1, w_e3, w_e2, w_s1, w_s3, w_s2, ln2_g, ln2_b):
    B, S, D = x.shape
    T = B * S
    gmat = jnp.kron(jnp.eye(H_B, dtype=F32), jnp.ones((N_B, N_B), F32)).astype(BF16)
    unit = np.arange(MOE_F)
    owner = np.where(unit < (N_EXPERTS + 1) * F_EXPERT, unit // F_EXPERT, -1)
    expand = jnp.asarray(owner[None, :] == np.arange(2 * N_EXPERTS)[:, None], BF16)
    mem_b = mem.reshape(B * N_MEM, D).astype(BF16)

    h, hb = layer_norm(x.reshape(T, D), ln0_g, ln0_b)
    v_first = None
    for i in range(DEPTH):
        has_vres = i > 0
        w_in = _regroup_w_in(w_in_first if i == 0 else w_in_rest[i - 1], has_vres)
        proj2 = matmul(hb, w_in, F32, 1024, 1024)
        proj3 = proj2.reshape(B, S, N_PROJ)

        k2, vt3 = kv_project(proj2, ckv_g[i], w_uk[i].astype(BF16), w_uv[i].T.astype(BF16))
        bias5 = indexer_bias(proj3)
        o_a = sparse_attention(proj3, k2.reshape(B, S, W_A),
                               vt3.reshape(B, S // ATT_TK, W_A, ATT_TK), bias5)

        mu = mu_rwkv[i]
        mu_rkv = mu[:3 * W_B].reshape(3, W_B)
        mu_l = [mu[3 * W_B:]] + ([mu_vres[i - 1]] if has_vres else [])
        mu_lora = _rows([jnp.concatenate(mu_l)], 1, LORA_W)
        prep_par = _rows([rw_w0[i], rw_a0[i], rw_v0[i - 1] if has_vres else jnp.zeros((W_B,), F32),
                          rw_k_k[i], rw_k_a[i]], 8, W_B)
        w_lora = _lora_up_matrix(rw_w_up[i], rw_a_up[i], rw_g_up[i],
                                 rw_v_up[i - 1] if has_vres else None)
        r_b, k_b, v_b, g_b, gl_b, *scan_ops = rwkv_prep(proj3, v_first, mu_rkv, mu_lora, prep_par,
                                                        w_lora, gmat)
        if i == 0:
            v_first = v_b
        o_scan = rwkv_scan(*scan_ops, v_b, gl_b)
        post_par = _rows([rw_lnx_g[i], rw_lnx_b[i], rw_r_k[i].reshape(W_B)], 8, W_B)
        flat = lambda z: z.reshape(T, W_B)
        o_b = rwkv_post(flat(o_scan), flat(r_b), flat(k_b), flat(v_b), flat(g_b), post_par, gmat)

        km = matmul(mem_b, w_mk[i].astype(BF16), BF16, 512, 512).reshape(B, N_MEM, W_M)
        vm = matmul(mem_b, w_mv[i].astype(BF16), BF16, 512, 512).reshape(B, N_MEM, W_M)
        o_c = memory_attention(proj3, km, vm)

        merged = gated_merge(o_a.reshape(T, W_A), o_b, o_c.reshape(T, W_M), w_pa[i].astype(BF16),
                             w_pb[i].astype(BF16), w_pc[i].astype(BF16), proj2)
        h, hb = matmul_residual_ln(merged, w_o[i].astype(BF16), h, ln1_g[i], ln1_b[i])

        bias_b = jnp.broadcast_to(router_bias[i].reshape(N_EXPERTS, 1), (N_EXPERTS, 128))
        gate = moe_router(hb, w_router[i].T.astype(BF16), bias_b)
        hidden = moe_hidden(hb, gate, expand, _expert_in_weights(w_e1[i], w_s1[i]),
                            _expert_in_weights(w_e3[i], w_s3[i]))
        h, hb = matmul_residual_ln(hidden, _expert_out_weights(w_e2[i], w_s2[i]), h,
                                   ln2_g[i], ln2_b[i])
    return h.reshape(B, S, D)
```

```python
import functools

import jax
import jax.numpy as jnp
import numpy as np
from jax import lax
from jax.experimental import pallas as pl
from jax.experimental.pallas import tpu as pltpu

F32 = jnp.float32
BF16 = jnp.bfloat16

D_MODEL = 4096
DEPTH = 4
CHUNK = 64
H_A, DH_A = 16, 128
W_A = H_A * DH_A
R_KV = 256
H_IDX, D_IDX = 32, 64
TOPK_MAX = 256
H_B, N_B = 16, 64
W_B = H_B * N_B
W_LORA, A_LORA, G_LORA, V_LORA = 64, 64, 160, 32
LNX_EPS = 64e-5
N_MEM = 256
H_M, DH_M = 4, 256
W_M = H_M * DH_M
N_EXPERTS, N_GROUPS, TOPK_GROUPS, TOP_K = 64, 8, 4, 8
F_EXPERT = 96
ROUTED_SCALE = 2.5
ALPHA = (2 * DEPTH) ** 0.25
LN_EPS = 1e-5

OFF_GATE = 0
OFF_QA = 3 * D_MODEL
OFF_QIDX = OFF_QA + W_A
OFF_RKV = OFF_QIDX + H_IDX * D_IDX
OFF_QM = OFF_RKV + 3 * W_B
OFF_CKV = OFF_QM + W_M
OFF_LORA = OFF_CKV + R_KV
LORA_W = 384
OFF_IDX = OFF_LORA + LORA_W
IDX_W = 128
N_PROJ = 21504

NEG_BIAS = -1e30
VMEM_LIMIT = 56 * 1024 * 1024


def _cparams(sem):
    return pltpu.CompilerParams(dimension_semantics=sem, vmem_limit_bytes=VMEM_LIMIT)


def _split2(x):
    hi = x.astype(BF16)
    lo = (x - hi.astype(F32)).astype(BF16)
    return hi, lo


def _split3(x):
    hi = x.astype(BF16)
    r1 = x - hi.astype(F32)
    mid = r1.astype(BF16)
    lo = (r1 - mid.astype(F32)).astype(BF16)
    return hi, mid, lo


def _mm_kernel(x_ref, w_ref, o_ref):
    o_ref[...] = jnp.dot(x_ref[...], w_ref[...], preferred_element_type=F32).astype(o_ref.dtype)


def matmul(x, w, out_dtype, tm, tn):
    M, K = x.shape
    N = w.shape[1]
    tm, tn = min(tm, M), min(tn, N)
    assert M % tm == 0 and N % tn == 0
    return pl.pallas_call(
        _mm_kernel,
        out_shape=jax.ShapeDtypeStruct((M, N), out_dtype),
        grid=(M // tm, N // tn),
        in_specs=[pl.BlockSpec((tm, K), lambda i, j: (i, 0)),
                  pl.BlockSpec((K, tn), lambda i, j: (0, j))],
        out_specs=pl.BlockSpec((tm, tn), lambda i, j: (i, j)),
        compiler_params=_cparams(("parallel", "arbitrary")),
        name="matmul",
    )(x, w)


def _ln_rows(y, g, b):
    mu = jnp.mean(y, axis=-1, keepdims=True)
    yc = y - mu
    var = jnp.mean(yc * yc, axis=-1, keepdims=True)
    return yc * lax.rsqrt(var + LN_EPS) * g + b


def _ln_kernel(x_ref, g_ref, b_ref, o_ref, ob_ref):
    y = _ln_rows(x_ref[...], g_ref[...], b_ref[...])
    o_ref[...] = y
    ob_ref[...] = y.astype(BF16)


def layer_norm(x, g, b, tm=256):
    M, D = x.shape
    row = pl.BlockSpec((tm, D), lambda i: (i, 0))
    vec = pl.BlockSpec((1, D), lambda i: (0, 0))
    return pl.pallas_call(
        _ln_kernel,
        out_shape=(jax.ShapeDtypeStruct((M, D), F32), jax.ShapeDtypeStruct((M, D), BF16)),
        grid=(M // tm,),
        in_specs=[row, vec, vec],
        out_specs=(row, row),
        compiler_params=_cparams(("parallel",)),
        name="layer_norm",
    )(x, g.reshape(1, D), b.reshape(1, D))


def _mm_res_ln_kernel(x_ref, w_ref, h_ref, g_ref, b_ref, o_ref, ob_ref, *, nj, tn):
    j = pl.program_id(1)
    y = jnp.dot(x_ref[...], w_ref[...], preferred_element_type=F32)
    for jj in range(nj):
        @pl.when(j == jj)
        def _(jj=jj):
            o_ref[:, jj * tn:(jj + 1) * tn] = y

    @pl.when(j == nj - 1)
    def _():
        D = nj * tn
        s1 = None
        for jj in range(nj):
            sl = slice(jj * tn, (jj + 1) * tn)
            z = ALPHA * h_ref[:, sl] + o_ref[:, sl]
            o_ref[:, sl] = z
            p = jnp.sum(z, axis=-1, keepdims=True)
            s1 = p if s1 is None else s1 + p
        mu = s1 * (1.0 / D)
        s2 = None
        for jj in range(nj):
            zc = o_ref[:, jj * tn:(jj + 1) * tn] - mu
            p = jnp.sum(zc * zc, axis=-1, keepdims=True)
            s2 = p if s2 is None else s2 + p
        rstd = lax.rsqrt(s2 * (1.0 / D) + LN_EPS)
        for jj in range(nj):
            sl = slice(jj * tn, (jj + 1) * tn)
            z = (o_ref[:, sl] - mu) * rstd * g_ref[:, sl] + b_ref[:, sl]
            o_ref[:, sl] = z
            ob_ref[:, sl] = z.astype(BF16)


def matmul_residual_ln(x, w, h, g, b, tm=512, tn=512):
    M, K = x.shape
    D = w.shape[1]
    nj = D // tn
    row = pl.BlockSpec((tm, D), lambda i, j: (i, 0), pipeline_mode=pl.Buffered(1))
    vec = pl.BlockSpec((1, D), lambda i, j: (0, 0))
    return pl.pallas_call(
        functools.partial(_mm_res_ln_kernel, nj=nj, tn=tn),
        out_shape=(jax.ShapeDtypeStruct((M, D), F32), jax.ShapeDtypeStruct((M, D), BF16)),
        grid=(M // tm, nj),
        in_specs=[pl.BlockSpec((tm, K), lambda i, j: (i, 0), pipeline_mode=pl.Buffered(1)),
                  pl.BlockSpec((K, tn), lambda i, j: (0, j)),
                  row, vec, vec],
        out_specs=(row, row),
        compiler_params=_cparams(("parallel", "arbitrary")),
        name="matmul_residual_ln",
    )(x, w, h, g.reshape(1, D), b.reshape(1, D))


def _kv_kernel(c_ref, g_ref, wk_ref, wv_ref, k_ref, v_ref):
    c = c_ref[...]
    ms = jnp.mean(c * c, axis=-1, keepdims=True)
    cn = (c * lax.rsqrt(ms + LN_EPS) * g_ref[...]).astype(BF16)
    k_ref[...] = jnp.dot(cn, wk_ref[...], preferred_element_type=F32).astype(BF16)
    v_ref[...] = jnp.dot(cn, wv_ref[...], preferred_element_type=F32).astype(BF16)


def kv_project(proj2, ckv_g, w_uk, w_uv, tm=512):
    M = proj2.shape[0]
    out = jax.ShapeDtypeStruct((M, W_A), BF16)
    wspec = pl.BlockSpec((R_KV, W_A), lambda i: (0, 0))
    ospec = pl.BlockSpec((tm, W_A), lambda i: (i, 0))
    return pl.pallas_call(
        _kv_kernel,
        out_shape=(out, out),
        grid=(M // tm,),
        in_specs=[pl.BlockSpec((tm, R_KV), lambda i: (i, OFF_CKV // R_KV)),
                  pl.BlockSpec((1, R_KV), lambda i: (0, 0)), wspec, wspec],
        out_specs=(ospec, ospec),
        compiler_params=_cparams(("parallel",)),
        name="kv_project",
    )(proj2, ckv_g.reshape(1, R_KV), w_uk, w_uv)


IDX_TQ = 128
IDX_TK = 256
KEY_NEG_INF = -2139095041
INT_MIN = -2147483648


def _indexer_kernel(q_ref, iq_ref, ik_ref, below_ref, bias_ref, qt_ref, wt_ref, key_ref, *, n_kt,
                    ksel):
    tq, tk = IDX_TQ, IDX_TK
    i = pl.program_id(1)
    qt_ref[...] = (q_ref[...].T * (D_IDX ** -0.5)).astype(BF16)
    wt_ref[...] = iq_ref[...].T * (H_IDX ** -0.5)

    n_t = ((i + 1) * tq + tk - 1) // tk
    q_chunk = (i * tq + lax.broadcasted_iota(jnp.int32, (tk, tq), 1)) // CHUNK
    row_iota = lax.broadcasted_iota(jnp.int32, (tk, tq), 0)

    def score_tile(j, carry):
        kt = ik_ref[pl.ds(pl.multiple_of(j * tk, tk), tk), :][:, :D_IDX].astype(BF16)
        acc = jnp.zeros((tk, tq), F32)
        for h2 in range(H_IDX // 2):
            w2 = jnp.concatenate([qt_ref[(2 * h2 + t) * D_IDX:(2 * h2 + t + 1) * D_IDX, :]
                                  for t in range(2)], axis=1)
            z = jnp.dot(kt, w2, preferred_element_type=F32)
            for t in range(2):
                h = 2 * h2 + t
                acc = acc + wt_ref[D_IDX + h:D_IDX + h + 1, :] * jnp.maximum(
                    z[:, t * tq:(t + 1) * tq], 0.0)
        admissible = ((j * tk + row_iota) // CHUNK) <= q_chunk
        score = jnp.where(admissible, acc, -jnp.inf)
        bits = lax.bitcast_convert_type(score, jnp.int32)
        key_ref[j] = bits ^ ((bits >> 31) & 0x7FFFFFFF)
        return carry

    lax.fori_loop(0, n_t, score_tile, 0)

    def count_keys(pred):
        def count_tile(j, cnts):
            c = pred(key_ref[j]).astype(jnp.int32)
            cnts = list(cnts)
            for r in range(tk // 8):
                cnts[r % len(cnts)] = cnts[r % len(cnts)] + c[r * 8:(r + 1) * 8, :]
            return tuple(cnts)

        cnts = lax.fori_loop(0, n_t, count_tile, (jnp.zeros((8, tq), jnp.int32),) * 4)
        cnt = (cnts[0] + cnts[1]) + (cnts[2] + cnts[3])
        return jnp.sum(cnt.astype(F32), axis=0, keepdims=True)

    def bit_step(it, thr):
        cand = thr + lax.shift_left(jnp.int32(1), 31 - it)
        return jnp.where(count_keys(lambda key: key >= cand) >= float(ksel), cand, thr)

    thr = lax.fori_loop(0, 32, bit_step, jnp.full((1, tq), INT_MIN, jnp.int32))
    ties_needed = float(ksel) - count_keys(lambda key: key > thr)
    below = below_ref[...]

    def bias_tile(j, ties_seen):
        key = key_ref[j]
        tied = key == thr
        tied_b = jnp.where(tied, 1.0, 0.0).astype(BF16)
        rank = ties_seen + jnp.dot(below, tied_b, preferred_element_type=F32)
        sel = ((key > thr) | (tied & (rank < ties_needed))) & (key > KEY_NEG_INF)
        bias_ref[j] = jnp.where(sel, 0.0, NEG_BIAS).T.astype(BF16)
        return ties_seen + jnp.sum(tied_b.astype(F32), axis=0, keepdims=True)

    def bias_pair(jj, ties_seen):
        return bias_tile(2 * jj + 1, bias_tile(2 * jj, ties_seen))

    ties_seen = lax.fori_loop(0, n_t // 2, bias_pair, jnp.zeros((1, tq), F32))

    @pl.when(n_t % 2 == 1)
    def _():
        bias_tile(n_t - 1, ties_seen)

    def fill_tile(j, carry):
        bias_ref[j] = jnp.full((tq, tk), NEG_BIAS, BF16)
        return carry

    lax.fori_loop(n_t, n_kt, fill_tile, 0)


def indexer_bias(proj3):
    B, S, _ = proj3.shape
    tq, tk = IDX_TQ, IDX_TK
    n_qt, n_kt = S // tq, S // tk
    ksel = min(TOPK_MAX, S // 4)
    below = jnp.asarray(np.tril(np.ones((tk, tk)), -1), BF16)
    return pl.pallas_call(
        functools.partial(_indexer_kernel, n_kt=n_kt, ksel=ksel),
        out_shape=jax.ShapeDtypeStruct((B, n_qt, n_kt, tq, tk), BF16),
        grid=(B, n_qt),
        in_specs=[pl.BlockSpec((None, tq, H_IDX * D_IDX), lambda b, i: (b, i, OFF_QIDX // (H_IDX * D_IDX))),
                  pl.BlockSpec((None, tq, IDX_W), lambda b, i: (b, i, OFF_IDX // IDX_W)),
                  pl.BlockSpec((None, S, IDX_W), lambda b, i: (b, 0, OFF_IDX // IDX_W)),
                  pl.BlockSpec((tk, tk), lambda b, i: (0, 0))],
        out_specs=pl.BlockSpec((None, None, n_kt, tq, tk), lambda b, i: (b, i, 0, 0, 0)),
        scratch_shapes=[pltpu.VMEM((H_IDX * D_IDX, tq), BF16),
                        pltpu.VMEM((IDX_W, tq), F32),
                        pltpu.VMEM((n_kt, tk, tq), jnp.int32)],
        compiler_params=_cparams(("parallel", "parallel")),
        name="indexer_bias",
    )(proj3, proj3, proj3, below)


ATT_TQ = 256
ATT_HP = 4
ATT_TK = 512


ATT_RG = 16


def _attention_kernel(q_ref, k_ref, v_ref, bias_ref, o_ref, qs_ref, s_ref, p_ref, m_ref, a_ref,
                      acc_ref):
    tq, tk, hp = ATT_TQ, ATT_TK, ATT_HP
    nq = tq // IDX_TQ
    nb = tk // IDX_TK
    i = pl.program_id(2)
    scale = DH_A ** -0.5 * np.log2(np.e)
    nt = (((1,), (1,)), ((), ()))
    streams = [(hh, qq) for hh in range(hp) for qq in range(nq)]
    cols = [slice(hh * DH_A, (hh + 1) * DH_A) for hh in range(hp)]
    rows = [slice(qq * IDX_TQ, (qq + 1) * IDX_TQ) for qq in range(nq)]
    for c, (hh, qq) in enumerate(streams):
        qs_ref[c] = (q_ref[rows[qq], cols[hh]] * scale).astype(BF16)
    m_ref[...] = jnp.full(m_ref.shape, NEG_BIAS, F32)
    acc_ref[...] = jnp.zeros_like(acc_ref)
    p_ref[1] = jnp.zeros(p_ref.shape[1:], BF16)
    a_ref[1] = jnp.zeros(a_ref.shape[1:], F32)
    n_kv = k_ref.shape[0] // tk
    n_t = ((i + 1) * tq + tk - 1) // tk
    ones = jnp.ones((tk, DH_A), BF16)

    def logits(j, slot):
        row0 = pl.multiple_of(j * tk, tk)
        for c, (hh, qq) in enumerate(streams):
            s_ref[slot, c] = lax.dot_general(qs_ref[c], k_ref[pl.ds(row0, tk), cols[hh]], nt,
                                             preferred_element_type=F32)

    def weighted_values(j, slot):
        row0 = pl.multiple_of(j * tk, tk)
        return [jnp.dot(p_ref[slot, c],
                        jnp.concatenate([v_ref[pl.ds(row0, tk), cols[hh]], ones], axis=1),
                        preferred_element_type=F32) for c, (hh, qq) in enumerate(streams)]

    def accumulate(pv, slot):
        for c in range(len(streams)):
            acc_ref[c] = jnp.concatenate([a_ref[slot, c]] * 2, axis=1) * acc_ref[c] + pv[c]

    def stage(j, slot):
        other = 1 - slot
        logits(jnp.minimum(j + 1, n_kv - 1), other)
        pv = weighted_values(jnp.maximum(j - 1, 0), other)
        for c, (hh, qq) in enumerate(streams):
            for r in range(IDX_TQ // ATT_RG):
                rs = slice(r * ATT_RG, (r + 1) * ATT_RG)
                bias = jnp.concatenate([bias_ref[qq, nb * j + t, rs, :] for t in range(nb)], axis=1)
                sc = s_ref[slot, c, rs, :] + bias.astype(F32)
                m_old = m_ref[c, rs, :]
                m_new = jnp.maximum(m_old, jnp.max(sc, axis=-1, keepdims=True))
                a_ref[slot, c, rs, :] = jnp.exp2(m_old - m_new)
                m_ref[c, rs, :] = m_new
                p_ref[slot, c, rs, :] = jnp.exp2(
                    sc - jnp.concatenate([m_new] * (tk // 128), axis=1)).astype(BF16)
        accumulate(pv, other)

    logits(0, 0)

    def body(jj, carry):
        stage(2 * jj, 0)
        stage(2 * jj + 1, 1)
        return carry

    n_pairs = (n_t + 1) // 2
    lax.fori_loop(0, n_pairs, body, 0)
    accumulate(weighted_values(2 * n_pairs - 1, 1), 1)
    for c, (hh, qq) in enumerate(streams):
        o_ref[rows[qq], cols[hh]] = (acc_ref[c, :, :DH_A] / acc_ref[c, :, DH_A:]).astype(BF16)


def sparse_attention(proj3, k3, v3, bias5):
    B, S, _ = proj3.shape
    tq, hp = ATT_TQ, ATT_HP
    wb = hp * DH_A
    n_kt = S // IDX_TK
    n_streams = hp * (tq // IDX_TQ)
    assert S % (2 * ATT_TK) == 0
    return pl.pallas_call(
        _attention_kernel,
        out_shape=jax.ShapeDtypeStruct((B, S, W_A), BF16),
        grid=(B, H_A // hp, S // tq),
        in_specs=[pl.BlockSpec((None, tq, wb), lambda b, h, i: (b, i, OFF_QA // wb + h)),
                  pl.BlockSpec((None, S, wb), lambda b, h, i: (b, 0, h), pipeline_mode=pl.Buffered(1)),
                  pl.BlockSpec((None, S, wb), lambda b, h, i: (b, 0, h), pipeline_mode=pl.Buffered(1)),
                  pl.BlockSpec((None, tq // IDX_TQ, n_kt, IDX_TQ, IDX_TK),
                               lambda b, h, i: (b, i, 0, 0, 0))],
        out_specs=pl.BlockSpec((None, tq, wb), lambda b, h, i: (b, i, h)),
        scratch_shapes=[pltpu.VMEM((n_streams, IDX_TQ, DH_A), BF16),
                        pltpu.VMEM((2, n_streams, IDX_TQ, ATT_TK), F32),
                        pltpu.VMEM((2, n_streams, IDX_TQ, ATT_TK), BF16),
                        pltpu.VMEM((n_streams, IDX_TQ, 128), F32),
                        pltpu.VMEM((2, n_streams, IDX_TQ, 128), F32),
                        pltpu.VMEM((n_streams, IDX_TQ, 2 * DH_A), F32)],
        compiler_params=_cparams(("parallel", "parallel", "parallel")),
        name="sparse_attention",
    )(proj3, k3, v3, bias5)


def _mem_attention_kernel(q_ref, km_ref, vm_ref, o_ref):
    scale = DH_M ** -0.5
    for h in range(H_M):
        sl = slice(h * DH_M, (h + 1) * DH_M)
        q = (q_ref[:, sl] * scale).astype(BF16)
        s = lax.dot_general(q, km_ref[:, sl], (((1,), (1,)), ((), ())), preferred_element_type=F32)
        m = jnp.max(s, axis=-1, keepdims=True)
        p = jnp.exp(s - m)
        p = p / jnp.sum(p, axis=-1, keepdims=True)
        o_ref[:, sl] = jnp.dot(p.astype(BF16), vm_ref[:, sl], preferred_element_type=F32).astype(BF16)


def memory_attention(proj3, km3, vm3, tm=512):
    B, S, _ = proj3.shape
    mspec = pl.BlockSpec((None, N_MEM, W_M), lambda b, i: (b, 0, 0))
    return pl.pallas_call(
        _mem_attention_kernel,
        out_shape=jax.ShapeDtypeStruct((B, S, W_M), BF16),
        grid=(B, S // tm),
        in_specs=[pl.BlockSpec((None, tm, W_M), lambda b, i: (b, i, OFF_QM // W_M)), mspec, mspec],
        out_specs=pl.BlockSpec((None, tm, W_M), lambda b, i: (b, i, 0)),
        compiler_params=_cparams(("parallel", "parallel")),
        name="memory_attention",
    )(proj3, km3, vm3)


def _group_sum(x, g_ref):
    hi, lo = _split2(x)
    return (jnp.dot(hi, g_ref[...], preferred_element_type=F32)
            + jnp.dot(lo, g_ref[...], preferred_element_type=F32))


def _rwkv_prep_kernel(*refs, has_vres):
    refs = list(refs)
    r_ref, k_ref, v_ref, l_ref = refs[:4]
    vf_ref = refs[4] if has_vres else None
    (mu_ref, mul_ref, par_ref, wl_ref, g_ref, tri_ref, same_ref,
     ro_ref, ko_ref, vo_ref, go_ref, gl_ref, at_ref, rt_ref, bt_ref, kt_ref, bt2_ref, kt2_ref,
     prev_ref, prevl_ref) = refs[4 + int(has_vres):]
    s_idx = pl.program_id(1)

    @pl.when(s_idx == 0)
    def _():
        prev_ref[...] = jnp.zeros_like(prev_ref)
        prevl_ref[...] = jnp.zeros_like(prevl_ref)

    def shift(x, carry_row, mu):
        tm = x.shape[0]
        rolled = pltpu.roll(x, shift=1, axis=0)
        first = lax.broadcasted_iota(jnp.int32, x.shape, 0) == 0
        prev = jnp.where(first, jnp.broadcast_to(carry_row, x.shape), rolled)
        return x + (prev - x) * mu, x[tm - 1:tm, :]

    r, r_last = shift(r_ref[...], prev_ref[0:1, :], mu_ref[0:1, :])
    k, k_last = shift(k_ref[...], prev_ref[1:2, :], mu_ref[1:2, :])
    v, v_last = shift(v_ref[...], prev_ref[2:3, :], mu_ref[2:3, :])
    lo, l_last = shift(l_ref[...], prevl_ref[0:1, :], mul_ref[...])
    prev_ref[0:1, :] = r_last
    prev_ref[1:2, :] = k_last
    prev_ref[2:3, :] = v_last
    prevl_ref[0:1, :] = l_last

    lane = lax.broadcasted_iota(jnp.int32, lo.shape, 1)
    lt = jnp.where(lane < W_LORA, jnp.tanh(lo),
                   jnp.where((lane >= W_LORA + A_LORA) & (lane < W_LORA + A_LORA + G_LORA),
                             jax.nn.sigmoid(lo), lo))
    up = jnp.dot(lt.astype(BF16), wl_ref[...], preferred_element_type=F32)
    w0, a0, v0 = par_ref[0:1, :], par_ref[1:2, :], par_ref[2:3, :]
    k_k, k_a = par_ref[3:4, :], par_ref[4:5, :]

    y = -(w0 + up[:, 0:W_B])
    softplus = jnp.maximum(y, 0.0) + jnp.log(1.0 + jnp.exp(-jnp.abs(y)))
    w_log = -softplus - 0.5
    lw = -jnp.exp(w_log)
    a = jax.nn.sigmoid(a0 + up[:, W_B:2 * W_B])
    go_ref[...] = up[:, 2 * W_B:3 * W_B]
    if has_vres:
        v = v + (vf_ref[...] - v) * jax.nn.sigmoid(v0 + up[:, 3 * W_B:4 * W_B])
    kk = k * k_k
    norm = jnp.sqrt(_group_sum(kk * kk, g_ref))
    kk = kk / jnp.maximum(norm, 1e-12)
    k = k * (1.0 + (a - 1.0) * k_a)
    ro_ref[...] = r
    ko_ref[...] = k
    vo_ref[...] = v

    p0, p1, p2 = _split3(lw)
    tri, same = tri_ref[...], same_ref[...]
    cs = (jnp.dot(tri, p0, preferred_element_type=F32) + jnp.dot(tri, p1, preferred_element_type=F32)
          + jnp.dot(tri, p2, preferred_element_type=F32))
    cl = (jnp.dot(same, p0, preferred_element_type=F32) + jnp.dot(same, p1, preferred_element_type=F32)
          + jnp.dot(same, p2, preferred_element_type=F32))
    e_neg = jnp.exp(-cs)
    to_end = jnp.exp(cl - cs)
    kka = kk * a
    at_ref[...] = (-kk * jnp.exp(cs - lw)).astype(BF16)
    rt_ref[...] = (r * jnp.exp(cs)).astype(BF16)
    bt_ref[...] = (kka * e_neg).astype(BF16)
    kt_ref[...] = (k * e_neg).astype(BF16)
    bt2_ref[...] = (kka * to_end).astype(BF16)
    kt2_ref[...] = (k * to_end).astype(BF16)
    gl_ref[...] = jnp.exp(cl)


def _chunk_matrices(tm):
    t = np.arange(tm)
    same = (t[:, None] // CHUNK) == (t[None, :] // CHUNK)
    tri = same & (t[None, :] <= t[:, None])
    return jnp.asarray(tri, BF16), jnp.asarray(same, BF16)


def rwkv_prep(proj3, v_first, mu_rkv, mu_lora, params, w_lora, gmat, tm=256):
    B, S, _ = proj3.shape
    has_vres = v_first is not None
    tok = lambda c: pl.BlockSpec((None, tm, W_B), lambda b, s: (b, s, c))
    full = lambda shp: pl.BlockSpec(shp, lambda b, s: (0,) * len(shp))
    in_specs = [tok(OFF_RKV // W_B), tok(OFF_RKV // W_B + 1), tok(OFF_RKV // W_B + 2),
                pl.BlockSpec((None, tm, LORA_W), lambda b, s: (b, s, OFF_LORA // LORA_W))]
    args = [proj3, proj3, proj3, proj3]
    if has_vres:
        in_specs.append(tok(0))
        args.append(v_first)
    in_specs += [full((3, W_B)), full((1, LORA_W)), full((8, W_B)), full((LORA_W, 4 * W_B)),
                 full((W_B, W_B)), full((tm, tm)), full((tm, tm))]
    args += [mu_rkv, mu_lora, params, w_lora, gmat, *_chunk_matrices(tm)]
    out32 = jax.ShapeDtypeStruct((B, S, W_B), F32)
    out16 = jax.ShapeDtypeStruct((B, S, W_B), BF16)
    return pl.pallas_call(
        functools.partial(_rwkv_prep_kernel, has_vres=has_vres),
        out_shape=(out32,) * 5 + (out16,) * 6,
        grid=(B, S // tm),
        in_specs=in_specs,
        out_specs=(tok(0),) * 11,
        scratch_shapes=[pltpu.VMEM((8, W_B), F32), pltpu.VMEM((8, LORA_W), F32)],
        compiler_params=_cparams(("parallel", "arbitrary")),
        name="rwkv_prep",
    )(*args)


SCAN_HPG = 16


def _rwkv_scan_kernel(at_ref, rt_ref, bt_ref, kt_ref, bt2_ref, kt2_ref, v_ref, gl_ref, o_ref,
                      state_ref):
    C, N, H = CHUNK, N_B, SCAN_HPG

    @pl.when(pl.program_id(2) == 0)
    def _():
        state_ref[...] = jnp.zeros_like(state_ref)

    row = lax.broadcasted_iota(jnp.int32, (C, C), 0)
    col = lax.broadcasted_iota(jnp.int32, (C, C), 1)
    incl = (col <= row).astype(F32)
    strict = (col < row).astype(F32)
    eye = (col == row).astype(F32)
    nt = (((1,), (1,)), ((), ()))
    dot = functools.partial(jnp.dot, preferred_element_type=F32)
    dot_nt = lambda x, y: lax.dot_general(x, y, nt, preferred_element_type=F32)
    hs = range(H)
    sl = [slice(h * N, (h + 1) * N) for h in hs]

    x = [jnp.concatenate([at_ref[:, sl[h]], rt_ref[:, sl[h]]], axis=0) for h in hs]
    v = [v_ref[:, sl[h]] for h in hs]
    vb = [v[h].astype(BF16) for h in hs]
    st = [state_ref[h] for h in hs]
    mb = [dot_nt(x[h], bt_ref[:, sl[h]]) for h in hs]
    mk = [dot_nt(x[h], kt_ref[:, sl[h]]) for h in hs]
    xh = [dot_nt(x[h], st[h].astype(BF16)) for h in hs]
    lab = [mb[h][:C] * strict for h in hs]
    mrb = [(mb[h][C:] * incl).astype(BF16) for h in hs]
    lm = [jnp.concatenate([mk[h][:C] * strict, mk[h][C:] * incl], axis=0).astype(BF16) for h in hs]
    lv = [dot(lm[h], vb[h]) for h in hs]
    t = [eye + lab[h] for h in hs]
    p = lab
    for _ in range(5):
        pb = [p[h].astype(BF16) for h in hs]
        p = [dot(pb[h], pb[h]) for h in hs]
        t = [t[h] + dot(t[h].astype(BF16), p[h].astype(BF16)) for h in hs]
    u = [dot(t[h].astype(BF16), (xh[h][:C] + lv[h][:C]).astype(BF16)) for h in hs]
    o = [xh[h][C:] + dot(mrb[h], u[h].astype(BF16)) + lv[h][C:] for h in hs]
    for h in hs:
        o_ref[:, sl[h]] = o[h]
    uv_t = []
    for h in hs:
        uv = jnp.concatenate([u[h], v[h]], axis=0)
        uv_t.append(jnp.concatenate([uv, jnp.zeros_like(uv)], axis=1).T[:N].astype(BF16))
    for h in hs:
        y2 = jnp.concatenate([bt2_ref[:, sl[h]], kt2_ref[:, sl[h]]], axis=0)
        state_ref[h] = st[h] * gl_ref[0:1, sl[h]] + dot(uv_t[h], y2)


def rwkv_scan(at, rt, bt, kt, bt2, kt2, v, gl):
    B, S, _ = v.shape
    wb = SCAN_HPG * N_B
    spec = pl.BlockSpec((None, CHUNK, wb), lambda b, g, c: (b, c, g))
    return pl.pallas_call(
        _rwkv_scan_kernel,
        out_shape=jax.ShapeDtypeStruct((B, S, W_B), F32),
        grid=(B, H_B // SCAN_HPG, S // CHUNK),
        in_specs=[spec] * 8,
        out_specs=spec,
        scratch_shapes=[pltpu.VMEM((SCAN_HPG, N_B, N_B), F32)],
        compiler_params=_cparams(("parallel", "parallel", "arbitrary")),
        name="rwkv_scan",
    )(at, rt, bt, kt, bt2, kt2, v, gl)


def _rwkv_post_kernel(o_ref, r_ref, k_ref, v_ref, g_ref, par_ref, gm_ref, out_ref):
    o = o_ref[...]
    mu = _group_sum(o, gm_ref) * (1.0 / N_B)
    oc = o - mu
    var = _group_sum(oc * oc, gm_ref) * (1.0 / N_B)
    on = oc * lax.rsqrt(var + LNX_EPS) * par_ref[0:1, :] + par_ref[1:2, :]
    bonus = _group_sum(r_ref[...] * k_ref[...] * par_ref[2:3, :], gm_ref) * v_ref[...]
    out_ref[...] = ((on + bonus) * g_ref[...]).astype(BF16)


def rwkv_post(o, r, k, v, g, params, gmat, tm=256):
    M = o.shape[0]
    row = pl.BlockSpec((tm, W_B), lambda i: (i, 0))
    return pl.pallas_call(
        _rwkv_post_kernel,
        out_shape=jax.ShapeDtypeStruct((M, W_B), BF16),
        grid=(M // tm,),
        in_specs=[row] * 5 + [pl.BlockSpec((8, W_B), lambda i: (0, 0)),
                              pl.BlockSpec((W_B, W_B), lambda i: (0, 0))],
        out_specs=row,
        compiler_params=_cparams(("parallel",)),
        name="rwkv_post",
    )(o, r, k, v, g, params, gmat)


def _merge_kernel(oa_ref, ob_ref, oc_ref, wa_ref, wb_ref, wc_ref, g0_ref, g1_ref, g2_ref, out_ref):
    ya = jnp.dot(oa_ref[...], wa_ref[...], preferred_element_type=F32)
    yb = jnp.dot(ob_ref[...], wb_ref[...], preferred_element_type=F32)
    yc = jnp.dot(oc_ref[...], wc_ref[...], preferred_element_type=F32)
    out = (jax.nn.sigmoid(g0_ref[...]) * ya + jax.nn.sigmoid(g1_ref[...]) * yb
           + jax.nn.sigmoid(g2_ref[...]) * yc)
    out_ref[...] = out.astype(BF16)


def gated_merge(o_a, o_b, o_c, w_pa, w_pb, w_pc, proj2, tm=512, tn=1024):
    M = o_a.shape[0]
    D = D_MODEL
    nj = D // tn
    act = lambda w: pl.BlockSpec((tm, w), lambda i, j: (i, 0))
    wgt = lambda w: pl.BlockSpec((w, tn), lambda i, j: (0, j))
    gate = lambda br: pl.BlockSpec((tm, tn), lambda i, j: (i, OFF_GATE // tn + br * nj + j))
    return pl.pallas_call(
        _merge_kernel,
        out_shape=jax.ShapeDtypeStruct((M, D), BF16),
        grid=(M // tm, nj),
        in_specs=[act(W_A), act(W_B), act(W_M), wgt(W_A), wgt(W_B), wgt(W_M),
                  gate(0), gate(1), gate(2)],
        out_specs=pl.BlockSpec((tm, tn), lambda i, j: (i, j)),
        compiler_params=_cparams(("parallel", "arbitrary")),
        name="gated_merge",
    )(o_a, o_b, o_c, w_pa, w_pb, w_pc, proj2, proj2, proj2)


def _first_index_of_max(x, idx, n):
    m = jnp.max(x, axis=0, keepdims=True)
    first = jnp.min(jnp.where(x == m, idx, n), axis=0, keepdims=True)
    return m, idx == first


def _router_kernel(h_ref, wr_ref, bias_ref, gate_ref):
    E, G = N_EXPERTS, N_GROUPS
    per = E // G
    logits = lax.dot_general(wr_ref[...], h_ref[...], (((1,), (1,)), ((), ())),
                             preferred_element_type=F32)
    tm = logits.shape[1]
    scores = jax.nn.sigmoid(logits)
    biased = scores + jnp.concatenate([bias_ref[...]] * (tm // 128), axis=1)
    idx8 = lax.broadcasted_iota(jnp.int32, (per, tm), 0)
    gscores = []
    for g in range(G):
        xg = biased[g * per:(g + 1) * per, :]
        m1, hit = _first_index_of_max(xg, idx8, per)
        m2 = jnp.max(jnp.where(hit, -jnp.inf, xg), axis=0, keepdims=True)
        gscores.append(m1 + m2)
    gs = jnp.concatenate(gscores, axis=0)
    gidx = lax.broadcasted_iota(jnp.int32, (G, tm), 0)
    gsel = jnp.zeros((G, tm), F32)
    for _ in range(TOPK_GROUPS):
        _, hit = _first_index_of_max(gs, gidx, G)
        gsel = jnp.where(hit, 1.0, gsel)
        gs = jnp.where(hit, -jnp.inf, gs)
    emask = jnp.concatenate(
        [jnp.broadcast_to(gsel[g:g + 1, :], (per, tm)) for g in range(G)], axis=0)
    cand = jnp.where(emask > 0.0, biased, -jnp.inf)
    eidx = lax.broadcasted_iota(jnp.int32, (E, tm), 0)
    picked = jnp.zeros((E, tm), F32)
    for _ in range(TOP_K):
        _, hit = _first_index_of_max(cand, eidx, E)
        picked = jnp.where(hit, scores, picked)
        cand = jnp.where(hit, -jnp.inf, cand)
    gate = picked / jnp.sum(picked, axis=0, keepdims=True) * ROUTED_SCALE
    shared = (eidx == 0).astype(F32)
    gate_ref[...] = jnp.concatenate([gate, shared], axis=0).T


def moe_router(hb, w_router_t, bias_b, tm=256):
    M, D = hb.shape
    return pl.pallas_call(
        _router_kernel,
        out_shape=jax.ShapeDtypeStruct((M, 2 * N_EXPERTS), F32),
        grid=(M // tm,),
        in_specs=[pl.BlockSpec((tm, D), lambda i: (i, 0)),
                  pl.BlockSpec((N_EXPERTS, D), lambda i: (0, 0)),
                  pl.BlockSpec((N_EXPERTS, 128), lambda i: (0, 0))],
        out_specs=pl.BlockSpec((tm, 2 * N_EXPERTS), lambda i: (i, 0)),
        compiler_params=_cparams(("parallel",)),
        name="moe_router",
    )(hb, w_router_t, bias_b)


MOE_F = 6656


def _moe_act_kernel(x_ref, gate_ref, e_ref, w1_ref, w3_ref, o_ref):
    x = x_ref[...]
    h1 = jnp.dot(x, w1_ref[...], preferred_element_type=F32)
    h3 = jnp.dot(x, w3_ref[...], preferred_element_type=F32)
    g0, g1, g2 = _split3(gate_ref[...])
    e = e_ref[...]
    gexp = (jnp.dot(g0, e, preferred_element_type=F32) + jnp.dot(g1, e, preferred_element_type=F32)
            + jnp.dot(g2, e, preferred_element_type=F32))
    o_ref[...] = (h1 * jax.nn.sigmoid(h1) * h3 * gexp).astype(BF16)


def moe_hidden(hb, gate, expand, w1c, w3c, tm=1024, tn=512):
    M, D = hb.shape
    wspec = pl.BlockSpec((D, tn), lambda i, j: (0, j))
    return pl.pallas_call(
        _moe_act_kernel,
        out_shape=jax.ShapeDtypeStruct((M, MOE_F), BF16),
        grid=(M // tm, MOE_F // tn),
        in_specs=[pl.BlockSpec((tm, D), lambda i, j: (i, 0)),
                  pl.BlockSpec((tm, 2 * N_EXPERTS), lambda i, j: (i, 0)),
                  pl.BlockSpec((2 * N_EXPERTS, tn), lambda i, j: (0, j)),
                  wspec, wspec],
        out_specs=pl.BlockSpec((tm, tn), lambda i, j: (i, j)),
        compiler_params=_cparams(("parallel", "arbitrary")),
        name="moe_hidden",
    )(hb, gate, expand, w1c, w3c)


def _regroup_w_in(w, has_vres):
    D = w.shape[0]
    o = np.cumsum((0, W_A, R_KV, H_IDX * D_IDX, D_IDX, H_IDX, 3 * W_B, W_LORA, A_LORA, G_LORA, W_M,
                   3 * D_MODEL)).tolist()
    q_a, c_kv, q_idx = w[:, o[0]:o[1]], w[:, o[1]:o[2]], w[:, o[2]:o[3]]
    idx = w[:, o[3]:o[5]]
    rkv, lora = w[:, o[5]:o[6]], w[:, o[6]:o[9]]
    q_m, gates = w[:, o[9]:o[10]], w[:, o[10]:o[11]]
    zeros = lambda n: jnp.zeros((D, n), w.dtype)
    vres = w[:, o[11]:o[11] + V_LORA] if has_vres else zeros(V_LORA)
    n_lora = W_LORA + A_LORA + G_LORA + V_LORA
    parts = [gates, q_a, q_idx, rkv, q_m, c_kv, lora, vres, zeros(LORA_W - n_lora),
             idx, zeros(IDX_W - D_IDX - H_IDX)]
    used = sum(p.shape[1] for p in parts)
    parts.append(zeros(N_PROJ - used))
    return jnp.concatenate(parts, axis=1).astype(BF16)


def _lora_up_matrix(w_up, a_up, g_up, v_up):
    m = jnp.zeros((LORA_W, 4 * W_B), F32)
    m = m.at[0:W_LORA, 0:W_B].set(w_up)
    m = m.at[W_LORA:W_LORA + A_LORA, W_B:2 * W_B].set(a_up)
    m = m.at[W_LORA + A_LORA:W_LORA + A_LORA + G_LORA, 2 * W_B:3 * W_B].set(g_up)
    if v_up is not None:
        r0 = W_LORA + A_LORA + G_LORA
        m = m.at[r0:r0 + V_LORA, 3 * W_B:4 * W_B].set(v_up)
    return m.astype(BF16)


def _rows(vectors, n_rows, width):
    m = jnp.zeros((n_rows, width), F32)
    for i, vec in enumerate(vectors):
        m = m.at[i, :vec.shape[0]].set(vec)
    return m


def _expert_in_weights(w_e, w_s):
    D = w_e.shape[1]
    routed = jnp.transpose(w_e, (1, 0, 2)).reshape(D, N_EXPERTS * F_EXPERT)
    pad = jnp.zeros((D, MOE_F - N_EXPERTS * F_EXPERT - w_s.shape[1]), w_e.dtype)
    return jnp.concatenate([routed, w_s, pad], axis=1).astype(BF16)


def _expert_out_weights(w_e2, w_s2):
    D = w_e2.shape[2]
    routed = w_e2.reshape(N_EXPERTS * F_EXPERT, D)
    pad = jnp.zeros((MOE_F - N_EXPERTS * F_EXPERT - w_s2.shape[0], D), w_e2.dtype)
    return jnp.concatenate([routed, w_s2, pad], axis=0).astype(BF16)


def kernel(x, mem, ln0_g, ln0_b, w_in_first, w_in_rest, ckv_g, w_uk, w_uv, mu_rwkv, mu_vres, rw_w0, rw_w_up, rw_a0, rw_a_up, rw_g_up, rw_v0, rw_v_up, rw_k_k, rw_k_a, rw_r_k, rw_lnx_g, rw_lnx_b, w_mk, w_mv, w_pa, w_pb, w_pc, w_o, ln1_g, ln1_b, w_router, router_bias, w_e1, w_e3, w_e2, w_s1, w_s3, w_s2, ln2_g, ln2_b):
    B, S, D = x.shape
    T = B * S
    gmat = jnp.kron(jnp.eye(H_B, dtype=F32), jnp.ones((N_B, N_B), F32)).astype(BF16)
    unit = np.arange(MOE_F)
    owner = np.where(unit < (N_EXPERTS + 1) * F_EXPERT, unit // F_EXPERT, -1)
    expand = jnp.asarray(owner[None, :] == np.arange(2 * N_EXPERTS)[:, None], BF16)
    mem_b = mem.reshape(B * N_MEM, D).astype(BF16)

    h, hb = layer_norm(x.reshape(T, D), ln0_g, ln0_b)
    v_first = None
    for i in range(DEPTH):
        has_vres = i > 0
        w_in = _regroup_w_in(w_in_first if i == 0 else w_in_rest[i - 1], has_vres)
        proj2 = matmul(hb, w_in, F32, 1024, 1024)
        proj3 = proj2.reshape(B, S, N_PROJ)

        k2, v2 = kv_project(proj2, ckv_g[i], w_uk[i].astype(BF16), w_uv[i].astype(BF16))
        bias5 = indexer_bias(proj3)
        o_a = sparse_attention(proj3, k2.reshape(B, S, W_A), v2.reshape(B, S, W_A), bias5)

        mu = mu_rwkv[i]
        mu_rkv = mu[:3 * W_B].reshape(3, W_B)
        mu_l = [mu[3 * W_B:]] + ([mu_vres[i - 1]] if has_vres else [])
        mu_lora = _rows([jnp.concatenate(mu_l)], 1, LORA_W)
        prep_par = _rows([rw_w0[i], rw_a0[i], rw_v0[i - 1] if has_vres else jnp.zeros((W_B,), F32),
                          rw_k_k[i], rw_k_a[i]], 8, W_B)
        w_lora = _lora_up_matrix(rw_w_up[i], rw_a_up[i], rw_g_up[i],
                                 rw_v_up[i - 1] if has_vres else None)
        r_b, k_b, v_b, g_b, gl_b, *scan_ops = rwkv_prep(proj3, v_first, mu_rkv, mu_lora, prep_par,
                                                        w_lora, gmat)
        if i == 0:
            v_first = v_b
        o_scan = rwkv_scan(*scan_ops, v_b, gl_b)
        post_par = _rows([rw_lnx_g[i], rw_lnx_b[i], rw_r_k[i].reshape(W_B)], 8, W_B)
        flat = lambda z: z.reshape(T, W_B)
        o_b = rwkv_post(flat(o_scan), flat(r_b), flat(k_b), flat(v_b), flat(g_b), post_par, gmat)

        km = matmul(mem_b, w_mk[i].astype(BF16), BF16, 512, 512).reshape(B, N_MEM, W_M)
        vm = matmul(mem_b, w_mv[i].astype(BF16), BF16, 512, 512).reshape(B, N_MEM, W_M)
        o_c = memory_attention(proj3, km, vm)

        merged = gated_merge(o_a.reshape(T, W_A), o_b, o_c.reshape(T, W_M), w_pa[i].astype(BF16),
                             w_pb[i].astype(BF16), w_pc[i].astype(BF16), proj2)
        h, hb = matmul_residual_ln(merged, w_o[i].astype(BF16), h, ln1_g[i], ln1_b[i])

        bias_b = jnp.broadcast_to(router_bias[i].reshape(N_EXPERTS, 1), (N_EXPERTS, 128))
        gate = moe_router(hb, w_router[i].T.astype(BF16), bias_b)
        hidden = moe_hidden(hb, gate, expand, _expert_in_weights(w_e1[i], w_s1[i]),
                            _expert_in_weights(w_e3[i], w_s3[i]))
        h, hb = matmul_residual_ln(hidden, _expert_out_weights(w_e2[i], w_s2[i]), h,
                                   ln2_g[i], ln2_b[i])
    return h.reshape(B, S, D)
```

```python
import functools

import jax
import jax.numpy as jnp
import numpy as np
from jax import lax
from jax.experimental import pallas as pl
from jax.experimental.pallas import tpu as pltpu

F32 = jnp.float32
BF16 = jnp.bfloat16

D_MODEL = 4096
DEPTH = 4
CHUNK = 64
H_A, DH_A = 16, 128
W_A = H_A * DH_A
R_KV = 256
H_IDX, D_IDX = 32, 64
TOPK_MAX = 256
H_B, N_B = 16, 64
W_B = H_B * N_B
W_LORA, A_LORA, G_LORA, V_LORA = 64, 64, 160, 32
LNX_EPS = 64e-5
N_MEM = 256
H_M, DH_M = 4, 256
W_M = H_M * DH_M
N_EXPERTS, N_GROUPS, TOPK_GROUPS, TOP_K = 64, 8, 4, 8
F_EXPERT = 96
ROUTED_SCALE = 2.5
ALPHA = (2 * DEPTH) ** 0.25
LN_EPS = 1e-5

OFF_GATE = 0
OFF_QA = 3 * D_MODEL
OFF_QIDX = OFF_QA + W_A
OFF_RKV = OFF_QIDX + H_IDX * D_IDX
OFF_QM = OFF_RKV + 3 * W_B
OFF_CKV = OFF_QM + W_M
OFF_LORA = OFF_CKV + R_KV
LORA_W = 384
OFF_IDX = OFF_LORA + LORA_W
IDX_W = 128
N_PROJ = 21504

NEG_BIAS = -1e30
VMEM_LIMIT = 56 * 1024 * 1024


def _cparams(sem):
    return pltpu.CompilerParams(dimension_semantics=sem, vmem_limit_bytes=VMEM_LIMIT)


def _split2(x):
    hi = x.astype(BF16)
    lo = (x - hi.astype(F32)).astype(BF16)
    return hi, lo


def _split3(x):
    hi = x.astype(BF16)
    r1 = x - hi.astype(F32)
    mid = r1.astype(BF16)
    lo = (r1 - mid.astype(F32)).astype(BF16)
    return hi, mid, lo


def _mm_kernel(x_ref, w_ref, o_ref):
    o_ref[...] = jnp.dot(x_ref[...], w_ref[...], preferred_element_type=F32).astype(o_ref.dtype)


def matmul(x, w, out_dtype, tm, tn):
    M, K = x.shape
    N = w.shape[1]
    tm, tn = min(tm, M), min(tn, N)
    assert M % tm == 0 and N % tn == 0
    return pl.pallas_call(
        _mm_kernel,
        out_shape=jax.ShapeDtypeStruct((M, N), out_dtype),
        grid=(M // tm, N // tn),
        in_specs=[pl.BlockSpec((tm, K), lambda i, j: (i, 0)),
                  pl.BlockSpec((K, tn), lambda i, j: (0, j))],
        out_specs=pl.BlockSpec((tm, tn), lambda i, j: (i, j)),
        compiler_params=_cparams(("parallel", "arbitrary")),
        name="matmul",
    )(x, w)


def _ln_rows(y, g, b):
    mu = jnp.mean(y, axis=-1, keepdims=True)
    yc = y - mu
    var = jnp.mean(yc * yc, axis=-1, keepdims=True)
    return yc * lax.rsqrt(var + LN_EPS) * g + b


def _ln_kernel(x_ref, g_ref, b_ref, o_ref, ob_ref):
    y = _ln_rows(x_ref[...], g_ref[...], b_ref[...])
    o_ref[...] = y
    ob_ref[...] = y.astype(BF16)


def layer_norm(x, g, b, tm=256):
    M, D = x.shape
    row = pl.BlockSpec((tm, D), lambda i: (i, 0))
    vec = pl.BlockSpec((1, D), lambda i: (0, 0))
    return pl.pallas_call(
        _ln_kernel,
        out_shape=(jax.ShapeDtypeStruct((M, D), F32), jax.ShapeDtypeStruct((M, D), BF16)),
        grid=(M // tm,),
        in_specs=[row, vec, vec],
        out_specs=(row, row),
        compiler_params=_cparams(("parallel",)),
        name="layer_norm",
    )(x, g.reshape(1, D), b.reshape(1, D))


def _mm_res_ln_kernel(x_ref, w_ref, h_ref, g_ref, b_ref, o_ref, ob_ref, *, nj, tn):
    j = pl.program_id(1)
    y = jnp.dot(x_ref[...], w_ref[...], preferred_element_type=F32)
    for jj in range(nj):
        @pl.when(j == jj)
        def _(jj=jj):
            o_ref[:, jj * tn:(jj + 1) * tn] = y

    @pl.when(j == nj - 1)
    def _():
        D = nj * tn
        s1 = None
        for jj in range(nj):
            sl = slice(jj * tn, (jj + 1) * tn)
            z = ALPHA * h_ref[:, sl] + o_ref[:, sl]
            o_ref[:, sl] = z
            p = jnp.sum(z, axis=-1, keepdims=True)
            s1 = p if s1 is None else s1 + p
        mu = s1 * (1.0 / D)
        s2 = None
        for jj in range(nj):
            zc = o_ref[:, jj * tn:(jj + 1) * tn] - mu
            p = jnp.sum(zc * zc, axis=-1, keepdims=True)
            s2 = p if s2 is None else s2 + p
        rstd = lax.rsqrt(s2 * (1.0 / D) + LN_EPS)
        for jj in range(nj):
            sl = slice(jj * tn, (jj + 1) * tn)
            z = (o_ref[:, sl] - mu) * rstd * g_ref[:, sl] + b_ref[:, sl]
            o_ref[:, sl] = z
            ob_ref[:, sl] = z.astype(BF16)


def matmul_residual_ln(x, w, h, g, b, tm=512, tn=512):
    M, K = x.shape
    D = w.shape[1]
    nj = D // tn
    row = pl.BlockSpec((tm, D), lambda i, j: (i, 0), pipeline_mode=pl.Buffered(1))
    vec = pl.BlockSpec((1, D), lambda i, j: (0, 0))
    return pl.pallas_call(
        functools.partial(_mm_res_ln_kernel, nj=nj, tn=tn),
        out_shape=(jax.ShapeDtypeStruct((M, D), F32), jax.ShapeDtypeStruct((M, D), BF16)),
        grid=(M // tm, nj),
        in_specs=[pl.BlockSpec((tm, K), lambda i, j: (i, 0), pipeline_mode=pl.Buffered(1)),
                  pl.BlockSpec((K, tn), lambda i, j: (0, j)),
                  row, vec, vec],
        out_specs=(row, row),
        compiler_params=_cparams(("parallel", "arbitrary")),
        name="matmul_residual_ln",
    )(x, w, h, g.reshape(1, D), b.reshape(1, D))


def _kv_kernel(c_ref, g_ref, wk_ref, wv_ref, k_ref, v_ref):
    c = c_ref[...]
    ms = jnp.mean(c * c, axis=-1, keepdims=True)
    cn = (c * lax.rsqrt(ms + LN_EPS) * g_ref[...]).astype(BF16)
    k_ref[...] = jnp.dot(cn, wk_ref[...], preferred_element_type=F32).astype(BF16)
    v_ref[...] = jnp.dot(cn, wv_ref[...], preferred_element_type=F32).astype(BF16)


def kv_project(proj2, ckv_g, w_uk, w_uv, tm=512):
    M = proj2.shape[0]
    out = jax.ShapeDtypeStruct((M, W_A), BF16)
    wspec = pl.BlockSpec((R_KV, W_A), lambda i: (0, 0))
    ospec = pl.BlockSpec((tm, W_A), lambda i: (i, 0))
    return pl.pallas_call(
        _kv_kernel,
        out_shape=(out, out),
        grid=(M // tm,),
        in_specs=[pl.BlockSpec((tm, R_KV), lambda i: (i, OFF_CKV // R_KV)),
                  pl.BlockSpec((1, R_KV), lambda i: (0, 0)), wspec, wspec],
        out_specs=(ospec, ospec),
        compiler_params=_cparams(("parallel",)),
        name="kv_project",
    )(proj2, ckv_g.reshape(1, R_KV), w_uk, w_uv)


IDX_TQ = 128
IDX_TK = 256
KEY_NEG_INF = -2139095041
INT_MIN = -2147483648


def _indexer_kernel(q_ref, iq_ref, ik_ref, below_ref, bias_ref, qt_ref, wt_ref, key_ref, *, n_kt,
                    ksel):
    tq, tk = IDX_TQ, IDX_TK
    i = pl.program_id(1)
    qt_ref[...] = (q_ref[...].T * (D_IDX ** -0.5)).astype(BF16)
    wt_ref[...] = iq_ref[...].T * (H_IDX ** -0.5)

    n_t = ((i + 1) * tq + tk - 1) // tk
    q_chunk = (i * tq + lax.broadcasted_iota(jnp.int32, (tk, tq), 1)) // CHUNK
    row_iota = lax.broadcasted_iota(jnp.int32, (tk, tq), 0)

    def score_tile(j, carry):
        kt = ik_ref[pl.ds(pl.multiple_of(j * tk, tk), tk), :][:, :D_IDX].astype(BF16)
        acc = jnp.zeros((tk, tq), F32)
        for h2 in range(H_IDX // 2):
            w2 = jnp.concatenate([qt_ref[(2 * h2 + t) * D_IDX:(2 * h2 + t + 1) * D_IDX, :]
                                  for t in range(2)], axis=1)
            z = jnp.dot(kt, w2, preferred_element_type=F32)
            for t in range(2):
                h = 2 * h2 + t
                acc = acc + wt_ref[D_IDX + h:D_IDX + h + 1, :] * jnp.maximum(
                    z[:, t * tq:(t + 1) * tq], 0.0)
        admissible = ((j * tk + row_iota) // CHUNK) <= q_chunk
        score = jnp.where(admissible, acc, -jnp.inf)
        bits = lax.bitcast_convert_type(score, jnp.int32)
        key_ref[j] = bits ^ ((bits >> 31) & 0x7FFFFFFF)
        return carry

    lax.fori_loop(0, n_t, score_tile, 0)

    def count_keys(pred):
        def count_tile(j, cnts):
            c = pred(key_ref[j]).astype(jnp.int32)
            cnts = list(cnts)
            for r in range(tk // 8):
                cnts[r % len(cnts)] = cnts[r % len(cnts)] + c[r * 8:(r + 1) * 8, :]
            return tuple(cnts)

        cnts = lax.fori_loop(0, n_t, count_tile, (jnp.zeros((8, tq), jnp.int32),) * 4)
        cnt = (cnts[0] + cnts[1]) + (cnts[2] + cnts[3])
        return jnp.sum(cnt.astype(F32), axis=0, keepdims=True)

    def bit_step(it, thr):
        cand = thr + lax.shift_left(jnp.int32(1), 31 - it)
        return jnp.where(count_keys(lambda key: key >= cand) >= float(ksel), cand, thr)

    thr = lax.fori_loop(0, 32, bit_step, jnp.full((1, tq), INT_MIN, jnp.int32))
    ties_needed = float(ksel) - count_keys(lambda key: key > thr)
    below = below_ref[...]

    def bias_tile(j, ties_seen):
        key = key_ref[j]
        tied = key == thr
        tied_b = jnp.where(tied, 1.0, 0.0).astype(BF16)
        rank = ties_seen + jnp.dot(below, tied_b, preferred_element_type=F32)
        sel = ((key > thr) | (tied & (rank < ties_needed))) & (key > KEY_NEG_INF)
        bias_ref[j] = jnp.where(sel, 0.0, NEG_BIAS).T.astype(BF16)
        return ties_seen + jnp.sum(tied_b.astype(F32), axis=0, keepdims=True)

    def bias_pair(jj, ties_seen):
        return bias_tile(2 * jj + 1, bias_tile(2 * jj, ties_seen))

    ties_seen = lax.fori_loop(0, n_t // 2, bias_pair, jnp.zeros((1, tq), F32))

    @pl.when(n_t % 2 == 1)
    def _():
        bias_tile(n_t - 1, ties_seen)

    def fill_tile(j, carry):
        bias_ref[j] = jnp.full((tq, tk), NEG_BIAS, BF16)
        return carry

    lax.fori_loop(n_t, n_kt, fill_tile, 0)


def indexer_bias(proj3):
    B, S, _ = proj3.shape
    tq, tk = IDX_TQ, IDX_TK
    n_qt, n_kt = S // tq, S // tk
    ksel = min(TOPK_MAX, S // 4)
    below = jnp.asarray(np.tril(np.ones((tk, tk)), -1), BF16)
    return pl.pallas_call(
        functools.partial(_indexer_kernel, n_kt=n_kt, ksel=ksel),
        out_shape=jax.ShapeDtypeStruct((B, n_qt, n_kt, tq, tk), BF16),
        grid=(B, n_qt),
        in_specs=[pl.BlockSpec((None, tq, H_IDX * D_IDX), lambda b, i: (b, i, OFF_QIDX // (H_IDX * D_IDX))),
                  pl.BlockSpec((None, tq, IDX_W), lambda b, i: (b, i, OFF_IDX // IDX_W)),
                  pl.BlockSpec((None, S, IDX_W), lambda b, i: (b, 0, OFF_IDX // IDX_W)),
                  pl.BlockSpec((tk, tk), lambda b, i: (0, 0))],
        out_specs=pl.BlockSpec((None, None, n_kt, tq, tk), lambda b, i: (b, i, 0, 0, 0)),
        scratch_shapes=[pltpu.VMEM((H_IDX * D_IDX, tq), BF16),
                        pltpu.VMEM((IDX_W, tq), F32),
                        pltpu.VMEM((n_kt, tk, tq), jnp.int32)],
        compiler_params=_cparams(("parallel", "parallel")),
        name="indexer_bias",
    )(proj3, proj3, proj3, below)


ATT_TQ = 256
ATT_HP = 4
ATT_TK = 512


ATT_RG = 16


def _attention_kernel(q_ref, k_ref, v_ref, bias_ref, o_ref, qs_ref, s_ref, p_ref, m_ref, a_ref,
                      acc_ref):
    tq, tk, hp = ATT_TQ, ATT_TK, ATT_HP
    nq = tq // IDX_TQ
    nb = tk // IDX_TK
    i = pl.program_id(2)
    scale = DH_A ** -0.5 * np.log2(np.e)
    nt = (((1,), (1,)), ((), ()))
    streams = [(hh, qq) for hh in range(hp) for qq in range(nq)]
    cols = [slice(hh * DH_A, (hh + 1) * DH_A) for hh in range(hp)]
    rows = [slice(qq * IDX_TQ, (qq + 1) * IDX_TQ) for qq in range(nq)]
    for c, (hh, qq) in enumerate(streams):
        qs_ref[c] = (q_ref[rows[qq], cols[hh]] * scale).astype(BF16)
    m_ref[...] = jnp.full(m_ref.shape, NEG_BIAS, F32)
    acc_ref[...] = jnp.zeros_like(acc_ref)
    p_ref[1] = jnp.zeros(p_ref.shape[1:], BF16)
    a_ref[1] = jnp.zeros(a_ref.shape[1:], F32)
    n_kv = k_ref.shape[0] // tk
    n_t = ((i + 1) * tq + tk - 1) // tk
    ones = jnp.ones((tk, DH_A), BF16)

    def logits(j, slot):
        row0 = pl.multiple_of(j * tk, tk)
        for c, (hh, qq) in enumerate(streams):
            s_ref[slot, c] = lax.dot_general(qs_ref[c], k_ref[pl.ds(row0, tk), cols[hh]], nt,
                                             preferred_element_type=F32)

    def weighted_values(j, slot):
        row0 = pl.multiple_of(j * tk, tk)
        return [jnp.dot(p_ref[slot, c],
                        jnp.concatenate([v_ref[pl.ds(row0, tk), cols[hh]], ones], axis=1),
                        preferred_element_type=F32) for c, (hh, qq) in enumerate(streams)]

    def accumulate(pv, slot):
        for c in range(len(streams)):
            acc_ref[c] = jnp.concatenate([a_ref[slot, c]] * 2, axis=1) * acc_ref[c] + pv[c]

    def stage(j, slot):
        other = 1 - slot
        logits(jnp.minimum(j + 1, n_kv - 1), other)
        pv = weighted_values(jnp.maximum(j - 1, 0), other)
        for c, (hh, qq) in enumerate(streams):
            for r in range(IDX_TQ // ATT_RG):
                rs = slice(r * ATT_RG, (r + 1) * ATT_RG)
                bias = jnp.concatenate([bias_ref[qq, nb * j + t, rs, :] for t in range(nb)], axis=1)
                sc = s_ref[slot, c, rs, :] + bias.astype(F32)
                m_old = m_ref[c, rs, :]
                m_new = jnp.maximum(m_old, jnp.max(sc, axis=-1, keepdims=True))
                a_ref[slot, c, rs, :] = jnp.exp2(m_old - m_new)
                m_ref[c, rs, :] = m_new
                p_ref[slot, c, rs, :] = jnp.exp2(
                    sc - jnp.concatenate([m_new] * (tk // 128), axis=1)).astype(BF16)
        accumulate(pv, other)

    logits(0, 0)

    def body(jj, carry):
        stage(2 * jj, 0)
        stage(2 * jj + 1, 1)
        return carry

    lax.fori_loop(0, n_t // 2, body, 0)

    @pl.when(n_t % 2 == 1)
    def _():
        stage(n_t - 1, 0)
        accumulate(weighted_values(n_t - 1, 0), 0)

    @pl.when(n_t % 2 == 0)
    def _():
        accumulate(weighted_values(n_t - 1, 1), 1)

    for c, (hh, qq) in enumerate(streams):
        o_ref[rows[qq], cols[hh]] = (acc_ref[c, :, :DH_A] / acc_ref[c, :, DH_A:]).astype(BF16)


def sparse_attention(proj3, k3, v3, bias5):
    B, S, _ = proj3.shape
    tq, hp = ATT_TQ, ATT_HP
    wb = hp * DH_A
    n_kt = S // IDX_TK
    n_streams = hp * (tq // IDX_TQ)
    return pl.pallas_call(
        _attention_kernel,
        out_shape=jax.ShapeDtypeStruct((B, S, W_A), BF16),
        grid=(B, H_A // hp, S // tq),
        in_specs=[pl.BlockSpec((None, tq, wb), lambda b, h, i: (b, i, OFF_QA // wb + h)),
                  pl.BlockSpec((None, S, wb), lambda b, h, i: (b, 0, h), pipeline_mode=pl.Buffered(1)),
                  pl.BlockSpec((None, S, wb), lambda b, h, i: (b, 0, h), pipeline_mode=pl.Buffered(1)),
                  pl.BlockSpec((None, tq // IDX_TQ, n_kt, IDX_TQ, IDX_TK),
                               lambda b, h, i: (b, i, 0, 0, 0))],
        out_specs=pl.BlockSpec((None, tq, wb), lambda b, h, i: (b, i, h)),
        scratch_shapes=[pltpu.VMEM((n_streams, IDX_TQ, DH_A), BF16),
                        pltpu.VMEM((2, n_streams, IDX_TQ, ATT_TK), F32),
                        pltpu.VMEM((2, n_streams, IDX_TQ, ATT_TK), BF16),
                        pltpu.VMEM((n_streams, IDX_TQ, 128), F32),
                        pltpu.VMEM((2, n_streams, IDX_TQ, 128), F32),
                        pltpu.VMEM((n_streams, IDX_TQ, 2 * DH_A), F32)],
        compiler_params=_cparams(("parallel", "parallel", "parallel")),
        name="sparse_attention",
    )(proj3, k3, v3, bias5)


def _mem_attention_kernel(q_ref, km_ref, vm_ref, o_ref):
    scale = DH_M ** -0.5
    for h in range(H_M):
        sl = slice(h * DH_M, (h + 1) * DH_M)
        q = (q_ref[:, sl] * scale).astype(BF16)
        s = lax.dot_general(q, km_ref[:, sl], (((1,), (1,)), ((), ())), preferred_element_type=F32)
        m = jnp.max(s, axis=-1, keepdims=True)
        p = jnp.exp(s - m)
        p = p / jnp.sum(p, axis=-1, keepdims=True)
        o_ref[:, sl] = jnp.dot(p.astype(BF16), vm_ref[:, sl], preferred_element_type=F32).astype(BF16)


def memory_attention(proj3, km3, vm3, tm=512):
    B, S, _ = proj3.shape
    mspec = pl.BlockSpec((None, N_MEM, W_M), lambda b, i: (b, 0, 0))
    return pl.pallas_call(
        _mem_attention_kernel,
        out_shape=jax.ShapeDtypeStruct((B, S, W_M), BF16),
        grid=(B, S // tm),
        in_specs=[pl.BlockSpec((None, tm, W_M), lambda b, i: (b, i, OFF_QM // W_M)), mspec, mspec],
        out_specs=pl.BlockSpec((None, tm, W_M), lambda b, i: (b, i, 0)),
        compiler_params=_cparams(("parallel", "parallel")),
        name="memory_attention",
    )(proj3, km3, vm3)


def _group_sum(x, g_ref):
    hi, lo = _split2(x)
    return (jnp.dot(hi, g_ref[...], preferred_element_type=F32)
            + jnp.dot(lo, g_ref[...], preferred_element_type=F32))


def _rwkv_prep_kernel(*refs, has_vres):
    refs = list(refs)
    r_ref, k_ref, v_ref, l_ref = refs[:4]
    vf_ref = refs[4] if has_vres else None
    (mu_ref, mul_ref, par_ref, wl_ref, g_ref, tri_ref, same_ref,
     ro_ref, ko_ref, vo_ref, go_ref, gl_ref, at_ref, rt_ref, bt_ref, kt_ref, bt2_ref, kt2_ref,
     prev_ref, prevl_ref) = refs[4 + int(has_vres):]
    s_idx = pl.program_id(1)

    @pl.when(s_idx == 0)
    def _():
        prev_ref[...] = jnp.zeros_like(prev_ref)
        prevl_ref[...] = jnp.zeros_like(prevl_ref)

    def shift(x, carry_row, mu):
        tm = x.shape[0]
        rolled = pltpu.roll(x, shift=1, axis=0)
        first = lax.broadcasted_iota(jnp.int32, x.shape, 0) == 0
        prev = jnp.where(first, jnp.broadcast_to(carry_row, x.shape), rolled)
        return x + (prev - x) * mu, x[tm - 1:tm, :]

    r, r_last = shift(r_ref[...], prev_ref[0:1, :], mu_ref[0:1, :])
    k, k_last = shift(k_ref[...], prev_ref[1:2, :], mu_ref[1:2, :])
    v, v_last = shift(v_ref[...], prev_ref[2:3, :], mu_ref[2:3, :])
    lo, l_last = shift(l_ref[...], prevl_ref[0:1, :], mul_ref[...])
    prev_ref[0:1, :] = r_last
    prev_ref[1:2, :] = k_last
    prev_ref[2:3, :] = v_last
    prevl_ref[0:1, :] = l_last

    lane = lax.broadcasted_iota(jnp.int32, lo.shape, 1)
    lt = jnp.where(lane < W_LORA, jnp.tanh(lo),
                   jnp.where((lane >= W_LORA + A_LORA) & (lane < W_LORA + A_LORA + G_LORA),
                             jax.nn.sigmoid(lo), lo))
    up = jnp.dot(lt.astype(BF16), wl_ref[...], preferred_element_type=F32)
    w0, a0, v0 = par_ref[0:1, :], par_ref[1:2, :], par_ref[2:3, :]
    k_k, k_a = par_ref[3:4, :], par_ref[4:5, :]

    y = -(w0 + up[:, 0:W_B])
    softplus = jnp.maximum(y, 0.0) + jnp.log(1.0 + jnp.exp(-jnp.abs(y)))
    w_log = -softplus - 0.5
    lw = -jnp.exp(w_log)
    a = jax.nn.sigmoid(a0 + up[:, W_B:2 * W_B])
    go_ref[...] = up[:, 2 * W_B:3 * W_B]
    if has_vres:
        v = v + (vf_ref[...] - v) * jax.nn.sigmoid(v0 + up[:, 3 * W_B:4 * W_B])
    kk = k * k_k
    norm = jnp.sqrt(_group_sum(kk * kk, g_ref))
    kk = kk / jnp.maximum(norm, 1e-12)
    k = k * (1.0 + (a - 1.0) * k_a)
    ro_ref[...] = r
    ko_ref[...] = k
    vo_ref[...] = v

    p0, p1, p2 = _split3(lw)
    tri, same = tri_ref[...], same_ref[...]
    cs = (jnp.dot(tri, p0, preferred_element_type=F32) + jnp.dot(tri, p1, preferred_element_type=F32)
          + jnp.dot(tri, p2, preferred_element_type=F32))
    cl = (jnp.dot(same, p0, preferred_element_type=F32) + jnp.dot(same, p1, preferred_element_type=F32)
          + jnp.dot(same, p2, preferred_element_type=F32))
    e_neg = jnp.exp(-cs)
    to_end = jnp.exp(cl - cs)
    kka = kk * a
    at_ref[...] = (-kk * jnp.exp(cs - lw)).astype(BF16)
    rt_ref[...] = (r * jnp.exp(cs)).astype(BF16)
    bt_ref[...] = (kka * e_neg).astype(BF16)
    kt_ref[...] = (k * e_neg).astype(BF16)
    bt2_ref[...] = (kka * to_end).astype(BF16)
    kt2_ref[...] = (k * to_end).astype(BF16)
    gl_ref[...] = jnp.exp(cl)


def _chunk_matrices(tm):
    t = np.arange(tm)
    same = (t[:, None] // CHUNK) == (t[None, :] // CHUNK)
    tri = same & (t[None, :] <= t[:, None])
    return jnp.asarray(tri, BF16), jnp.asarray(same, BF16)


def rwkv_prep(proj3, v_first, mu_rkv, mu_lora, params, w_lora, gmat, tm=256):
    B, S, _ = proj3.shape
    has_vres = v_first is not None
    tok = lambda c: pl.BlockSpec((None, tm, W_B), lambda b, s: (b, s, c))
    full = lambda shp: pl.BlockSpec(shp, lambda b, s: (0,) * len(shp))
    in_specs = [tok(OFF_RKV // W_B), tok(OFF_RKV // W_B + 1), tok(OFF_RKV // W_B + 2),
                pl.BlockSpec((None, tm, LORA_W), lambda b, s: (b, s, OFF_LORA // LORA_W))]
    args = [proj3, proj3, proj3, proj3]
    if has_vres:
        in_specs.append(tok(0))
        args.append(v_first)
    in_specs += [full((3, W_B)), full((1, LORA_W)), full((8, W_B)), full((LORA_W, 4 * W_B)),
                 full((W_B, W_B)), full((tm, tm)), full((tm, tm))]
    args += [mu_rkv, mu_lora, params, w_lora, gmat, *_chunk_matrices(tm)]
    out32 = jax.ShapeDtypeStruct((B, S, W_B), F32)
    out16 = jax.ShapeDtypeStruct((B, S, W_B), BF16)
    return pl.pallas_call(
        functools.partial(_rwkv_prep_kernel, has_vres=has_vres),
        out_shape=(out32,) * 5 + (out16,) * 6,
        grid=(B, S // tm),
        in_specs=in_specs,
        out_specs=(tok(0),) * 11,
        scratch_shapes=[pltpu.VMEM((8, W_B), F32), pltpu.VMEM((8, LORA_W), F32)],
        compiler_params=_cparams(("parallel", "arbitrary")),
        name="rwkv_prep",
    )(*args)


SCAN_HPG = 16


def _rwkv_scan_kernel(at_ref, rt_ref, bt_ref, kt_ref, bt2_ref, kt2_ref, v_ref, gl_ref, o_ref,
                      state_ref):
    C, N, H = CHUNK, N_B, SCAN_HPG

    @pl.when(pl.program_id(2) == 0)
    def _():
        state_ref[...] = jnp.zeros_like(state_ref)

    row = lax.broadcasted_iota(jnp.int32, (C, C), 0)
    col = lax.broadcasted_iota(jnp.int32, (C, C), 1)
    incl = (col <= row).astype(F32)
    strict = (col < row).astype(F32)
    eye = (col == row).astype(F32)
    nt = (((1,), (1,)), ((), ()))
    dot = functools.partial(jnp.dot, preferred_element_type=F32)
    dot_nt = lambda x, y: lax.dot_general(x, y, nt, preferred_element_type=F32)
    hs = range(H)
    sl = [slice(h * N, (h + 1) * N) for h in hs]

    x = [jnp.concatenate([at_ref[:, sl[h]], rt_ref[:, sl[h]]], axis=0) for h in hs]
    v = [v_ref[:, sl[h]] for h in hs]
    vb = [v[h].astype(BF16) for h in hs]
    st = [state_ref[h] for h in hs]
    mb = [dot_nt(x[h], bt_ref[:, sl[h]]) for h in hs]
    mk = [dot_nt(x[h], kt_ref[:, sl[h]]) for h in hs]
    xh = [dot_nt(x[h], st[h].astype(BF16)) for h in hs]
    lab = [mb[h][:C] * strict for h in hs]
    mrb = [(mb[h][C:] * incl).astype(BF16) for h in hs]
    lm = [jnp.concatenate([mk[h][:C] * strict, mk[h][C:] * incl], axis=0).astype(BF16) for h in hs]
    lv = [dot(lm[h], vb[h]) for h in hs]
    t = [eye + lab[h] for h in hs]
    p = lab
    for _ in range(5):
        pb = [p[h].astype(BF16) for h in hs]
        p = [dot(pb[h], pb[h]) for h in hs]
        t = [t[h] + dot(t[h].astype(BF16), p[h].astype(BF16)) for h in hs]
    u = [dot(t[h].astype(BF16), (xh[h][:C] + lv[h][:C]).astype(BF16)) for h in hs]
    o = [xh[h][C:] + dot(mrb[h], u[h].astype(BF16)) + lv[h][C:] for h in hs]
    for h in hs:
        o_ref[:, sl[h]] = o[h]
    uv_t = []
    for h in hs:
        uv = jnp.concatenate([u[h], v[h]], axis=0)
        uv_t.append(jnp.concatenate([uv, jnp.zeros_like(uv)], axis=1).T[:N].astype(BF16))
    for h in hs:
        y2 = jnp.concatenate([bt2_ref[:, sl[h]], kt2_ref[:, sl[h]]], axis=0)
        state_ref[h] = st[h] * gl_ref[0:1, sl[h]] + dot(uv_t[h], y2)


def rwkv_scan(at, rt, bt, kt, bt2, kt2, v, gl):
    B, S, _ = v.shape
    wb = SCAN_HPG * N_B
    spec = pl.BlockSpec((None, CHUNK, wb), lambda b, g, c: (b, c, g))
    return pl.pallas_call(
        _rwkv_scan_kernel,
        out_shape=jax.ShapeDtypeStruct((B, S, W_B), F32),
        grid=(B, H_B // SCAN_HPG, S // CHUNK),
        in_specs=[spec] * 8,
        out_specs=spec,
        scratch_shapes=[pltpu.VMEM((SCAN_HPG, N_B, N_B), F32)],
        compiler_params=_cparams(("parallel", "parallel", "arbitrary")),
        name="rwkv_scan",
    )(at, rt, bt, kt, bt2, kt2, v, gl)


def _rwkv_post_kernel(o_ref, r_ref, k_ref, v_ref, g_ref, par_ref, gm_ref, out_ref):
    o = o_ref[...]
    mu = _group_sum(o, gm_ref) * (1.0 / N_B)
    oc = o - mu
    var = _group_sum(oc * oc, gm_ref) * (1.0 / N_B)
    on = oc * lax.rsqrt(var + LNX_EPS) * par_ref[0:1, :] + par_ref[1:2, :]
    bonus = _group_sum(r_ref[...] * k_ref[...] * par_ref[2:3, :], gm_ref) * v_ref[...]
    out_ref[...] = ((on + bonus) * g_ref[...]).astype(BF16)


def rwkv_post(o, r, k, v, g, params, gmat, tm=256):
    M = o.shape[0]
    row = pl.BlockSpec((tm, W_B), lambda i: (i, 0))
    return pl.pallas_call(
        _rwkv_post_kernel,
        out_shape=jax.ShapeDtypeStruct((M, W_B), BF16),
        grid=(M // tm,),
        in_specs=[row] * 5 + [pl.BlockSpec((8, W_B), lambda i: (0, 0)),
                              pl.BlockSpec((W_B, W_B), lambda i: (0, 0))],
        out_specs=row,
        compiler_params=_cparams(("parallel",)),
        name="rwkv_post",
    )(o, r, k, v, g, params, gmat)


def _merge_kernel(oa_ref, ob_ref, oc_ref, wa_ref, wb_ref, wc_ref, g0_ref, g1_ref, g2_ref, out_ref):
    ya = jnp.dot(oa_ref[...], wa_ref[...], preferred_element_type=F32)
    yb = jnp.dot(ob_ref[...], wb_ref[...], preferred_element_type=F32)
    yc = jnp.dot(oc_ref[...], wc_ref[...], preferred_element_type=F32)
    out = (jax.nn.sigmoid(g0_ref[...]) * ya + jax.nn.sigmoid(g1_ref[...]) * yb
           + jax.nn.sigmoid(g2_ref[...]) * yc)
    out_ref[...] = out.astype(BF16)


def gated_merge(o_a, o_b, o_c, w_pa, w_pb, w_pc, proj2, tm=512, tn=1024):
    M = o_a.shape[0]
    D = D_MODEL
    nj = D // tn
    act = lambda w: pl.BlockSpec((tm, w), lambda i, j: (i, 0))
    wgt = lambda w: pl.BlockSpec((w, tn), lambda i, j: (0, j))
    gate = lambda br: pl.BlockSpec((tm, tn), lambda i, j: (i, OFF_GATE // tn + br * nj + j))
    return pl.pallas_call(
        _merge_kernel,
        out_shape=jax.ShapeDtypeStruct((M, D), BF16),
        grid=(M // tm, nj),
        in_specs=[act(W_A), act(W_B), act(W_M), wgt(W_A), wgt(W_B), wgt(W_M),
                  gate(0), gate(1), gate(2)],
        out_specs=pl.BlockSpec((tm, tn), lambda i, j: (i, j)),
        compiler_params=_cparams(("parallel", "arbitrary")),
        name="gated_merge",
    )(o_a, o_b, o_c, w_pa, w_pb, w_pc, proj2, proj2, proj2)


def _first_index_of_max(x, idx, n):
    m = jnp.max(x, axis=0, keepdims=True)
    first = jnp.min(jnp.where(x == m, idx, n), axis=0, keepdims=True)
    return m, idx == first


def _router_kernel(h_ref, wr_ref, bias_ref, gate_ref):
    E, G = N_EXPERTS, N_GROUPS
    per = E // G
    logits = lax.dot_general(wr_ref[...], h_ref[...], (((1,), (1,)), ((), ())),
                             preferred_element_type=F32)
    tm = logits.shape[1]
    scores = jax.nn.sigmoid(logits)
    biased = scores + jnp.concatenate([bias_ref[...]] * (tm // 128), axis=1)
    idx8 = lax.broadcasted_iota(jnp.int32, (per, tm), 0)
    gscores = []
    for g in range(G):
        xg = biased[g * per:(g + 1) * per, :]
        m1, hit = _first_index_of_max(xg, idx8, per)
        m2 = jnp.max(jnp.where(hit, -jnp.inf, xg), axis=0, keepdims=True)
        gscores.append(m1 + m2)
    gs = jnp.concatenate(gscores, axis=0)
    gidx = lax.broadcasted_iota(jnp.int32, (G, tm), 0)
    gsel = jnp.zeros((G, tm), F32)
    for _ in range(TOPK_GROUPS):
        _, hit = _first_index_of_max(gs, gidx, G)
        gsel = jnp.where(hit, 1.0, gsel)
        gs = jnp.where(hit, -jnp.inf, gs)
    emask = jnp.concatenate(
        [jnp.broadcast_to(gsel[g:g + 1, :], (per, tm)) for g in range(G)], axis=0)
    cand = jnp.where(emask > 0.0, biased, -jnp.inf)
    eidx = lax.broadcasted_iota(jnp.int32, (E, tm), 0)
    picked = jnp.zeros((E, tm), F32)
    for _ in range(TOP_K):
        _, hit = _first_index_of_max(cand, eidx, E)
        picked = jnp.where(hit, scores, picked)
        cand = jnp.where(hit, -jnp.inf, cand)
    gate = picked / jnp.sum(picked, axis=0, keepdims=True) * ROUTED_SCALE
    shared = (eidx == 0).astype(F32)
    gate_ref[...] = jnp.concatenate([gate, shared], axis=0).T


def moe_router(hb, w_router_t, bias_b, tm=256):
    M, D = hb.shape
    return pl.pallas_call(
        _router_kernel,
        out_shape=jax.ShapeDtypeStruct((M, 2 * N_EXPERTS), F32),
        grid=(M // tm,),
        in_specs=[pl.BlockSpec((tm, D), lambda i: (i, 0)),
                  pl.BlockSpec((N_EXPERTS, D), lambda i: (0, 0)),
                  pl.BlockSpec((N_EXPERTS, 128), lambda i: (0, 0))],
        out_specs=pl.BlockSpec((tm, 2 * N_EXPERTS), lambda i: (i, 0)),
        compiler_params=_cparams(("parallel",)),
        name="moe_router",
    )(hb, w_router_t, bias_b)


MOE_F = 6656


def _moe_act_kernel(x_ref, gate_ref, e_ref, w1_ref, w3_ref, o_ref):
    x = x_ref[...]
    h1 = jnp.dot(x, w1_ref[...], preferred_element_type=F32)
    h3 = jnp.dot(x, w3_ref[...], preferred_element_type=F32)
    g0, g1, g2 = _split3(gate_ref[...])
    e = e_ref[...]
    gexp = (jnp.dot(g0, e, preferred_element_type=F32) + jnp.dot(g1, e, preferred_element_type=F32)
            + jnp.dot(g2, e, preferred_element_type=F32))
    o_ref[...] = (h1 * jax.nn.sigmoid(h1) * h3 * gexp).astype(BF16)


def moe_hidden(hb, gate, expand, w1c, w3c, tm=1024, tn=512):
    M, D = hb.shape
    wspec = pl.BlockSpec((D, tn), lambda i, j: (0, j))
    return pl.pallas_call(
        _moe_act_kernel,
        out_shape=jax.ShapeDtypeStruct((M, MOE_F), BF16),
        grid=(M // tm, MOE_F // tn),
        in_specs=[pl.BlockSpec((tm, D), lambda i, j: (i, 0)),
                  pl.BlockSpec((tm, 2 * N_EXPERTS), lambda i, j: (i, 0)),
                  pl.BlockSpec((2 * N_EXPERTS, tn), lambda i, j: (0, j)),
                  wspec, wspec],
        out_specs=pl.BlockSpec((tm, tn), lambda i, j: (i, j)),
        compiler_params=_cparams(("parallel", "arbitrary")),
        name="moe_hidden",
    )(hb, gate, expand, w1c, w3c)


def _regroup_w_in(w, has_vres):
    D = w.shape[0]
    o = np.cumsum((0, W_A, R_KV, H_IDX * D_IDX, D_IDX, H_IDX, 3 * W_B, W_LORA, A_LORA, G_LORA, W_M,
                   3 * D_MODEL)).tolist()
    q_a, c_kv, q_idx = w[:, o[0]:o[1]], w[:, o[1]:o[2]], w[:, o[2]:o[3]]
    idx = w[:, o[3]:o[5]]
    rkv, lora = w[:, o[5]:o[6]], w[:, o[6]:o[9]]
    q_m, gates = w[:, o[9]:o[10]], w[:, o[10]:o[11]]
    zeros = lambda n: jnp.zeros((D, n), w.dtype)
    vres = w[:, o[11]:o[11] + V_LORA] if has_vres else zeros(V_LORA)
    n_lora = W_LORA + A_LORA + G_LORA + V_LORA
    parts = [gates, q_a, q_idx, rkv, q_m, c_kv, lora, vres, zeros(LORA_W - n_lora),
             idx, zeros(IDX_W - D_IDX - H_IDX)]
    used = sum(p.shape[1] for p in parts)
    parts.append(zeros(N_PROJ - used))
    return jnp.concatenate(parts, axis=1).astype(BF16)


def _lora_up_matrix(w_up, a_up, g_up, v_up):
    m = jnp.zeros((LORA_W, 4 * W_B), F32)
    m = m.at[0:W_LORA, 0:W_B].set(w_up)
    m = m.at[W_LORA:W_LORA + A_LORA, W_B:2 * W_B].set(a_up)
    m = m.at[W_LORA + A_LORA:W_LORA + A_LORA + G_LORA, 2 * W_B:3 * W_B].set(g_up)
    if v_up is not None:
        r0 = W_LORA + A_LORA + G_LORA
        m = m.at[r0:r0 + V_LORA, 3 * W_B:4 * W_B].set(v_up)
    return m.astype(BF16)


def _rows(vectors, n_rows, width):
    m = jnp.zeros((n_rows, width), F32)
    for i, vec in enumerate(vectors):
        m = m.at[i, :vec.shape[0]].set(vec)
    return m


def _expert_in_weights(w_e, w_s):
    D = w_e.shape[1]
    routed = jnp.transpose(w_e, (1, 0, 2)).reshape(D, N_EXPERTS * F_EXPERT)
    pad = jnp.zeros((D, MOE_F - N_EXPERTS * F_EXPERT - w_s.shape[1]), w_e.dtype)
    return jnp.concatenate([routed, w_s, pad], axis=1).astype(BF16)


def _expert_out_weights(w_e2, w_s2):
    D = w_e2.shape[2]
    routed = w_e2.reshape(N_EXPERTS * F_EXPERT, D)
    pad = jnp.zeros((MOE_F - N_EXPERTS * F_EXPERT - w_s2.shape[0], D), w_e2.dtype)
    return jnp.concatenate([routed, w_s2, pad], axis=0).astype(BF16)


def kernel(x, mem, ln0_g, ln0_b, w_in_first, w_in_rest, ckv_g, w_uk, w_uv, mu_rwkv, mu_vres, rw_w0, rw_w_up, rw_a0, rw_a_up, rw_g_up, rw_v0, rw_v_up, rw_k_k, rw_k_a, rw_r_k, rw_lnx_g, rw_lnx_b, w_mk, w_mv, w_pa, w_pb, w_pc, w_o, ln1_g, ln1_b, w_router, router_bias, w_e1, w_e3, w_e2, w_s1, w_s3, w_s2, ln2_g, ln2_b):
    B, S, D = x.shape
    T = B * S
    gmat = jnp.kron(jnp.eye(H_B, dtype=F32), jnp.ones((N_B, N_B), F32)).astype(BF16)
    unit = np.arange(MOE_F)
    owner = np.where(unit < (N_EXPERTS + 1) * F_EXPERT, unit // F_EXPERT, -1)
    expand = jnp.asarray(owner[None, :] == np.arange(2 * N_EXPERTS)[:, None], BF16)
    mem_b = mem.reshape(B * N_MEM, D).astype(BF16)

    h, hb = layer_norm(x.reshape(T, D), ln0_g, ln0_b)
    v_first = None
    for i in range(DEPTH):
        has_vres = i > 0
        w_in = _regroup_w_in(w_in_first if i == 0 else w_in_rest[i - 1], has_vres)
        proj2 = matmul(hb, w_in, F32, 1024, 1024)
        proj3 = proj2.reshape(B, S, N_PROJ)

        k2, v2 = kv_project(proj2, ckv_g[i], w_uk[i].astype(BF16), w_uv[i].astype(BF16))
        bias5 = indexer_bias(proj3)
        o_a = sparse_attention(proj3, k2.reshape(B, S, W_A), v2.reshape(B, S, W_A), bias5)

        mu = mu_rwkv[i]
        mu_rkv = mu[:3 * W_B].reshape(3, W_B)
        mu_l = [mu[3 * W_B:]] + ([mu_vres[i - 1]] if has_vres else [])
        mu_lora = _rows([jnp.concatenate(mu_l)], 1, LORA_W)
        prep_par = _rows([rw_w0[i], rw_a0[i], rw_v0[i - 1] if has_vres else jnp.zeros((W_B,), F32),
                          rw_k_k[i], rw_k_a[i]], 8, W_B)
        w_lora = _lora_up_matrix(rw_w_up[i], rw_a_up[i], rw_g_up[i],
                                 rw_v_up[i - 1] if has_vres else None)
        r_b, k_b, v_b, g_b, gl_b, *scan_ops = rwkv_prep(proj3, v_first, mu_rkv, mu_lora, prep_par,
                                                        w_lora, gmat)
        if i == 0:
            v_first = v_b
        o_scan = rwkv_scan(*scan_ops, v_b, gl_b)
        post_par = _rows([rw_lnx_g[i], rw_lnx_b[i], rw_r_k[i].reshape(W_B)], 8, W_B)
        flat = lambda z: z.reshape(T, W_B)
        o_b = rwkv_post(flat(o_scan), flat(r_b), flat(k_b), flat(v_b), flat(g_b), post_par, gmat)

        km = matmul(mem_b, w_mk[i].astype(BF16), BF16, 512, 512).reshape(B, N_MEM, W_M)
        vm = matmul(mem_b, w_mv[i].astype(BF16), BF16, 512, 512).reshape(B, N_MEM, W_M)
        o_c = memory_attention(proj3, km, vm)

        merged = gated_merge(o_a.reshape(T, W_A), o_b, o_c.reshape(T, W_M), w_pa[i].astype(BF16),
                             w_pb[i].astype(BF16), w_pc[i].astype(BF16), proj2)
        h, hb = matmul_residual_ln(merged, w_o[i].astype(BF16), h, ln1_g[i], ln1_b[i])

        bias_b = jnp.broadcast_to(router_bias[i].reshape(N_EXPERTS, 1), (N_EXPERTS, 128))
        gate = moe_router(hb, w_router[i].T.astype(BF16), bias_b)
        hidden = moe_hidden(hb, gate, expand, _expert_in_weights(w_e1[i], w_s1[i]),
                            _expert_in_weights(w_e3[i], w_s3[i]))
        h, hb = matmul_residual_ln(hidden, _expert_out_weights(w_e2[i], w_s2[i]), h,
                                   ln2_g[i], ln2_b[i])
    return h.reshape(B, S, D)
```

```python
import functools

import jax
import jax.numpy as jnp
import numpy as np
from jax import lax
from jax.experimental import pallas as pl
from jax.experimental.pallas import tpu as pltpu

F32 = jnp.float32
BF16 = jnp.bfloat16

D_MODEL = 4096
DEPTH = 4
CHUNK = 64
H_A, DH_A = 16, 128
W_A = H_A * DH_A
R_KV = 256
H_IDX, D_IDX = 32, 64
TOPK_MAX = 256
H_B, N_B = 16, 64
W_B = H_B * N_B
W_LORA, A_LORA, G_LORA, V_LORA = 64, 64, 160, 32
LNX_EPS = 64e-5
N_MEM = 256
H_M, DH_M = 4, 256
W_M = H_M * DH_M
N_EXPERTS, N_GROUPS, TOPK_GROUPS, TOP_K = 64, 8, 4, 8
F_EXPERT = 96
ROUTED_SCALE = 2.5
ALPHA = (2 * DEPTH) ** 0.25
LN_EPS = 1e-5

OFF_GATE = 0
OFF_QA = 3 * D_MODEL
OFF_QIDX = OFF_QA + W_A
OFF_RKV = OFF_QIDX + H_IDX * D_IDX
OFF_QM = OFF_RKV + 3 * W_B
OFF_CKV = OFF_QM + W_M
OFF_LORA = OFF_CKV + R_KV
LORA_W = 384
OFF_IDX = OFF_LORA + LORA_W
IDX_W = 128
N_PROJ = 21504

NEG_BIAS = -1e30
LANES = 128
VMEM_LIMIT = 56 * 1024 * 1024


def _cparams(sem):
    return pltpu.CompilerParams(dimension_semantics=sem, vmem_limit_bytes=VMEM_LIMIT)


def _split2(x):
    hi = x.astype(BF16)
    lo = (x - hi.astype(F32)).astype(BF16)
    return hi, lo


def _split3(x):
    hi = x.astype(BF16)
    r1 = x - hi.astype(F32)
    mid = r1.astype(BF16)
    lo = (r1 - mid.astype(F32)).astype(BF16)
    return hi, mid, lo


def _mm_kernel(x_ref, w_ref, o_ref):
    o_ref[...] = jnp.dot(x_ref[...], w_ref[...], preferred_element_type=F32).astype(o_ref.dtype)


def matmul(x, w, out_dtype, tm, tn):
    M, K = x.shape
    N = w.shape[1]
    tm, tn = min(tm, M), min(tn, N)
    assert M % tm == 0 and N % tn == 0
    return pl.pallas_call(
        _mm_kernel,
        out_shape=jax.ShapeDtypeStruct((M, N), out_dtype),
        grid=(M // tm, N // tn),
        in_specs=[pl.BlockSpec((tm, K), lambda i, j: (i, 0)),
                  pl.BlockSpec((K, tn), lambda i, j: (0, j))],
        out_specs=pl.BlockSpec((tm, tn), lambda i, j: (i, j)),
        compiler_params=_cparams(("parallel", "arbitrary")),
        name="matmul",
    )(x, w)


def _ln_rows(y, g, b):
    mu = jnp.mean(y, axis=-1, keepdims=True)
    yc = y - mu
    var = jnp.mean(yc * yc, axis=-1, keepdims=True)
    return yc * lax.rsqrt(var + LN_EPS) * g + b


def _ln_kernel(x_ref, g_ref, b_ref, o_ref, ob_ref):
    y = _ln_rows(x_ref[...], g_ref[...], b_ref[...])
    o_ref[...] = y
    ob_ref[...] = y.astype(BF16)


def layer_norm(x, g, b, tm=256):
    M, D = x.shape
    row = pl.BlockSpec((tm, D), lambda i: (i, 0))
    vec = pl.BlockSpec((1, D), lambda i: (0, 0))
    return pl.pallas_call(
        _ln_kernel,
        out_shape=(jax.ShapeDtypeStruct((M, D), F32), jax.ShapeDtypeStruct((M, D), BF16)),
        grid=(M // tm,),
        in_specs=[row, vec, vec],
        out_specs=(row, row),
        compiler_params=_cparams(("parallel",)),
        name="layer_norm",
    )(x, g.reshape(1, D), b.reshape(1, D))


def _mm_res_ln_kernel(x_ref, w_ref, h_ref, g_ref, b_ref, o_ref, ob_ref, *, nj, tn):
    j = pl.program_id(1)
    y = jnp.dot(x_ref[...], w_ref[...], preferred_element_type=F32)
    for jj in range(nj):
        @pl.when(j == jj)
        def _(jj=jj):
            o_ref[:, jj * tn:(jj + 1) * tn] = y

    @pl.when(j == nj - 1)
    def _():
        D = nj * tn
        s1 = None
        for jj in range(nj):
            sl = slice(jj * tn, (jj + 1) * tn)
            z = ALPHA * h_ref[:, sl] + o_ref[:, sl]
            o_ref[:, sl] = z
            p = jnp.sum(z, axis=-1, keepdims=True)
            s1 = p if s1 is None else s1 + p
        mu = s1 * (1.0 / D)
        s2 = None
        for jj in range(nj):
            zc = o_ref[:, jj * tn:(jj + 1) * tn] - mu
            p = jnp.sum(zc * zc, axis=-1, keepdims=True)
            s2 = p if s2 is None else s2 + p
        rstd = lax.rsqrt(s2 * (1.0 / D) + LN_EPS)
        for jj in range(nj):
            sl = slice(jj * tn, (jj + 1) * tn)
            z = (o_ref[:, sl] - mu) * rstd * g_ref[:, sl] + b_ref[:, sl]
            o_ref[:, sl] = z
            ob_ref[:, sl] = z.astype(BF16)


def matmul_residual_ln(x, w, h, g, b, tm=512, tn=512):
    M, K = x.shape
    D = w.shape[1]
    nj = D // tn
    row = pl.BlockSpec((tm, D), lambda i, j: (i, 0), pipeline_mode=pl.Buffered(1))
    vec = pl.BlockSpec((1, D), lambda i, j: (0, 0))
    return pl.pallas_call(
        functools.partial(_mm_res_ln_kernel, nj=nj, tn=tn),
        out_shape=(jax.ShapeDtypeStruct((M, D), F32), jax.ShapeDtypeStruct((M, D), BF16)),
        grid=(M // tm, nj),
        in_specs=[pl.BlockSpec((tm, K), lambda i, j: (i, 0), pipeline_mode=pl.Buffered(1)),
                  pl.BlockSpec((K, tn), lambda i, j: (0, j)),
                  row, vec, vec],
        out_specs=(row, row),
        compiler_params=_cparams(("parallel", "arbitrary")),
        name="matmul_residual_ln",
    )(x, w, h, g.reshape(1, D), b.reshape(1, D))


def _kv_kernel(c_ref, g_ref, wk_ref, wv_ref, k_ref, v_ref):
    c = c_ref[...]
    ms = jnp.mean(c * c, axis=-1, keepdims=True)
    cn = (c * lax.rsqrt(ms + LN_EPS) * g_ref[...]).astype(BF16)
    k_ref[...] = jnp.dot(cn, wk_ref[...], preferred_element_type=F32).astype(BF16)
    v_ref[...] = jnp.dot(cn, wv_ref[...], preferred_element_type=F32).astype(BF16)


def kv_project(proj2, ckv_g, w_uk, w_uv, tm=512):
    M = proj2.shape[0]
    out = jax.ShapeDtypeStruct((M, W_A), BF16)
    wspec = pl.BlockSpec((R_KV, W_A), lambda i: (0, 0))
    ospec = pl.BlockSpec((tm, W_A), lambda i: (i, 0))
    return pl.pallas_call(
        _kv_kernel,
        out_shape=(out, out),
        grid=(M // tm,),
        in_specs=[pl.BlockSpec((tm, R_KV), lambda i: (i, OFF_CKV // R_KV)),
                  pl.BlockSpec((1, R_KV), lambda i: (0, 0)), wspec, wspec],
        out_specs=(ospec, ospec),
        compiler_params=_cparams(("parallel",)),
        name="kv_project",
    )(proj2, ckv_g.reshape(1, R_KV), w_uk, w_uv)


IDX_TQ = 128
IDX_TK = 256
KEY_NEG_INF = -2139095041
INT_MIN = -2147483648


def _indexer_kernel(q_ref, iq_ref, ik_ref, below_ref, bias_ref, qt_ref, wt_ref, key_ref, *, n_kt,
                    ksel):
    tq, tk = IDX_TQ, IDX_TK
    i = pl.program_id(1)
    qt_ref[...] = (q_ref[...].T * (D_IDX ** -0.5)).astype(BF16)
    wt_ref[...] = iq_ref[...].T * (H_IDX ** -0.5)

    n_t = ((i + 1) * tq + tk - 1) // tk
    q_chunk = (i * tq + lax.broadcasted_iota(jnp.int32, (tk, tq), 1)) // CHUNK
    row_iota = lax.broadcasted_iota(jnp.int32, (tk, tq), 0)

    def score_tile(j, carry):
        kt = ik_ref[pl.ds(pl.multiple_of(j * tk, tk), tk), :][:, :D_IDX].astype(BF16)
        acc = jnp.zeros((tk, tq), F32)
        for h2 in range(H_IDX // 2):
            w2 = jnp.concatenate([qt_ref[(2 * h2 + t) * D_IDX:(2 * h2 + t + 1) * D_IDX, :]
                                  for t in range(2)], axis=1)
            z = jnp.dot(kt, w2, preferred_element_type=F32)
            for t in range(2):
                h = 2 * h2 + t
                acc = acc + wt_ref[D_IDX + h:D_IDX + h + 1, :] * jnp.maximum(
                    z[:, t * tq:(t + 1) * tq], 0.0)
        admissible = ((j * tk + row_iota) // CHUNK) <= q_chunk
        score = jnp.where(admissible, acc, -jnp.inf)
        bits = lax.bitcast_convert_type(score, jnp.int32)
        key_ref[j] = bits ^ ((bits >> 31) & 0x7FFFFFFF)
        return carry

    lax.fori_loop(0, n_t, score_tile, 0)

    def count_keys(pred):
        def count_tile(j, cnts):
            c = pred(key_ref[j]).astype(jnp.int32)
            cnts = list(cnts)
            for r in range(tk // 8):
                cnts[r % len(cnts)] = cnts[r % len(cnts)] + c[r * 8:(r + 1) * 8, :]
            return tuple(cnts)

        cnts = lax.fori_loop(0, n_t, count_tile, (jnp.zeros((8, tq), jnp.int32),) * 4)
        cnt = (cnts[0] + cnts[1]) + (cnts[2] + cnts[3])
        return jnp.sum(cnt.astype(F32), axis=0, keepdims=True)

    def bit_step(it, thr):
        cand = thr + lax.shift_left(jnp.int32(1), 31 - it)
        return jnp.where(count_keys(lambda key: key >= cand) >= float(ksel), cand, thr)

    thr = lax.fori_loop(0, 32, bit_step, jnp.full((1, tq), INT_MIN, jnp.int32))
    ties_needed = float(ksel) - count_keys(lambda key: key > thr)
    below = below_ref[...]

    def bias_tile(j, ties_seen):
        key = key_ref[j]
        tied = key == thr
        tied_b = jnp.where(tied, 1.0, 0.0).astype(BF16)
        rank = ties_seen + jnp.dot(below, tied_b, preferred_element_type=F32)
        sel = ((key > thr) | (tied & (rank < ties_needed))) & (key > KEY_NEG_INF)
        bias_ref[j] = jnp.where(sel, 0.0, NEG_BIAS).T.astype(BF16)
        return ties_seen + jnp.sum(tied_b.astype(F32), axis=0, keepdims=True)

    def bias_pair(jj, ties_seen):
        return bias_tile(2 * jj + 1, bias_tile(2 * jj, ties_seen))

    ties_seen = lax.fori_loop(0, n_t // 2, bias_pair, jnp.zeros((1, tq), F32))

    @pl.when(n_t % 2 == 1)
    def _():
        bias_tile(n_t - 1, ties_seen)

    def fill_tile(j, carry):
        bias_ref[j] = jnp.full((tq, tk), NEG_BIAS, BF16)
        return carry

    lax.fori_loop(n_t, n_kt, fill_tile, 0)


def indexer_bias(proj3):
    B, S, _ = proj3.shape
    tq, tk = IDX_TQ, IDX_TK
    n_qt, n_kt = S // tq, S // tk
    ksel = min(TOPK_MAX, S // 4)
    below = jnp.asarray(np.tril(np.ones((tk, tk)), -1), BF16)
    return pl.pallas_call(
        functools.partial(_indexer_kernel, n_kt=n_kt, ksel=ksel),
        out_shape=jax.ShapeDtypeStruct((B, n_qt, n_kt, tq, tk), BF16),
        grid=(B, n_qt),
        in_specs=[pl.BlockSpec((None, tq, H_IDX * D_IDX), lambda b, i: (b, i, OFF_QIDX // (H_IDX * D_IDX))),
                  pl.BlockSpec((None, tq, IDX_W), lambda b, i: (b, i, OFF_IDX // IDX_W)),
                  pl.BlockSpec((None, S, IDX_W), lambda b, i: (b, 0, OFF_IDX // IDX_W)),
                  pl.BlockSpec((tk, tk), lambda b, i: (0, 0))],
        out_specs=pl.BlockSpec((None, None, n_kt, tq, tk), lambda b, i: (b, i, 0, 0, 0)),
        scratch_shapes=[pltpu.VMEM((H_IDX * D_IDX, tq), BF16),
                        pltpu.VMEM((IDX_W, tq), F32),
                        pltpu.VMEM((n_kt, tk, tq), jnp.int32)],
        compiler_params=_cparams(("parallel", "parallel")),
        name="indexer_bias",
    )(proj3, proj3, proj3, below)


ATT_TQ = 256
ATT_HP = 4
ATT_TK = 512


ATT_RG = 16


def _attention_kernel(q_ref, k_ref, v_ref, bias_ref, o_ref, qs_ref, s_ref, p_ref, m_ref, a_ref,
                      acc_ref):
    tq, tk, hp = ATT_TQ, ATT_TK, ATT_HP
    nq = tq // IDX_TQ
    nb = tk // IDX_TK
    i = pl.program_id(2)
    scale = DH_A ** -0.5 * np.log2(np.e)
    nt = (((1,), (1,)), ((), ()))
    streams = [(hh, qq) for hh in range(hp) for qq in range(nq)]
    cols = [slice(hh * DH_A, (hh + 1) * DH_A) for hh in range(hp)]
    rows = [slice(qq * IDX_TQ, (qq + 1) * IDX_TQ) for qq in range(nq)]
    for c, (hh, qq) in enumerate(streams):
        qs_ref[c] = (q_ref[rows[qq], cols[hh]] * scale).astype(BF16)
    m_ref[...] = jnp.full(m_ref.shape, NEG_BIAS, F32)
    acc_ref[...] = jnp.zeros_like(acc_ref)
    p_ref[1] = jnp.zeros(p_ref.shape[1:], BF16)
    a_ref[1] = jnp.zeros(a_ref.shape[1:], F32)
    n_kv = k_ref.shape[0] // tk
    n_t = ((i + 1) * tq + tk - 1) // tk
    ones = jnp.ones((tk, DH_A), BF16)

    def logits(j, slot):
        row0 = pl.multiple_of(j * tk, tk)
        for c, (hh, qq) in enumerate(streams):
            s_ref[slot, c] = lax.dot_general(qs_ref[c], k_ref[pl.ds(row0, tk), cols[hh]], nt,
                                             preferred_element_type=F32)

    def weighted_values(j, slot):
        row0 = pl.multiple_of(j * tk, tk)
        return [jnp.dot(p_ref[slot, c],
                        jnp.concatenate([v_ref[pl.ds(row0, tk), cols[hh]], ones], axis=1),
                        preferred_element_type=F32) for c, (hh, qq) in enumerate(streams)]

    def accumulate(pv, slot):
        for c in range(len(streams)):
            acc_ref[c] = jnp.concatenate([a_ref[slot, c]] * 2, axis=1) * acc_ref[c] + pv[c]

    def stage(j, slot):
        other = 1 - slot
        logits(jnp.minimum(j + 1, n_kv - 1), other)
        pv = weighted_values(jnp.maximum(j - 1, 0), other)
        for c, (hh, qq) in enumerate(streams):
            for r in range(IDX_TQ // ATT_RG):
                rs = slice(r * ATT_RG, (r + 1) * ATT_RG)
                bias = jnp.concatenate([bias_ref[qq, nb * j + t, rs, :] for t in range(nb)], axis=1)
                sc = s_ref[slot, c, rs, :] + bias.astype(F32)
                m_old = m_ref[c, rs, :]
                m_new = jnp.maximum(m_old, jnp.max(sc, axis=-1, keepdims=True))
                a_ref[slot, c, rs, :] = jnp.exp2(m_old - m_new)
                m_ref[c, rs, :] = m_new
                p_ref[slot, c, rs, :] = jnp.exp2(
                    sc - jnp.concatenate([m_new] * (tk // LANES), axis=1)).astype(BF16)
        accumulate(pv, other)

    logits(0, 0)

    def body(jj, carry):
        stage(2 * jj, 0)
        stage(2 * jj + 1, 1)
        return carry

    lax.fori_loop(0, n_t // 2, body, 0)

    @pl.when(n_t % 2 == 1)
    def _():
        stage(n_t - 1, 0)
        accumulate(weighted_values(n_t - 1, 0), 0)

    @pl.when(n_t % 2 == 0)
    def _():
        accumulate(weighted_values(n_t - 1, 1), 1)

    for c, (hh, qq) in enumerate(streams):
        o_ref[rows[qq], cols[hh]] = (acc_ref[c, :, :DH_A] / acc_ref[c, :, DH_A:]).astype(BF16)


def sparse_attention(proj3, k3, v3, bias5):
    B, S, _ = proj3.shape
    tq, hp = ATT_TQ, ATT_HP
    wb = hp * DH_A
    n_kt = S // IDX_TK
    n_streams = hp * (tq // IDX_TQ)
    return pl.pallas_call(
        _attention_kernel,
        out_shape=jax.ShapeDtypeStruct((B, S, W_A), BF16),
        grid=(B, H_A // hp, S // tq),
        in_specs=[pl.BlockSpec((None, tq, wb), lambda b, h, i: (b, i, OFF_QA // wb + h)),
                  pl.BlockSpec((None, S, wb), lambda b, h, i: (b, 0, h), pipeline_mode=pl.Buffered(1)),
                  pl.BlockSpec((None, S, wb), lambda b, h, i: (b, 0, h), pipeline_mode=pl.Buffered(1)),
                  pl.BlockSpec((None, tq // IDX_TQ, n_kt, IDX_TQ, IDX_TK),
                               lambda b, h, i: (b, i, 0, 0, 0))],
        out_specs=pl.BlockSpec((None, tq, wb), lambda b, h, i: (b, i, h)),
        scratch_shapes=[pltpu.VMEM((n_streams, IDX_TQ, DH_A), BF16),
                        pltpu.VMEM((2, n_streams, IDX_TQ, ATT_TK), F32),
                        pltpu.VMEM((2, n_streams, IDX_TQ, ATT_TK), BF16),
                        pltpu.VMEM((n_streams, IDX_TQ, LANES), F32),
                        pltpu.VMEM((2, n_streams, IDX_TQ, LANES), F32),
                        pltpu.VMEM((n_streams, IDX_TQ, 2 * DH_A), F32)],
        compiler_params=_cparams(("parallel", "parallel", "parallel")),
        name="sparse_attention",
    )(proj3, k3, v3, bias5)


def _mem_attention_kernel(q_ref, km_ref, vm_ref, o_ref):
    scale = DH_M ** -0.5
    for h in range(H_M):
        sl = slice(h * DH_M, (h + 1) * DH_M)
        q = (q_ref[:, sl] * scale).astype(BF16)
        s = lax.dot_general(q, km_ref[:, sl], (((1,), (1,)), ((), ())), preferred_element_type=F32)
        m = jnp.max(s, axis=-1, keepdims=True)
        p = jnp.exp(s - m)
        p = p / jnp.sum(p, axis=-1, keepdims=True)
        o_ref[:, sl] = jnp.dot(p.astype(BF16), vm_ref[:, sl], preferred_element_type=F32).astype(BF16)


def memory_attention(proj3, km3, vm3, tm=512):
    B, S, _ = proj3.shape
    mspec = pl.BlockSpec((None, N_MEM, W_M), lambda b, i: (b, 0, 0))
    return pl.pallas_call(
        _mem_attention_kernel,
        out_shape=jax.ShapeDtypeStruct((B, S, W_M), BF16),
        grid=(B, S // tm),
        in_specs=[pl.BlockSpec((None, tm, W_M), lambda b, i: (b, i, OFF_QM // W_M)), mspec, mspec],
        out_specs=pl.BlockSpec((None, tm, W_M), lambda b, i: (b, i, 0)),
        compiler_params=_cparams(("parallel", "parallel")),
        name="memory_attention",
    )(proj3, km3, vm3)


def _group_sum(x, g_ref):
    hi, lo = _split2(x)
    return (jnp.dot(hi, g_ref[...], preferred_element_type=F32)
            + jnp.dot(lo, g_ref[...], preferred_element_type=F32))


def _rwkv_prep_kernel(*refs, has_vres):
    refs = list(refs)
    r_ref, k_ref, v_ref, l_ref = refs[:4]
    vf_ref = refs[4] if has_vres else None
    (mu_ref, mul_ref, par_ref, wl_ref, g_ref, tri_ref, same_ref,
     ro_ref, ko_ref, vo_ref, go_ref, gl_ref, at_ref, rt_ref, bt_ref, kt_ref, bt2_ref, kt2_ref,
     prev_ref, prevl_ref) = refs[4 + int(has_vres):]
    s_idx = pl.program_id(1)

    @pl.when(s_idx == 0)
    def _():
        prev_ref[...] = jnp.zeros_like(prev_ref)
        prevl_ref[...] = jnp.zeros_like(prevl_ref)

    def shift(x, carry_row, mu):
        tm = x.shape[0]
        rolled = pltpu.roll(x, shift=1, axis=0)
        first = lax.broadcasted_iota(jnp.int32, x.shape, 0) == 0
        prev = jnp.where(first, jnp.broadcast_to(carry_row, x.shape), rolled)
        return x + (prev - x) * mu, x[tm - 1:tm, :]

    r, r_last = shift(r_ref[...], prev_ref[0:1, :], mu_ref[0:1, :])
    k, k_last = shift(k_ref[...], prev_ref[1:2, :], mu_ref[1:2, :])
    v, v_last = shift(v_ref[...], prev_ref[2:3, :], mu_ref[2:3, :])
    lo, l_last = shift(l_ref[...], prevl_ref[0:1, :], mul_ref[...])
    prev_ref[0:1, :] = r_last
    prev_ref[1:2, :] = k_last
    prev_ref[2:3, :] = v_last
    prevl_ref[0:1, :] = l_last

    lane = lax.broadcasted_iota(jnp.int32, lo.shape, 1)
    lt = jnp.where(lane < W_LORA, jnp.tanh(lo),
                   jnp.where((lane >= W_LORA + A_LORA) & (lane < W_LORA + A_LORA + G_LORA),
                             jax.nn.sigmoid(lo), lo))
    up = jnp.dot(lt.astype(BF16), wl_ref[...], preferred_element_type=F32)
    w0, a0, v0 = par_ref[0:1, :], par_ref[1:2, :], par_ref[2:3, :]
    k_k, k_a = par_ref[3:4, :], par_ref[4:5, :]

    y = -(w0 + up[:, 0:W_B])
    softplus = jnp.maximum(y, 0.0) + jnp.log(1.0 + jnp.exp(-jnp.abs(y)))
    w_log = -softplus - 0.5
    lw = -jnp.exp(w_log)
    a = jax.nn.sigmoid(a0 + up[:, W_B:2 * W_B])
    go_ref[...] = up[:, 2 * W_B:3 * W_B]
    if has_vres:
        v = v + (vf_ref[...] - v) * jax.nn.sigmoid(v0 + up[:, 3 * W_B:4 * W_B])
    kk = k * k_k
    norm = jnp.sqrt(_group_sum(kk * kk, g_ref))
    kk = kk / jnp.maximum(norm, 1e-12)
    k = k * (1.0 + (a - 1.0) * k_a)
    ro_ref[...] = r
    ko_ref[...] = k
    vo_ref[...] = v

    p0, p1, p2 = _split3(lw)
    tri, same = tri_ref[...], same_ref[...]
    cs = (jnp.dot(tri, p0, preferred_element_type=F32) + jnp.dot(tri, p1, preferred_element_type=F32)
          + jnp.dot(tri, p2, preferred_element_type=F32))
    cl = (jnp.dot(same, p0, preferred_element_type=F32) + jnp.dot(same, p1, preferred_element_type=F32)
          + jnp.dot(same, p2, preferred_element_type=F32))
    e_neg = jnp.exp(-cs)
    to_end = jnp.exp(cl - cs)
    kka = kk * a
    at_ref[...] = (-kk * jnp.exp(cs - lw)).astype(BF16)
    rt_ref[...] = (r * jnp.exp(cs)).astype(BF16)
    bt_ref[...] = (kka * e_neg).astype(BF16)
    kt_ref[...] = (k * e_neg).astype(BF16)
    bt2_ref[...] = (kka * to_end).astype(BF16)
    kt2_ref[...] = (k * to_end).astype(BF16)
    gl_ref[...] = jnp.exp(cl)


def _chunk_matrices(tm):
    t = np.arange(tm)
    same = (t[:, None] // CHUNK) == (t[None, :] // CHUNK)
    tri = same & (t[None, :] <= t[:, None])
    return jnp.asarray(tri, BF16), jnp.asarray(same, BF16)


def rwkv_prep(proj3, v_first, mu_rkv, mu_lora, params, w_lora, gmat, tm=256):
    B, S, _ = proj3.shape
    has_vres = v_first is not None
    tok = lambda c: pl.BlockSpec((None, tm, W_B), lambda b, s: (b, s, c))
    full = lambda shp: pl.BlockSpec(shp, lambda b, s: (0,) * len(shp))
    in_specs = [tok(OFF_RKV // W_B), tok(OFF_RKV // W_B + 1), tok(OFF_RKV // W_B + 2),
                pl.BlockSpec((None, tm, LORA_W), lambda b, s: (b, s, OFF_LORA // LORA_W))]
    args = [proj3, proj3, proj3, proj3]
    if has_vres:
        in_specs.append(tok(0))
        args.append(v_first)
    in_specs += [full((3, W_B)), full((1, LORA_W)), full((8, W_B)), full((LORA_W, 4 * W_B)),
                 full((W_B, W_B)), full((tm, tm)), full((tm, tm))]
    args += [mu_rkv, mu_lora, params, w_lora, gmat, *_chunk_matrices(tm)]
    out32 = jax.ShapeDtypeStruct((B, S, W_B), F32)
    out16 = jax.ShapeDtypeStruct((B, S, W_B), BF16)
    return pl.pallas_call(
        functools.partial(_rwkv_prep_kernel, has_vres=has_vres),
        out_shape=(out32,) * 5 + (out16,) * 6,
        grid=(B, S // tm),
        in_specs=in_specs,
        out_specs=(tok(0),) * 11,
        scratch_shapes=[pltpu.VMEM((8, W_B), F32), pltpu.VMEM((8, LORA_W), F32)],
        compiler_params=_cparams(("parallel", "arbitrary")),
        name="rwkv_prep",
    )(*args)


SCAN_HPG = 16


def _rwkv_scan_kernel(at_ref, rt_ref, bt_ref, kt_ref, bt2_ref, kt2_ref, v_ref, gl_ref, o_ref,
                      state_ref):
    C, N, H = CHUNK, N_B, SCAN_HPG

    @pl.when(pl.program_id(2) == 0)
    def _():
        state_ref[...] = jnp.zeros_like(state_ref)

    row = lax.broadcasted_iota(jnp.int32, (C, C), 0)
    col = lax.broadcasted_iota(jnp.int32, (C, C), 1)
    incl = (col <= row).astype(F32)
    strict = (col < row).astype(F32)
    eye = (col == row).astype(F32)
    nt = (((1,), (1,)), ((), ()))
    dot = functools.partial(jnp.dot, preferred_element_type=F32)
    dot_nt = lambda x, y: lax.dot_general(x, y, nt, preferred_element_type=F32)
    hs = range(H)
    sl = [slice(h * N, (h + 1) * N) for h in hs]

    x = [jnp.concatenate([at_ref[:, sl[h]], rt_ref[:, sl[h]]], axis=0) for h in hs]
    v = [v_ref[:, sl[h]] for h in hs]
    vb = [v[h].astype(BF16) for h in hs]
    st = [state_ref[h] for h in hs]
    mb = [dot_nt(x[h], bt_ref[:, sl[h]]) for h in hs]
    mk = [dot_nt(x[h], kt_ref[:, sl[h]]) for h in hs]
    xh = [dot_nt(x[h], st[h].astype(BF16)) for h in hs]
    lab = [mb[h][:C] * strict for h in hs]
    mrb = [(mb[h][C:] * incl).astype(BF16) for h in hs]
    lm = [jnp.concatenate([mk[h][:C] * strict, mk[h][C:] * incl], axis=0).astype(BF16) for h in hs]
    lv = [dot(lm[h], vb[h]) for h in hs]
    t = [eye + lab[h] for h in hs]
    p = lab
    for _ in range(5):
        pb = [p[h].astype(BF16) for h in hs]
        p = [dot(pb[h], pb[h]) for h in hs]
        t = [t[h] + dot(t[h].astype(BF16), p[h].astype(BF16)) for h in hs]
    u = [dot(t[h].astype(BF16), (xh[h][:C] + lv[h][:C]).astype(BF16)) for h in hs]
    o = [xh[h][C:] + dot(mrb[h], u[h].astype(BF16)) + lv[h][C:] for h in hs]
    for h in hs:
        o_ref[:, sl[h]] = o[h]
    uv_t = []
    for h in hs:
        uv = jnp.concatenate([u[h], v[h]], axis=0)
        uv_t.append(jnp.concatenate([uv, jnp.zeros_like(uv)], axis=1).T[:N].astype(BF16))
    for h in hs:
        y2 = jnp.concatenate([bt2_ref[:, sl[h]], kt2_ref[:, sl[h]]], axis=0)
        state_ref[h] = st[h] * gl_ref[0:1, sl[h]] + dot(uv_t[h], y2)


def rwkv_scan(at, rt, bt, kt, bt2, kt2, v, gl):
    B, S, _ = v.shape
    wb = SCAN_HPG * N_B
    spec = pl.BlockSpec((None, CHUNK, wb), lambda b, g, c: (b, c, g))
    return pl.pallas_call(
        _rwkv_scan_kernel,
        out_shape=jax.ShapeDtypeStruct((B, S, W_B), F32),
        grid=(B, H_B // SCAN_HPG, S // CHUNK),
        in_specs=[spec] * 8,
        out_specs=spec,
        scratch_shapes=[pltpu.VMEM((SCAN_HPG, N_B, N_B), F32)],
        compiler_params=_cparams(("parallel", "parallel", "arbitrary")),
        name="rwkv_scan",
    )(at, rt, bt, kt, bt2, kt2, v, gl)


def _rwkv_post_kernel(o_ref, r_ref, k_ref, v_ref, g_ref, par_ref, gm_ref, out_ref):
    o = o_ref[...]
    mu = _group_sum(o, gm_ref) * (1.0 / N_B)
    oc = o - mu
    var = _group_sum(oc * oc, gm_ref) * (1.0 / N_B)
    on = oc * lax.rsqrt(var + LNX_EPS) * par_ref[0:1, :] + par_ref[1:2, :]
    bonus = _group_sum(r_ref[...] * k_ref[...] * par_ref[2:3, :], gm_ref) * v_ref[...]
    out_ref[...] = ((on + bonus) * g_ref[...]).astype(BF16)


def rwkv_post(o, r, k, v, g, params, gmat, tm=256):
    M = o.shape[0]
    row = pl.BlockSpec((tm, W_B), lambda i: (i, 0))
    return pl.pallas_call(
        _rwkv_post_kernel,
        out_shape=jax.ShapeDtypeStruct((M, W_B), BF16),
        grid=(M // tm,),
        in_specs=[row] * 5 + [pl.BlockSpec((8, W_B), lambda i: (0, 0)),
                              pl.BlockSpec((W_B, W_B), lambda i: (0, 0))],
        out_specs=row,
        compiler_params=_cparams(("parallel",)),
        name="rwkv_post",
    )(o, r, k, v, g, params, gmat)


def _merge_kernel(oa_ref, ob_ref, oc_ref, wa_ref, wb_ref, wc_ref, g0_ref, g1_ref, g2_ref, out_ref):
    ya = jnp.dot(oa_ref[...], wa_ref[...], preferred_element_type=F32)
    yb = jnp.dot(ob_ref[...], wb_ref[...], preferred_element_type=F32)
    yc = jnp.dot(oc_ref[...], wc_ref[...], preferred_element_type=F32)
    out = (jax.nn.sigmoid(g0_ref[...]) * ya + jax.nn.sigmoid(g1_ref[...]) * yb
           + jax.nn.sigmoid(g2_ref[...]) * yc)
    out_ref[...] = out.astype(BF16)


def gated_merge(o_a, o_b, o_c, w_pa, w_pb, w_pc, proj2, tm=512, tn=1024):
    M = o_a.shape[0]
    D = D_MODEL
    nj = D // tn
    act = lambda w: pl.BlockSpec((tm, w), lambda i, j: (i, 0))
    wgt = lambda w: pl.BlockSpec((w, tn), lambda i, j: (0, j))
    gate = lambda br: pl.BlockSpec((tm, tn), lambda i, j: (i, OFF_GATE // tn + br * nj + j))
    return pl.pallas_call(
        _merge_kernel,
        out_shape=jax.ShapeDtypeStruct((M, D), BF16),
        grid=(M // tm, nj),
        in_specs=[act(W_A), act(W_B), act(W_M), wgt(W_A), wgt(W_B), wgt(W_M),
                  gate(0), gate(1), gate(2)],
        out_specs=pl.BlockSpec((tm, tn), lambda i, j: (i, j)),
        compiler_params=_cparams(("parallel", "arbitrary")),
        name="gated_merge",
    )(o_a, o_b, o_c, w_pa, w_pb, w_pc, proj2, proj2, proj2)


def _first_index_of_max(x, idx, n):
    m = jnp.max(x, axis=0, keepdims=True)
    first = jnp.min(jnp.where(x == m, idx, n), axis=0, keepdims=True)
    return m, idx == first


def _router_kernel(h_ref, wr_ref, bias_ref, gate_ref):
    E, G = N_EXPERTS, N_GROUPS
    per = E // G
    logits = lax.dot_general(wr_ref[...], h_ref[...], (((1,), (1,)), ((), ())),
                             preferred_element_type=F32)
    tm = logits.shape[1]
    scores = jax.nn.sigmoid(logits)
    biased = scores + jnp.concatenate([bias_ref[...]] * (tm // LANES), axis=1)
    idx8 = lax.broadcasted_iota(jnp.int32, (per, tm), 0)
    gscores = []
    for g in range(G):
        xg = biased[g * per:(g + 1) * per, :]
        m1, hit = _first_index_of_max(xg, idx8, per)
        m2 = jnp.max(jnp.where(hit, -jnp.inf, xg), axis=0, keepdims=True)
        gscores.append(m1 + m2)
    gs = jnp.concatenate(gscores, axis=0)
    gidx = lax.broadcasted_iota(jnp.int32, (G, tm), 0)
    gsel = jnp.zeros((G, tm), F32)
    for _ in range(TOPK_GROUPS):
        _, hit = _first_index_of_max(gs, gidx, G)
        gsel = jnp.where(hit, 1.0, gsel)
        gs = jnp.where(hit, -jnp.inf, gs)
    emask = jnp.concatenate(
        [jnp.broadcast_to(gsel[g:g + 1, :], (per, tm)) for g in range(G)], axis=0)
    cand = jnp.where(emask > 0.0, biased, -jnp.inf)
    eidx = lax.broadcasted_iota(jnp.int32, (E, tm), 0)
    picked = jnp.zeros((E, tm), F32)
    for _ in range(TOP_K):
        _, hit = _first_index_of_max(cand, eidx, E)
        picked = jnp.where(hit, scores, picked)
        cand = jnp.where(hit, -jnp.inf, cand)
    gate = picked / jnp.sum(picked, axis=0, keepdims=True) * ROUTED_SCALE
    shared = (eidx == 0).astype(F32)
    gate_ref[...] = jnp.concatenate([gate, shared], axis=0).T


def moe_router(hb, w_router_t, bias_b, tm=256):
    M, D = hb.shape
    return pl.pallas_call(
        _router_kernel,
        out_shape=jax.ShapeDtypeStruct((M, 2 * N_EXPERTS), F32),
        grid=(M // tm,),
        in_specs=[pl.BlockSpec((tm, D), lambda i: (i, 0)),
                  pl.BlockSpec((N_EXPERTS, D), lambda i: (0, 0)),
                  pl.BlockSpec((N_EXPERTS, LANES), lambda i: (0, 0))],
        out_specs=pl.BlockSpec((tm, 2 * N_EXPERTS), lambda i: (i, 0)),
        compiler_params=_cparams(("parallel",)),
        name="moe_router",
    )(hb, w_router_t, bias_b)


MOE_F = 6656


def _moe_act_kernel(x_ref, gate_ref, e_ref, w1_ref, w3_ref, o_ref):
    x = x_ref[...]
    h1 = jnp.dot(x, w1_ref[...], preferred_element_type=F32)
    h3 = jnp.dot(x, w3_ref[...], preferred_element_type=F32)
    g0, g1, g2 = _split3(gate_ref[...])
    e = e_ref[...]
    gexp = (jnp.dot(g0, e, preferred_element_type=F32) + jnp.dot(g1, e, preferred_element_type=F32)
            + jnp.dot(g2, e, preferred_element_type=F32))
    o_ref[...] = (h1 * jax.nn.sigmoid(h1) * h3 * gexp).astype(BF16)


def moe_hidden(hb, gate, expand, w1c, w3c, tm=1024, tn=512):
    M, D = hb.shape
    wspec = pl.BlockSpec((D, tn), lambda i, j: (0, j))
    return pl.pallas_call(
        _moe_act_kernel,
        out_shape=jax.ShapeDtypeStruct((M, MOE_F), BF16),
        grid=(M // tm, MOE_F // tn),
        in_specs=[pl.BlockSpec((tm, D), lambda i, j: (i, 0)),
                  pl.BlockSpec((tm, 2 * N_EXPERTS), lambda i, j: (i, 0)),
                  pl.BlockSpec((2 * N_EXPERTS, tn), lambda i, j: (0, j)),
                  wspec, wspec],
        out_specs=pl.BlockSpec((tm, tn), lambda i, j: (i, j)),
        compiler_params=_cparams(("parallel", "arbitrary")),
        name="moe_hidden",
    )(hb, gate, expand, w1c, w3c)


def _regroup_w_in(w, has_vres):
    D = w.shape[0]
    o = np.cumsum((0, W_A, R_KV, H_IDX * D_IDX, D_IDX, H_IDX, 3 * W_B, W_LORA, A_LORA, G_LORA, W_M,
                   3 * D_MODEL)).tolist()
    q_a, c_kv, q_idx = w[:, o[0]:o[1]], w[:, o[1]:o[2]], w[:, o[2]:o[3]]
    idx = w[:, o[3]:o[5]]
    rkv, lora = w[:, o[5]:o[6]], w[:, o[6]:o[9]]
    q_m, gates = w[:, o[9]:o[10]], w[:, o[10]:o[11]]
    zeros = lambda n: jnp.zeros((D, n), w.dtype)
    vres = w[:, o[11]:o[11] + V_LORA] if has_vres else zeros(V_LORA)
    n_lora = W_LORA + A_LORA + G_LORA + V_LORA
    parts = [gates, q_a, q_idx, rkv, q_m, c_kv, lora, vres, zeros(LORA_W - n_lora),
             idx, zeros(IDX_W - D_IDX - H_IDX)]
    used = sum(p.shape[1] for p in parts)
    parts.append(zeros(N_PROJ - used))
    return jnp.concatenate(parts, axis=1).astype(BF16)


def _lora_up_matrix(w_up, a_up, g_up, v_up):
    m = jnp.zeros((LORA_W, 4 * W_B), F32)
    m = m.at[0:W_LORA, 0:W_B].set(w_up)
    m = m.at[W_LORA:W_LORA + A_LORA, W_B:2 * W_B].set(a_up)
    m = m.at[W_LORA + A_LORA:W_LORA + A_LORA + G_LORA, 2 * W_B:3 * W_B].set(g_up)
    if v_up is not None:
        r0 = W_LORA + A_LORA + G_LORA
        m = m.at[r0:r0 + V_LORA, 3 * W_B:4 * W_B].set(v_up)
    return m.astype(BF16)


def _rows(vectors, n_rows, width):
    m = jnp.zeros((n_rows, width), F32)
    for i, vec in enumerate(vectors):
        m = m.at[i, :vec.shape[0]].set(vec)
    return m


def _expert_in_weights(w_e, w_s):
    D = w_e.shape[1]
    routed = jnp.transpose(w_e, (1, 0, 2)).reshape(D, N_EXPERTS * F_EXPERT)
    pad = jnp.zeros((D, MOE_F - N_EXPERTS * F_EXPERT - w_s.shape[1]), w_e.dtype)
    return jnp.concatenate([routed, w_s, pad], axis=1).astype(BF16)


def _expert_out_weights(w_e2, w_s2):
    D = w_e2.shape[2]
    routed = w_e2.reshape(N_EXPERTS * F_EXPERT, D)
    pad = jnp.zeros((MOE_F - N_EXPERTS * F_EXPERT - w_s2.shape[0], D), w_e2.dtype)
    return jnp.concatenate([routed, w_s2, pad], axis=0).astype(BF16)


def kernel(x, mem, ln0_g, ln0_b, w_in_first, w_in_rest, ckv_g, w_uk, w_uv, mu_rwkv, mu_vres, rw_w0, rw_w_up, rw_a0, rw_a_up, rw_g_up, rw_v0, rw_v_up, rw_k_k, rw_k_a, rw_r_k, rw_lnx_g, rw_lnx_b, w_mk, w_mv, w_pa, w_pb, w_pc, w_o, ln1_g, ln1_b, w_router, router_bias, w_e1, w_e3, w_e2, w_s1, w_s3, w_s2, ln2_g, ln2_b):
    B, S, D = x.shape
    T = B * S
    gmat = jnp.kron(jnp.eye(H_B, dtype=F32), jnp.ones((N_B, N_B), F32)).astype(BF16)
    unit = np.arange(MOE_F)
    owner = np.where(unit < (N_EXPERTS + 1) * F_EXPERT, unit // F_EXPERT, -1)
    expand = jnp.asarray(owner[None, :] == np.arange(2 * N_EXPERTS)[:, None], BF16)
    mem_b = mem.reshape(B * N_MEM, D).astype(BF16)

    h, hb = layer_norm(x.reshape(T, D), ln0_g, ln0_b)
    v_first = None
    for i in range(DEPTH):
        has_vres = i > 0
        w_in = _regroup_w_in(w_in_first if i == 0 else w_in_rest[i - 1], has_vres)
        proj2 = matmul(hb, w_in, F32, 1024, 1024)
        proj3 = proj2.reshape(B, S, N_PROJ)

        k2, v2 = kv_project(proj2, ckv_g[i], w_uk[i].astype(BF16), w_uv[i].astype(BF16))
        bias5 = indexer_bias(proj3)
        o_a = sparse_attention(proj3, k2.reshape(B, S, W_A), v2.reshape(B, S, W_A), bias5)

        mu = mu_rwkv[i]
        mu_rkv = mu[:3 * W_B].reshape(3, W_B)
        mu_l = [mu[3 * W_B:]] + ([mu_vres[i - 1]] if has_vres else [])
        mu_lora = _rows([jnp.concatenate(mu_l)], 1, LORA_W)
        prep_par = _rows([rw_w0[i], rw_a0[i], rw_v0[i - 1] if has_vres else jnp.zeros((W_B,), F32),
                          rw_k_k[i], rw_k_a[i]], 8, W_B)
        w_lora = _lora_up_matrix(rw_w_up[i], rw_a_up[i], rw_g_up[i],
                                 rw_v_up[i - 1] if has_vres else None)
        r_b, k_b, v_b, g_b, gl_b, *scan_ops = rwkv_prep(proj3, v_first, mu_rkv, mu_lora, prep_par,
                                                        w_lora, gmat)
        if i == 0:
            v_first = v_b
        o_scan = rwkv_scan(*scan_ops, v_b, gl_b)
        post_par = _rows([rw_lnx_g[i], rw_lnx_b[i], rw_r_k[i].reshape(W_B)], 8, W_B)
        flat = lambda z: z.reshape(T, W_B)
        o_b = rwkv_post(flat(o_scan), flat(r_b), flat(k_b), flat(v_b), flat(g_b), post_par, gmat)

        km = matmul(mem_b, w_mk[i].astype(BF16), BF16, 512, 512).reshape(B, N_MEM, W_M)
        vm = matmul(mem_b, w_mv[i].astype(BF16), BF16, 512, 512).reshape(B, N_MEM, W_M)
        o_c = memory_attention(proj3, km, vm)

        merged = gated_merge(o_a.reshape(T, W_A), o_b, o_c.reshape(T, W_M), w_pa[i].astype(BF16),
                             w_pb[i].astype(BF16), w_pc[i].astype(BF16), proj2)
        h, hb = matmul_residual_ln(merged, w_o[i].astype(BF16), h, ln1_g[i], ln1_b[i], tn=1024)

        bias_b = jnp.broadcast_to(router_bias[i].reshape(N_EXPERTS, 1), (N_EXPERTS, LANES))
        gate = moe_router(hb, w_router[i].T.astype(BF16), bias_b)
        hidden = moe_hidden(hb, gate, expand, _expert_in_weights(w_e1[i], w_s1[i]),
                            _expert_in_weights(w_e3[i], w_s3[i]))
        h, hb = matmul_residual_ln(hidden, _expert_out_weights(w_e2[i], w_s2[i]), h,
                                   ln2_g[i], ln2_b[i])
    return h.reshape(B, S, D)
```
